```python
import jax, jax.numpy as jnp
from jax import lax
import numpy as np

D_MODEL = 1024
BATCH = 8
SEQ = 4096
DEPTH = 2

CTX_LEN = 256
GRID_W = 64
N_MIXERS = 2
NA_HEADS = 16
NA_HEAD_DIM = D_MODEL // NA_HEADS
NA_WIN_ROWS = 8
NA_WIN_COLS = 16
NA_QBLK_COLS = 16
NA_KBLK_COLS = 32
ML_HEADS = 8
ML_QK_DIM = 64
ML_V_DIM = 128
ML_CHUNK = 64
ML_QK_W = ML_HEADS * ML_QK_DIM
ML_V_W = ML_HEADS * ML_V_DIM
ML_IN_W = 2 * ML_QK_W + 2 * ML_V_W + 4 * ML_HEADS
ROPE_BASE = 10000.0
N_EXPERTS = 16
EC_CAPACITY = 2
EXPERT_HIDDEN = 2 * D_MODEL
EPS = 1e-6
NEG = -1e30
N_NA_LAYERS = (DEPTH + 1) // 2
N_ML_LAYERS = DEPTH // 2

kernel_name = 'hybrid_na_mlstm_ec_moe_dit'


def rms_norm(x, g):
    xf = x.astype(jnp.float32)
    y = xf * lax.rsqrt(jnp.mean(xf * xf, axis=-1, keepdims=True) + EPS)
    return (y * g.astype(jnp.float32)).astype(x.dtype)


def _rope_angles(pos, n_dims):
    inv = ROPE_BASE ** (-jnp.arange(0, n_dims, 2, dtype=jnp.float32) / n_dims)
    return pos.astype(jnp.float32)[:, None] * inv[None, :]


def _rotate(x, ang):
    x1, x2 = jnp.split(x, 2, axis=-1)
    cos = jnp.cos(ang)[None, :, None, :]
    sin = jnp.sin(ang)[None, :, None, :]
    return jnp.concatenate([x1 * cos - x2 * sin, x1 * sin + x2 * cos], axis=-1)


def rope_2d(x):
    t = jnp.arange(x.shape[1])
    half = x.shape[-1] // 2
    xr = _rotate(x[..., :half], _rope_angles(t // GRID_W, half))
    xc = _rotate(x[..., half:], _rope_angles(t % GRID_W, half))
    return jnp.concatenate([xr, xc], axis=-1).astype(x.dtype)


def neighbourhood_attention(h_lat, h_ctx, w_qkv, w_out, rel_bias, with_ctx_out):
    B, n, D = h_lat.shape
    rows = n // GRID_W
    wr = min(NA_WIN_ROWS, rows)
    n_qb = GRID_W // NA_QBLK_COLS
    hd = (NA_HEADS, NA_HEAD_DIM)
    scale = NA_HEAD_DIM ** -0.5
    q, k, v = jnp.split(h_lat @ w_qkv, 3, axis=-1)
    q = (q * scale).reshape((B, rows, GRID_W) + hd)
    k = k.reshape((B, rows, GRID_W) + hd)
    v = v.reshape((B, rows, GRID_W) + hd)
    q_c, k_c, v_c = jnp.split(h_ctx @ w_qkv, 3, axis=-1)
    q_c = (q_c * scale).reshape((B, CTX_LEN) + hd)
    k_c = k_c.reshape((B, CTX_LEN) + hd)
    v_c = v_c.reshape((B, CTX_LEN) + hd)

    q_col = np.arange(GRID_W).reshape(n_qb, NA_QBLK_COLS)
    win_c0 = np.clip(q_col - NA_WIN_COLS // 2, 0, GRID_W - NA_WIN_COLS)
    k_col = np.clip(q_col[:, :1] - NA_WIN_COLS // 2, 0, GRID_W - NA_KBLK_COLS) + np.arange(NA_KBLK_COLS)
    in_win = (k_col[:, None, :] >= win_c0[:, :, None]) & (k_col[:, None, :] < win_c0[:, :, None] + NA_WIN_COLS)
    dc_idx = np.clip(k_col[:, None, :] - q_col[:, :, None] + NA_WIN_COLS - 1, 0, 2 * NA_WIN_COLS - 2)
    col_bias = jnp.where(in_win, rel_bias.astype(jnp.float32)[:, :, dc_idx], NEG)

    def row_block(args):
        r, q_r = args
        r0 = jnp.clip(r - wr // 2, 0, rows - wr)
        k_blk = lax.dynamic_slice_in_dim(k, r0, wr, axis=1)[:, :, k_col]
        v_blk = lax.dynamic_slice_in_dim(v, r0, wr, axis=1)[:, :, k_col]
        q_blk = q_r.reshape((B, n_qb, NA_QBLK_COLS) + hd)
        bias = col_bias[:, r0 + jnp.arange(wr) - r + NA_WIN_ROWS - 1]
        bias = jnp.transpose(bias, (0, 2, 3, 1, 4))[None]
        s_win = jnp.einsum('bjqhd,bwjkhd->bhjqwk', q_blk, k_blk).astype(jnp.float32) + bias
        s_ctx = jnp.einsum('bjqhd,bkhd->bhjqk', q_blk, k_c).astype(jnp.float32)
        n_win = wr * NA_KBLK_COLS
        s = jnp.concatenate([s_win.reshape(B, NA_HEADS, n_qb, NA_QBLK_COLS, n_win), s_ctx], axis=-1)
        p = jax.nn.softmax(s, axis=-1).astype(v.dtype)
        p_win = p[..., :n_win].reshape(B, NA_HEADS, n_qb, NA_QBLK_COLS, wr, NA_KBLK_COLS)
        o = (jnp.einsum('bhjqwk,bwjkhd->bjqhd', p_win, v_blk)
             + jnp.einsum('bhjqk,bkhd->bjqhd', p[..., n_win:], v_c))
        return o.reshape(B, GRID_W, D)

    o_lat = lax.map(row_block, (jnp.arange(rows), jnp.moveaxis(q, 1, 0)))
    y_lat = jnp.moveaxis(o_lat, 0, 1).reshape(B, n, D) @ w_out
    y_ctx = None
    if with_ctx_out:
        s = jnp.einsum('bqhd,bkhd->bhqk', q_c, k_c).astype(jnp.float32)
        p = jax.nn.softmax(s, axis=-1).astype(v_c.dtype)
        y_ctx = jnp.einsum('bhqk,bkhd->bqhd', p, v_c).reshape(B, CTX_LEN, D) @ w_out
    return y_lat, y_ctx


def mlstm_scan(q, k, v, i_pre, logf, state):
    B, G, T, _ = q.shape
    dv = v.shape[-1]
    nc = T // ML_CHUNK
    causal = jnp.tril(jnp.ones((ML_CHUNK, ML_CHUNK), dtype=bool))

    def chunks(t):
        return jnp.moveaxis(t.astype(jnp.float32).reshape((B, G, nc, ML_CHUNK) + t.shape[3:]), 2, 0)

    def step(carry, xs):
        C, nv, m = carry
        qc, kc, vc, ic, fc = xs
        b = jnp.cumsum(fc, axis=-1)
        dmat = jnp.where(causal, b[..., :, None] - b[..., None, :] + ic[..., None, :], NEG)
        inter = b + m[..., None]
        m_row = jnp.maximum(dmat.max(axis=-1), inter)
        w_inter = jnp.exp(inter - m_row)
        s = jnp.einsum('bgtd,bgsd->bgts', qc, kc) * jnp.exp(dmat - m_row[..., None])
        num = jnp.einsum('bgts,bgsv->bgtv', s, vc) + w_inter[..., None] * jnp.einsum('bgtd,bgdv->bgtv', qc, C)
        den = s.sum(axis=-1) + w_inter * jnp.einsum('bgtd,bgd->bgt', qc, nv)
        h = num / jnp.maximum(jnp.abs(den), jnp.exp(-m_row))[..., None]
        b_end = b[..., -1]
        g = b_end[..., None] - b + ic
        m_new = jnp.maximum(b_end + m, g.max(axis=-1))
        w_k = jnp.exp(g - m_new[..., None])
        decay = jnp.exp(b_end + m - m_new)
        C_new = decay[..., None, None] * C + jnp.einsum('bgs,bgsd,bgsv->bgdv', w_k, kc, vc)
        n_new = decay[..., None] * nv + jnp.einsum('bgs,bgsd->bgd', w_k, kc)
        return (C_new, n_new, m_new), h

    state, h = lax.scan(step, state, (chunks(q), chunks(k), chunks(v), chunks(i_pre), chunks(logf)))
    return jnp.moveaxis(h, 0, 2).reshape(B, G, T, dv), state


def _bidir(q, k, v, i_pre, logf):
    B, T = q.shape[:2]

    def heads(t):
        t = jnp.moveaxis(t, 1, 2)
        return jnp.concatenate([t, jnp.flip(t, axis=2)], axis=1)

    def gates(g):
        g = jnp.transpose(g, (0, 2, 3, 1))
        return jnp.stack([g[:, 0], jnp.flip(g[:, 1], axis=-1)], axis=1).reshape(B, 2 * ML_HEADS, T)

    return heads(q), heads(k), heads(v), gates(i_pre), gates(logf)


def mlstm_mixer(h_lat, h_ctx, w_in, b_gate, norm_w, w_out, with_ctx_out):
    B = h_lat.shape[0]
    splits = [ML_QK_W, 2 * ML_QK_W, 2 * ML_QK_W + ML_V_W, 2 * ML_QK_W + 2 * ML_V_W]

    def project(h):
        T = h.shape[1]
        q, k, v, o, g = jnp.split(h @ w_in, splits, axis=-1)
        g = g.astype(jnp.float32) + b_gate.astype(jnp.float32)
        i_pre = g[..., :2 * ML_HEADS].reshape(B, T, 2, ML_HEADS)
        logf = jax.nn.log_sigmoid(g[..., 2 * ML_HEADS:]).reshape(B, T, 2, ML_HEADS)
        q = q.reshape(B, T, ML_HEADS, ML_QK_DIM)
        k = k.reshape(B, T, ML_HEADS, ML_QK_DIM) * (ML_QK_DIM ** -0.5)
        v = v.reshape(B, T, ML_HEADS, ML_V_DIM)
        return q, k, v, o, i_pre, logf

    def readout(h, o):
        T = h.shape[2]
        hs = h[:, :ML_HEADS] + jnp.flip(h[:, ML_HEADS:], axis=2)
        hs = jnp.moveaxis(hs, 1, 2)
        hs = hs * lax.rsqrt(jnp.mean(hs * hs, axis=-1, keepdims=True) + EPS)
        hs = hs.reshape(B, T, ML_V_W) * norm_w.astype(jnp.float32)
        return (hs * jax.nn.sigmoid(o.astype(jnp.float32))).astype(h_lat.dtype) @ w_out

    G = 2 * ML_HEADS
    init = (jnp.zeros((B, G, ML_QK_DIM, ML_V_DIM), jnp.float32),
            jnp.zeros((B, G, ML_QK_DIM), jnp.float32),
            jnp.full((B, G), NEG, jnp.float32))
    qc, kc, vc, oc, ic, fc = project(h_ctx)
    h_c, ctx_state = mlstm_scan(*_bidir(qc, kc, vc, ic, fc), init)
    ql, kl, vl, ol, il, fl = project(h_lat)
    h_l, _ = mlstm_scan(*_bidir(rope_2d(ql), rope_2d(kl), vl, il, fl), ctx_state)
    y_ctx = readout(h_c, oc) if with_ctx_out else None
    return readout(h_l, ol), y_ctx


def expert_choice_ffn(h, w_router, w_gate, w_up, w_down):
    n = h.shape[1]
    cap = EC_CAPACITY * n // N_EXPERTS

    def one_set(hs):
        aff = jax.nn.softmax((hs @ w_router).astype(jnp.float32), axis=-1)
        gate, idx = lax.top_k(aff.T, cap)
        xe = hs[idx]
        a = jnp.einsum('ecd,edf->ecf', xe, w_gate)
        u = jnp.einsum('ecd,edf->ecf', xe, w_up)
        ye = jnp.einsum('ecf,efd->ecd', jax.nn.silu(a) * u, w_down)
        ye = ye * gate[..., None].astype(ye.dtype)
        return jnp.zeros_like(hs).at[idx].add(ye)

    return lax.map(one_set, h)


def setup_inputs(seed: int = 0) -> dict:
    key = jax.random.key(seed)
    ks = jax.random.split(key, 20)
    D, H = D_MODEL, ML_HEADS
    nrm = lambda k, s, sc: jax.random.normal(k, s, jnp.float32) * sc
    f_bias = jnp.tile(jnp.linspace(3.0, 6.0, H, dtype=jnp.float32), 2)[None] + nrm(ks[11], (N_ML_LAYERS, 2 * H), 0.1)
    return {
        'x': nrm(ks[0], (BATCH, SEQ, D), 1.0),
        'c': nrm(ks[1], (BATCH, D), 1.0),
        'ctx': nrm(ks[2], (BATCH, CTX_LEN, D), 1.0),
        'c_ctx': nrm(ks[3], (D,), 1.0),
        'ada_w': nrm(ks[4], (DEPTH, D, 6 * D), 0.5 * D ** -0.5),
        'ada_b': nrm(ks[5], (DEPTH, 6 * D), 0.02),
        'norm_g': 1.0 + nrm(ks[6], (DEPTH, 4, D), 0.02),
        'na_w_qkv': nrm(ks[7], (N_NA_LAYERS, D, 3 * D), D ** -0.5),
        'na_w_out': nrm(ks[8], (N_NA_LAYERS, D, D), D ** -0.5),
        'na_rel_bias': nrm(ks[9], (N_NA_LAYERS, NA_HEADS, 2 * NA_WIN_ROWS - 1, 2 * NA_WIN_COLS - 1), 0.1),
        'ml_w_in': nrm(ks[10], (N_ML_LAYERS, D, ML_IN_W), D ** -0.5),
        'ml_b_gate': jnp.concatenate([nrm(ks[12], (N_ML_LAYERS, 2 * H), 0.1), f_bias], axis=-1),
        'ml_norm_w': 1.0 + nrm(ks[13], (N_ML_LAYERS, ML_V_W), 0.02),
        'ml_w_out': nrm(ks[14], (N_ML_LAYERS, ML_V_W, D), ML_V_W ** -0.5),
        'moe_w_router': nrm(ks[15], (DEPTH, D, N_EXPERTS), D ** -0.5),
        'moe_w_gate': nrm(ks[16], (DEPTH, N_EXPERTS, D, EXPERT_HIDDEN), D ** -0.5),
        'moe_w_up': nrm(ks[17], (DEPTH, N_EXPERTS, D, EXPERT_HIDDEN), D ** -0.5),
        'moe_w_down': nrm(ks[18], (DEPTH, N_EXPERTS, EXPERT_HIDDEN, D), EXPERT_HIDDEN ** -0.5),
    }


def reference(x, c, ctx, c_ctx, ada_w, ada_b, norm_g, na_w_qkv, na_w_out, na_rel_bias,
              ml_w_in, ml_b_gate, ml_norm_w, ml_w_out, moe_w_router, moe_w_gate, moe_w_up, moe_w_down):
    xc = ctx
    for l in range(DEPTH):
        last = l == DEPTH - 1
        j = l // N_MIXERS
        g = norm_g[l]
        sh_m, sc_m, gt_m, sh_f, sc_f, gt_f = jnp.split((jax.nn.silu(c) @ ada_w[l] + ada_b[l])[:, None, :], 6, axis=-1)
        csh_m, csc_m, cgt_m, csh_f, csc_f, cgt_f = jnp.split(jax.nn.silu(c_ctx) @ ada_w[l] + ada_b[l], 6, axis=-1)
        h = rms_norm(x, g[0]) * (1.0 + sc_m) + sh_m
        hc = rms_norm(xc, g[0]) * (1.0 + csc_m) + csh_m
        if l % N_MIXERS == 0:
            y, yc = neighbourhood_attention(h, hc, na_w_qkv[j], na_w_out[j], na_rel_bias[j], not last)
        else:
            y, yc = mlstm_mixer(h, hc, ml_w_in[j], ml_b_gate[j], ml_norm_w[j], ml_w_out[j], not last)
        x = x + gt_m * rms_norm(y, g[1])
        h = rms_norm(x, g[2]) * (1.0 + sc_f) + sh_f
        x = x + gt_f * rms_norm(expert_choice_ffn(h, moe_w_router[l], moe_w_gate[l], moe_w_up[l], moe_w_down[l]), g[3])
        if not last:
            xc = xc + cgt_m * rms_norm(yc, g[1])
            hc = rms_norm(xc, g[2]) * (1.0 + csc_f) + csh_f
            xc = xc + cgt_f * rms_norm(expert_choice_ffn(hc, moe_w_router[l], moe_w_gate[l], moe_w_up[l], moe_w_down[l]), g[3])
    return x
```

```python
import functools

import numpy as np
import jax
import jax.numpy as jnp
from jax import lax
from jax.experimental import pallas as pl
from jax.experimental.pallas import tpu as pltpu

F32 = jnp.float32
BF16 = jnp.bfloat16

LANE = 128
SUBLANE = 8
VMEM_LIMIT = 56 * 1024 * 1024

EPS = 1e-6
NEG = -1e30
GRID_W = 64
HEAD_DIM = 64
ML_V_DIM = 128
WIN_ROWS = 8
WIN_COLS = 16
ROPE_BASE = 10000.0
EC_CAPACITY = 2

TM = 256
Q_ROWS = TM // GRID_W
K_ROWS = Q_ROWS + WIN_ROWS
ML_CHUNK = 256


def _cp(sem):
    return pltpu.CompilerParams(dimension_semantics=sem, vmem_limit_bytes=VMEM_LIMIT)


def _dot(a, b):
    return jnp.dot(a, b, preferred_element_type=F32)


def _dot_nt(a, b):
    return lax.dot_general(a, b, (((1,), (1,)), ((), ())), preferred_element_type=F32)


def _split3(x):
    x1 = x.astype(BF16)
    r1 = x - x1.astype(F32)
    x2 = r1.astype(BF16)
    x3 = (r1 - x2.astype(F32)).astype(BF16)
    return x1, x2, x3


def _rms(x, g):
    return x * lax.rsqrt(jnp.mean(x * x, axis=-1, keepdims=True) + EPS) * g


def _silu(x):
    return x * jax.nn.sigmoid(x)


def _ada_kernel(c_ref, w_ref, b_ref, o_ref):
    s = _silu(c_ref[...]).astype(BF16)
    o_ref[0] = _dot(s, w_ref[0].astype(BF16)) + b_ref[0]


def _ada(c_all, ada_w, ada_b):
    depth, d, n = ada_w.shape
    rows = c_all.shape[0]
    tn = 1024 if n % 1024 == 0 else n
    return pl.pallas_call(
        _ada_kernel,
        grid=(depth, n // tn),
        in_specs=[pl.BlockSpec((rows, d), lambda l, j: (0, 0)),
                  pl.BlockSpec((1, d, tn), lambda l, j: (l, 0, j)),
                  pl.BlockSpec((1, 1, tn), lambda l, j: (l, 0, j))],
        out_specs=pl.BlockSpec((1, rows, tn), lambda l, j: (l, 0, j)),
        out_shape=jax.ShapeDtypeStruct((depth, rows, n), F32),
        compiler_params=_cp(("arbitrary", "arbitrary")),
        name="ada_mod",
    )(c_all, ada_w, ada_b.reshape(depth, 1, n))


def _mod_spec(batch, n_ctx_tiles, d):
    return pl.BlockSpec((1, 6, d), lambda b, i: (jnp.where(i < n_ctx_tiles, batch, b), 0, 0))


def _proj_na_kernel(x_ref, mod_ref, g_ref, w_ref, o_ref):
    m = mod_ref[0]
    h = (_rms(x_ref[0], g_ref[0:1, :]) * (1.0 + m[1:2, :]) + m[0:1, :]).astype(BF16)
    o_ref[0] = _dot(h, w_ref[...]).astype(BF16)


def _proj_na(xa, mod, g, w_bf16, n_ctx_tiles):
    batch, tt, d = xa.shape
    n = w_bf16.shape[1]
    return pl.pallas_call(
        _proj_na_kernel,
        grid=(batch, tt // TM),
        in_specs=[pl.BlockSpec((1, TM, d), lambda b, i: (b, i, 0)),
                  _mod_spec(batch, n_ctx_tiles, d),
                  pl.BlockSpec((4, d), lambda b, i: (0, 0)),
                  pl.BlockSpec((d, n), lambda b, i: (0, 0))],
        out_specs=pl.BlockSpec((1, TM, n), lambda b, i: (b, i, 0)),
        out_shape=jax.ShapeDtypeStruct((batch, tt, n), BF16),
        compiler_params=_cp(("arbitrary", "arbitrary")),
        name="proj_na",
    )(xa, mod, g, w_bf16)


def _proj_ml_kernel(x_ref, mod_ref, g_ref, w_ref, bg_ref, cos_ref, sin_ref, o_ref, gate_ref, *, qk_w, main_w):
    m = mod_ref[0]
    h = (_rms(x_ref[0], g_ref[0:1, :]) * (1.0 + m[1:2, :]) + m[0:1, :]).astype(BF16)
    r = _dot(h, w_ref[...])
    qk = r[:, :2 * qk_w]
    reps = 2 * qk_w // LANE
    cos = jnp.concatenate([cos_ref[...]] * reps, axis=1)
    sin = jnp.concatenate([sin_ref[...]] * reps, axis=1)
    lane = lax.broadcasted_iota(jnp.int32, qk.shape, 1)
    partner = jnp.where(lane % 32 < 16, pltpu.roll(qk, 2 * qk_w - 16, 1), pltpu.roll(qk, 16, 1))
    qk = qk * cos + partner * sin
    o_ref[0, :, :2 * qk_w] = qk.astype(BF16)
    o_ref[0, :, 2 * qk_w:] = r[:, 2 * qk_w:main_w].astype(BF16)
    gr = r[:, main_w:] + bg_ref[...]
    gate_ref[0, :, :LANE] = gr[:, :LANE]
    f = gr[:, LANE:]
    gate_ref[0, :, LANE:] = jnp.minimum(f, 0.0) - jnp.log1p(jnp.exp(-jnp.abs(f)))


def _proj_ml(xa, mod, g, w_bf16, bg, cos_t, sin_t, n_ctx_tiles, qk_w, main_w):
    batch, tt, d = xa.shape
    n = w_bf16.shape[1]
    kern = functools.partial(_proj_ml_kernel, qk_w=qk_w, main_w=main_w)
    return pl.pallas_call(
        kern,
        grid=(batch, tt // TM),
        in_specs=[pl.BlockSpec((1, TM, d), lambda b, i: (b, i, 0)),
                  _mod_spec(batch, n_ctx_tiles, d),
                  pl.BlockSpec((4, d), lambda b, i: (0, 0)),
                  pl.BlockSpec((d, n), lambda b, i: (0, 0)),
                  pl.BlockSpec((1, 2 * LANE), lambda b, i: (0, 0)),
                  pl.BlockSpec((TM, LANE), lambda b, i: (i, 0)),
                  pl.BlockSpec((TM, LANE), lambda b, i: (i, 0))],
        out_specs=[pl.BlockSpec((1, TM, main_w), lambda b, i: (b, i, 0)),
                   pl.BlockSpec((1, TM, 2 * LANE), lambda b, i: (b, i, 0))],
        out_shape=[jax.ShapeDtypeStruct((batch, tt, main_w), BF16),
                   jax.ShapeDtypeStruct((batch, tt, 2 * LANE), F32)],
        compiler_params=_cp(("arbitrary", "arbitrary")),
        name="proj_ml",
    )(xa, mod, g, w_bf16, bg, cos_t, sin_t)


def _na_kernel(q_ref, k_ref, v_ref, bias_ref, o_ref, *, ctx, rows):
    step = pl.program_id(2)
    n_keys = K_ROWS * GRID_W

    def attend(s_list, v_list):
        m = s_list[0].max(axis=1, keepdims=True)
        for s in s_list[1:]:
            m = jnp.maximum(m, s.max(axis=1, keepdims=True))
        num, den = 0.0, 0.0
        for s, v in zip(s_list, v_list):
            p = jnp.exp(s - m)
            den = den + p.sum(axis=1, keepdims=True)
            num = num + _dot(p.astype(BF16), v)
        return num / den

    @pl.when(step == 0)
    def _():
        outs = []
        for hh in range(2):
            sl = slice(hh * HEAD_DIM, (hh + 1) * HEAD_DIM)
            q = q_ref[0, :, sl]
            s_c = _dot_nt(q, k_ref[0, 0:ctx, sl])
            outs.append(attend([s_c], [v_ref[0, 0:ctx, sl]]))
        o_ref[0] = jnp.concatenate(outs, axis=1).astype(BF16)

    @pl.when(step > 0)
    def _():
        rb = step - 1
        start = jnp.clip(Q_ROWS * rb - WIN_ROWS // 2, 0, rows - K_ROWS)
        ks = pl.multiple_of(ctx + start * GRID_W, TM)
        outs = []
        for hh in range(2):
            sl = slice(hh * HEAD_DIM, (hh + 1) * HEAD_DIM)
            q = q_ref[0, :, sl]
            s_w = _dot_nt(q, k_ref[0, pl.ds(ks, n_keys), sl]) + bias_ref[hh, 0]
            s_c = _dot_nt(q, k_ref[0, 0:ctx, sl])
            outs.append(attend([s_w, s_c], [v_ref[0, pl.ds(ks, n_keys), sl], v_ref[0, 0:ctx, sl]]))
        o_ref[0] = jnp.concatenate(outs, axis=1).astype(BF16)


def _na_bias_table(rel_bias, rows):
    n_rb = rows // Q_ROWS
    tabs = []
    for rb in (0, 1, n_rb - 1):
        start = int(np.clip(Q_ROWS * rb - WIN_ROWS // 2, 0, rows - K_ROWS))
        qr = Q_ROWS * rb + np.arange(Q_ROWS)
        kr = start + np.arange(K_ROWS)
        r0 = np.clip(qr - WIN_ROWS // 2, 0, rows - WIN_ROWS)
        row_ok = (kr[None, :] >= r0[:, None]) & (kr[None, :] < r0[:, None] + WIN_ROWS)
        dr = np.clip(kr[None, :] - qr[:, None] + WIN_ROWS - 1, 0, 2 * WIN_ROWS - 2)
        qc = np.arange(GRID_W)
        c0 = np.clip(qc - WIN_COLS // 2, 0, GRID_W - WIN_COLS)
        col_ok = (qc[None, :] >= c0[:, None]) & (qc[None, :] < c0[:, None] + WIN_COLS)
        dc = np.clip(qc[None, :] - qc[:, None] + WIN_COLS - 1, 0, 2 * WIN_COLS - 2)
        ok = row_ok[:, None, :, None] & col_ok[None, :, None, :]
        vals = rel_bias.astype(F32)[:, dr[:, None, :, None], dc[None, :, None, :]]
        tabs.append(jnp.where(ok[None], vals, NEG).reshape(rel_bias.shape[0], TM, K_ROWS * GRID_W))
    return jnp.stack(tabs, axis=1)


def _na(qkv, bias_tab, ctx, rows):
    batch, tt, d3 = qkv.shape
    d = d3 // 3
    hp = d // LANE
    n_rb = rows // Q_ROWS
    n_ct = ctx // TM
    nk = K_ROWS * GRID_W

    def case(s):
        rb = s - 1
        return jnp.where(rb <= 0, 0, jnp.where(rb == n_rb - 1, 2, 1))

    kern = functools.partial(_na_kernel, ctx=ctx, rows=rows)
    return pl.pallas_call(
        kern,
        grid=(hp, batch, n_rb + 1),
        in_specs=[pl.BlockSpec((1, TM, LANE), lambda h, b, s: (b, jnp.where(s == 0, 0, s - 1 + n_ct), h)),
                  pl.BlockSpec((1, tt, LANE), lambda h, b, s: (b, 0, hp + h)),
                  pl.BlockSpec((1, tt, LANE), lambda h, b, s: (b, 0, 2 * hp + h)),
                  pl.BlockSpec((2, 1, TM, nk), lambda h, b, s: (h, case(s), 0, 0))],
        out_specs=pl.BlockSpec((1, TM, LANE), lambda h, b, s: (b, jnp.where(s == 0, 0, s - 1 + n_ct), h)),
        out_shape=jax.ShapeDtypeStruct((batch, tt, d), BF16),
        compiler_params=_cp(("arbitrary", "arbitrary", "arbitrary")),
        name="na_attention",
    )(qkv, qkv, qkv, bias_tab)


def _scan_kernel(q_ref, k_ref, v_ref, gate_ref, o_ref, c_ref, m_ref, *, heads, reverse):
    L = ML_CHUNK
    j = pl.program_id(1)

    @pl.when(j == 0)
    def _():
        c_ref[...] = jnp.zeros_like(c_ref)
        m_ref[...] = jnp.full_like(m_ref, NEG)

    row = lax.broadcasted_iota(jnp.int32, (L, L), 0)
    col = lax.broadcasted_iota(jnp.int32, (L, L), 1)
    mask = (row <= col) if reverse else (row >= col)
    tri = jnp.where(mask, 1.0, 0.0).astype(BF16)

    gi = gate_ref[0, :, :LANE]
    gf = gate_ref[0, :, LANE:]
    f1, f2, f3 = _split3(gf)
    b_col = _dot(tri, f1) + _dot(tri, f2) + _dot(tri, f3)
    b_row = b_col.T
    i_row = gi.T
    last = 0 if reverse else L - 1
    b_end = b_col[last:last + 1, :]
    k_t = k_ref[0].astype(F32).T
    m_all = m_ref[...]
    lane1 = lax.broadcasted_iota(jnp.int32, (1, LANE), 1)
    ones_col = jnp.where(lax.broadcasted_iota(jnp.int32, (L, LANE), 1) == 0, 1.0, 0.0).astype(BF16)

    for h in range(heads):
        c = (heads if reverse else 0) + h
        bc = b_col[:, c:c + 1]
        br = b_row[c:c + 1, :]
        ir = i_row[c:c + 1, :]
        mh = m_all[:, c:c + 1]
        be = b_end[:, c:c + 1]
        dmat = jnp.where(mask, bc - br + ir, NEG)
        inter = bc + mh
        m_t = jnp.maximum(dmat.max(axis=1, keepdims=True), inter)
        p = jnp.exp(dmat - m_t)
        w_inter = jnp.exp(inter - m_t)
        q = q_ref[0, :, h * HEAD_DIM:(h + 1) * HEAD_DIM]
        k = k_ref[0, :, h * HEAD_DIM:(h + 1) * HEAD_DIM]
        v_aug = jnp.concatenate([v_ref[0, :, h * ML_V_DIM:(h + 1) * ML_V_DIM], ones_col], axis=1)
        s = (_dot_nt(q, k) * p).astype(BF16)
        state = c_ref[h]
        r = _dot(s, v_aug) + w_inter * _dot(q, state.astype(BF16))
        den = r[:, ML_V_DIM:ML_V_DIM + 1]
        o_ref[0, :, h * ML_V_DIM:(h + 1) * ML_V_DIM] = r[:, :ML_V_DIM] / jnp.maximum(jnp.abs(den), jnp.exp(-m_t))
        g_row = be - br + ir
        m_new = jnp.maximum(be + mh, g_row.max(axis=1, keepdims=True))
        w_k = jnp.exp(g_row - m_new)
        decay = jnp.exp(be + mh - m_new)
        kw = (k_t[h * HEAD_DIM:(h + 1) * HEAD_DIM, :] * w_k).astype(BF16)
        c_ref[h] = decay * state + _dot(kw, v_aug)
        m_all = jnp.where(lane1 == c, m_new, m_all)
    m_ref[...] = m_all


def _scan(proj, gates, heads, ctx, reverse):
    batch, tt, _ = proj.shape
    L = ML_CHUNK
    nc = tt // L
    ncc = ctx // L
    qk_w = heads * HEAD_DIM
    v_w = heads * ML_V_DIM

    def chunk(j):
        if not reverse:
            return j
        return jnp.where(j < ncc, ncc - 1 - j, nc - 1 - (j - ncc))

    kern = functools.partial(_scan_kernel, heads=heads, reverse=reverse)
    return pl.pallas_call(
        kern,
        grid=(batch, nc),
        in_specs=[pl.BlockSpec((1, L, qk_w), lambda b, j: (b, chunk(j), 0)),
                  pl.BlockSpec((1, L, qk_w), lambda b, j: (b, chunk(j), 1)),
                  pl.BlockSpec((1, L, v_w), lambda b, j: (b, chunk(j), 2 * qk_w // v_w)),
                  pl.BlockSpec((1, L, 2 * LANE), lambda b, j: (b, chunk(j), 0))],
        out_specs=pl.BlockSpec((1, L, v_w), lambda b, j: (b, chunk(j), 0)),
        out_shape=jax.ShapeDtypeStruct((batch, tt, v_w), F32),
        scratch_shapes=[pltpu.VMEM((heads, HEAD_DIM, 2 * ML_V_DIM), F32),
                        pltpu.VMEM((1, LANE), F32)],
        compiler_params=_cp(("arbitrary", "arbitrary")),
        name="mlstm_scan_bwd" if reverse else "mlstm_scan_fwd",
    )(proj, proj, proj, gates)


def _post_common(y_in, w_ref, xa_ref, mod_ref, g_ref, wr_ref, x1_ref, h2s_ref, lg_ref):
    m = mod_ref[0]
    y = _dot(y_in, w_ref[...])
    x1 = xa_ref[0] + m[2:3, :] * _rms(y, g_ref[1:2, :])
    x1_ref[0] = x1
    h2 = _rms(x1, g_ref[2:3, :]) * (1.0 + m[4:5, :]) + m[3:4, :]
    d = h2.shape[1]
    p = d // LANE
    for jj in range(p):
        h2s_ref[0, pl.ds(jj, TM, stride=p), :] = h2[:, jj * LANE:(jj + 1) * LANE]
    h_hi = h2.astype(BF16)
    h_lo = (h2 - h_hi.astype(F32)).astype(BF16)
    lg_ref[0] = _dot(h_hi, wr_ref[0]) + _dot(h_hi, wr_ref[1]) + _dot(h_lo, wr_ref[0])


def _post_na_kernel(o_ref, w_ref, xa_ref, mod_ref, g_ref, wr_ref, x1_ref, h2s_ref, lg_ref):
    _post_common(o_ref[0], w_ref, xa_ref, mod_ref, g_ref, wr_ref, x1_ref, h2s_ref, lg_ref)


def _post_ml_kernel(hf_ref, hb_ref, og_ref, nw_ref, w_ref, xa_ref, mod_ref, g_ref, wr_ref,
                    x1_ref, h2s_ref, lg_ref, *, heads):
    hs = hf_ref[0] + hb_ref[0]
    parts = []
    for h in range(heads):
        t = hs[:, h * ML_V_DIM:(h + 1) * ML_V_DIM]
        parts.append(t * lax.rsqrt(jnp.mean(t * t, axis=-1, keepdims=True) + EPS))
    hn = jnp.concatenate(parts, axis=1) * nw_ref[...]
    y_in = (hn * jax.nn.sigmoid(og_ref[0].astype(F32))).astype(BF16)
    _post_common(y_in, w_ref, xa_ref, mod_ref, g_ref, wr_ref, x1_ref, h2s_ref, lg_ref)


def _post(mixer_inputs, w_out, xa, mod, g, wr, n_ctx_tiles, ml_heads=None):
    batch, tt, d = xa.shape
    p = d // LANE
    tile = lambda w: pl.BlockSpec((1, TM, w), lambda b, i: (b, i, 0))
    full2 = lambda a: pl.BlockSpec(a.shape, lambda b, i: (0,) * a.ndim)
    if ml_heads is None:
        (o,) = mixer_inputs
        kern = _post_na_kernel
        head_specs, head_args = [tile(d)], [o]
    else:
        hf, hb, proj, nw = mixer_inputs
        v_w = ml_heads * ML_V_DIM
        og_block = (2 * ml_heads * HEAD_DIM + v_w) // v_w
        kern = functools.partial(_post_ml_kernel, heads=ml_heads)
        head_specs = [tile(v_w), tile(v_w), pl.BlockSpec((1, TM, v_w), lambda b, i: (b, i, og_block)), full2(nw)]
        head_args = [hf, hb, proj, nw]
    return pl.pallas_call(
        kern,
        grid=(batch, tt // TM),
        in_specs=head_specs + [full2(w_out), tile(d), _mod_spec(batch, n_ctx_tiles, d), full2(g), full2(wr)],
        out_specs=[tile(d),
                   pl.BlockSpec((1, TM * p, LANE), lambda b, i: (b, i, 0)),
                   tile(LANE)],
        out_shape=[jax.ShapeDtypeStruct((batch, tt, d), F32),
                   jax.ShapeDtypeStruct((batch, tt * p, LANE), F32),
                   jax.ShapeDtypeStruct((batch, tt, LANE), F32)],
        compiler_params=_cp(("arbitrary", "arbitrary")),
        name="post_mixer",
    )(*head_args, w_out, xa, mod, g, wr)


def _route_kernel(lg_ref, idx_ref, pos_ref, col_ref, *, n_exp, cap):
    t = lg_ref.shape[1]
    n_tiles = t // LANE
    lt = lg_ref[0].T[:n_exp, :]
    e = jnp.exp(lt - lt.max(axis=0, keepdims=True))
    aff = e / e.sum(axis=0, keepdims=True)
    bits = pltpu.bitcast(aff, jnp.int32)

    def bisect(i, cur):
        cand = cur | jnp.left_shift(jnp.int32(1), 30 - i)
        cnt = jnp.where(bits >= cand, 1.0, 0.0).sum(axis=1, keepdims=True)
        return jnp.where(cnt >= cap, cand, cur)

    thr = lax.fori_loop(0, 31, bisect, jnp.zeros((n_exp, 1), jnp.int32))
    gt = bits > thr
    eq = bits == thr
    need = cap - jnp.where(gt, 1.0, 0.0).sum(axis=1, keepdims=True)

    upper = jnp.where(lax.broadcasted_iota(jnp.int32, (LANE, LANE), 0)
                      <= lax.broadcasted_iota(jnp.int32, (LANE, LANE), 1), 1.0, 0.0).astype(BF16)

    def excl_cumsum(mask_f32):
        carry = jnp.zeros((n_exp, 1), F32)
        parts = []
        for jt in range(n_tiles):
            tile = mask_f32[:, jt * LANE:(jt + 1) * LANE]
            inc = _dot(tile.astype(BF16), upper)
            parts.append(inc - tile + carry)
            carry = carry + inc[:, LANE - 1:LANE]
        return jnp.concatenate(parts, axis=1)

    eq_f = jnp.where(eq, 1.0, 0.0)
    sel = gt | (eq & (excl_cumsum(eq_f) < need))
    sel_f = jnp.where(sel, 1.0, 0.0)
    pos_ref[...] = jnp.where(sel, excl_cumsum(sel_f), -1.0)

    t_iota = lax.broadcasted_iota(jnp.int32, (SUBLANE, LANE), 1).astype(F32)
    r_iota = lax.broadcasted_iota(jnp.int32, (SUBLANE, LANE), 0).astype(F32)
    col_ref[...] = jnp.zeros_like(col_ref)
    for ex in range(n_exp):
        def ranks(rc, _):
            r0 = (rc * SUBLANE).astype(F32)
            acc = jnp.zeros((SUBLANE, LANE), F32)
            for jt in range(n_tiles):
                prow = pos_ref[ex:ex + 1, jt * LANE:(jt + 1) * LANE]
                acc = acc + jnp.where(prow == r_iota + r0, t_iota + float(jt * LANE), 0.0)
            col_ref[pl.ds(pl.multiple_of(rc * SUBLANE, SUBLANE), SUBLANE), ex:ex + 1] = acc.sum(axis=1, keepdims=True)
            return 0
        lax.fori_loop(0, cap // SUBLANE, ranks, 0)
    idx_ref[0] = col_ref[...].T[:n_exp, :cap].astype(jnp.int32)


def _route(logits, n_exp, cap):
    batch, t, _ = logits.shape
    kern = functools.partial(_route_kernel, n_exp=n_exp, cap=cap)
    return pl.pallas_call(
        kern,
        grid=(batch,),
        in_specs=[pl.BlockSpec((1, t, LANE), lambda b: (b, 0, 0))],
        out_specs=pl.BlockSpec((1, n_exp, cap), lambda b: (b, 0, 0)),
        out_shape=jax.ShapeDtypeStruct((batch, n_exp, cap), jnp.int32),
        scratch_shapes=[pltpu.VMEM((n_exp, t), F32), pltpu.VMEM((-(-cap // LANE) * LANE, LANE), F32)],
        compiler_params=_cp(("arbitrary",)),
        name="route",
    )(logits)


def _gather_kernel(idx_ref, src_ref, lg_ref, xe_ref, gate_ref, tile_ref, gl_ref, *, n_exp, slots, p, stride):
    unroll = 8

    def body(c, _):
        for u in range(unroll):
            r = c * unroll + u
            t = idx_ref[0, 0, 0, r]
            tile_ref[pl.ds(r, p, stride=stride), :] = src_ref[0, pl.ds(pl.multiple_of(t * p, p), p), :]
            gl_ref[pl.ds(r, 1), :] = lg_ref[0, pl.ds(t, 1), :]
        return 0

    lax.fori_loop(0, slots // unroll, body, 0)
    xe_ref[0, 0] = jnp.concatenate([tile_ref[pl.ds(jj * stride, slots), :] for jj in range(p)], axis=1).astype(BF16)
    lg = gl_ref[...]
    lane = lax.broadcasted_iota(jnp.int32, lg.shape, 1)
    lg = jnp.where(lane < n_exp, lg, NEG)
    ex = jnp.exp(lg - lg.max(axis=1, keepdims=True))
    gate_ref[0, 0] = ex / ex.sum(axis=1, keepdims=True)


def _gather(idx, h2s, logits, n_exp):
    batch, _, _, slots = idx.shape
    tt = logits.shape[1]
    p = h2s.shape[1] // tt
    d = p * LANE
    stride = slots + SUBLANE
    kern = functools.partial(_gather_kernel, n_exp=n_exp, slots=slots, p=p, stride=stride)
    return pl.pallas_call(
        kern,
        grid=(batch, n_exp),
        in_specs=[pl.BlockSpec((1, 1, 1, slots), lambda b, e: (b, e, 0, 0), memory_space=pltpu.SMEM),
                  pl.BlockSpec((1, tt * p, LANE), lambda b, e: (b, 0, 0), pipeline_mode=pl.Buffered(1)),
                  pl.BlockSpec((1, tt, LANE), lambda b, e: (b, 0, 0))],
        out_specs=[pl.BlockSpec((1, 1, slots, d), lambda b, e: (e, b, 0, 0)),
                   pl.BlockSpec((1, 1, slots, LANE), lambda b, e: (e, b, 0, 0))],
        out_shape=[jax.ShapeDtypeStruct((n_exp, batch, slots, d), BF16),
                   jax.ShapeDtypeStruct((n_exp, batch, slots, LANE), F32)],
        scratch_shapes=[pltpu.VMEM((p * stride, LANE), F32), pltpu.VMEM((slots, LANE), F32)],
        compiler_params=_cp(("arbitrary", "arbitrary")),
        name="moe_gather",
    )(idx, h2s, logits)


def _ffn_kernel(xe_ref, gate_ref, wg_ref, wu_ref, wd_ref, ye_ref, *, f_chunk):
    e = pl.program_id(0)
    x = xe_ref[0, 0]
    slots, d = x.shape
    f = wg_ref.shape[2]
    y = jnp.zeros((slots, d), F32)
    for c in range(f // f_chunk):
        cs = slice(c * f_chunk, (c + 1) * f_chunk)
        a = _dot(x, wg_ref[0, :, cs])
        u = _dot(x, wu_ref[0, :, cs])
        y = y + _dot((_silu(a) * u).astype(BF16), wd_ref[0, cs, :])
    gt = gate_ref[0, 0]
    lane = lax.broadcasted_iota(jnp.int32, gt.shape, 1)
    y = y * jnp.where(lane == e, gt, 0.0).sum(axis=1, keepdims=True)
    p = d // LANE
    for jj in range(p):
        ye_ref[0, 0, pl.ds(jj, slots, stride=p), :] = y[:, jj * LANE:(jj + 1) * LANE]


def _ffn(xe, gates, wg, wu, wd):
    n_exp, batch, slots, d = xe.shape
    f = wg.shape[2]
    p = d // LANE
    kern = functools.partial(_ffn_kernel, f_chunk=min(f, 512))
    return pl.pallas_call(
        kern,
        grid=(n_exp, batch),
        in_specs=[pl.BlockSpec((1, 1, slots, d), lambda e, b: (e, b, 0, 0)),
                  pl.BlockSpec((1, 1, slots, LANE), lambda e, b: (e, b, 0, 0)),
                  pl.BlockSpec((1, d, f), lambda e, b: (e, 0, 0)),
                  pl.BlockSpec((1, d, f), lambda e, b: (e, 0, 0)),
                  pl.BlockSpec((1, f, d), lambda e, b: (e, 0, 0))],
        out_specs=pl.BlockSpec((1, 1, slots * p, LANE), lambda e, b: (e, b, 0, 0)),
        out_shape=jax.ShapeDtypeStruct((n_exp, batch, slots * p, LANE), F32),
        compiler_params=_cp(("arbitrary", "arbitrary")),
        name="moe_ffn",
    )(xe, gates, wg, wu, wd)


def _combine_kernel(idx_ref, ye_ref, acc_ref, *, slots, p):
    @pl.when(pl.program_id(1) == 0)
    def _():
        acc_ref[...] = jnp.zeros_like(acc_ref)

    unroll = 4

    def body(c, _):
        rows, vals = [], []
        for u in range(unroll):
            r = c * unroll + u
            row = pl.multiple_of(idx_ref[0, 0, 0, r] * p, p)
            rows.append(row)
            vals.append(acc_ref[0, pl.ds(row, p), :] + ye_ref[0, 0, pl.ds(pl.multiple_of(r * p, p), p), :])
        for row, val in zip(rows, vals):
            acc_ref[0, pl.ds(row, p), :] = val
        return 0

    lax.fori_loop(0, slots // unroll, body, 0)


def _combine(idx, ye, tt):
    n_exp, batch, sp, _ = ye.shape
    slots = idx.shape[3]
    p = sp // slots
    kern = functools.partial(_combine_kernel, slots=slots, p=p)
    return pl.pallas_call(
        kern,
        grid=(batch, n_exp),
        in_specs=[pl.BlockSpec((1, 1, 1, slots), lambda b, e: (b, e, 0, 0), memory_space=pltpu.SMEM),
                  pl.BlockSpec((1, 1, sp, LANE), lambda b, e: (e, b, 0, 0))],
        out_specs=pl.BlockSpec((1, tt * p, LANE), lambda b, e: (b, 0, 0)),
        out_shape=jax.ShapeDtypeStruct((batch, tt * p, LANE), F32),
        compiler_params=_cp(("arbitrary", "arbitrary")),
        name="moe_combine",
    )(idx, ye)


def _final_kernel(moe_ref, x1_ref, mod_ref, g_ref, o_ref):
    p = x1_ref.shape[2] // LANE
    moe = jnp.concatenate([moe_ref[0, pl.ds(jj, TM, stride=p), :] for jj in range(p)], axis=1)
    o_ref[0] = x1_ref[0] + mod_ref[0][5:6, :] * _rms(moe, g_ref[3:4, :])


def _final(moe_s, x1, mod, g, n_ctx_tiles, skip_tiles):
    batch, tt, d = x1.shape
    p = d // LANE
    n_tiles = tt // TM - skip_tiles
    return pl.pallas_call(
        _final_kernel,
        grid=(batch, n_tiles),
        in_specs=[pl.BlockSpec((1, TM * p, LANE), lambda b, i: (b, i + skip_tiles, 0)),
                  pl.BlockSpec((1, TM, d), lambda b, i: (b, i + skip_tiles, 0)),
                  pl.BlockSpec((1, 6, d), lambda b, i: (jnp.where(i + skip_tiles < n_ctx_tiles, batch, b), 0, 0)),
                  pl.BlockSpec((4, d), lambda b, i: (0, 0))],
        out_specs=pl.BlockSpec((1, TM, d), lambda b, i: (b, i, 0)),
        out_shape=jax.ShapeDtypeStruct((batch, n_tiles * TM, d), F32),
        compiler_params=_cp(("arbitrary", "arbitrary")),
        name="post_ffn",
    )(moe_s, x1, mod, g)


def _rope_tables(ctx, seq):
    half = HEAD_DIM // 2
    inv = ROPE_BASE ** (-np.arange(0, half, 2, dtype=np.float32) / half)
    t = np.arange(seq)
    ang_r = (t // GRID_W).astype(np.float32)[:, None] * inv[None, :]
    ang_c = (t % GRID_W).astype(np.float32)[:, None] * inv[None, :]
    ang = jnp.asarray(np.concatenate([ang_r, ang_r, ang_c, ang_c], axis=1))
    sign = np.tile(np.concatenate([-np.ones(16), np.ones(16)]), 2).astype(np.float32)
    cos = jnp.concatenate([jnp.ones((ctx, HEAD_DIM), F32), jnp.cos(ang)], axis=0)
    sin = jnp.concatenate([jnp.zeros((ctx, HEAD_DIM), F32), jnp.sin(ang) * sign[None, :]], axis=0)
    return jnp.tile(cos, (1, LANE // HEAD_DIM)), jnp.tile(sin, (1, LANE // HEAD_DIM))


def _moe(h2s, logits, x1, mod, g, wr_unused, w_gate, w_up, w_down, ctx, n_ctx_tiles, skip_tiles):
    batch, tt, d = x1.shape
    n_exp = w_gate.shape[0]
    seq = tt - ctx
    idx_ctx = _route(logits[:, :ctx], n_exp, EC_CAPACITY * ctx // n_exp)
    idx_lat = _route(logits[:, ctx:], n_exp, EC_CAPACITY * seq // n_exp)
    idx = jnp.concatenate([idx_ctx, idx_lat + ctx], axis=2)[:, :, None, :]
    xe, gates = _gather(idx, h2s, logits, n_exp)
    ye = _ffn(xe, gates, w_gate.astype(BF16), w_up.astype(BF16), w_down.astype(BF16))
    moe_s = _combine(idx, ye, tt)
    return _final(moe_s, x1, mod, g, n_ctx_tiles, skip_tiles)


def kernel(x, c, ctx, c_ctx, ada_w, ada_b, norm_g, na_w_qkv, na_w_out, na_rel_bias, ml_w_in, ml_b_gate,
           ml_norm_w, ml_w_out, moe_w_router, moe_w_gate, moe_w_up, moe_w_down):
    batch, seq, d = x.shape
    n_ctx = ctx.shape[1]
    depth = ada_w.shape[0]
    rows = seq // GRID_W
    n_exp = moe_w_router.shape[-1]
    ml_heads = ml_norm_w.shape[-1] // ML_V_DIM
    assert n_ctx % TM == 0 and seq % TM == 0 and n_ctx % ML_CHUNK == 0 and seq % ML_CHUNK == 0
    assert rows >= K_ROWS and d % (2 * HEAD_DIM) == 0 and n_exp <= LANE and batch < 16
    n_ct = n_ctx // TM

    xa = jnp.concatenate([ctx, x], axis=1)
    c_all = jnp.zeros((16, d), F32).at[:batch].set(c).at[batch].set(c_ctx)
    mod = _ada(c_all, ada_w, ada_b).reshape(depth, 16, 6, d)
    cos_t, sin_t = _rope_tables(n_ctx, seq)

    for l in range(depth):
        last = l == depth - 1
        jx = l // 2
        g = norm_g[l]
        wr = jnp.zeros((d, LANE), F32).at[:, :n_exp].set(moe_w_router[l])
        wr_hi = wr.astype(BF16)
        wr2 = jnp.stack([wr_hi, (wr - wr_hi.astype(F32)).astype(BF16)])
        if l % 2 == 0:
            scale = HEAD_DIM ** -0.5
            w = jnp.concatenate([na_w_qkv[jx][:, :d] * scale, na_w_qkv[jx][:, d:]], axis=1).astype(BF16)
            qkv = _proj_na(xa, mod[l], g, w, n_ct)
            o = _na(qkv, _na_bias_table(na_rel_bias[jx], rows), n_ctx, rows)
            x1, h2s, logits = _post((o,), na_w_out[jx].astype(BF16), xa, mod[l], g, wr2, n_ct)
        else:
            qk_w = ml_heads * HEAD_DIM
            v_w = ml_heads * ML_V_DIM
            main_w = 2 * qk_w + 2 * v_w
            w_in = ml_w_in[jx]
            ng = 2 * ml_heads
            pad = jnp.zeros((d, LANE - ng), F32)
            w = jnp.concatenate([w_in[:, :qk_w], w_in[:, qk_w:2 * qk_w] * (HEAD_DIM ** -0.5), w_in[:, 2 * qk_w:main_w],
                                 w_in[:, main_w:main_w + ng], pad, w_in[:, main_w + ng:], pad], axis=1).astype(BF16)
            zpad = jnp.zeros((LANE - ng,), F32)
            bg = jnp.concatenate([ml_b_gate[jx][:ng], zpad, ml_b_gate[jx][ng:], zpad])[None, :]
            proj, gates = _proj_ml(xa, mod[l], g, w, bg, cos_t, sin_t, n_ct, qk_w, main_w)
            hf = _scan(proj, gates, ml_heads, n_ctx, reverse=False)
            hb = _scan(proj, gates, ml_heads, n_ctx, reverse=True)
            x1, h2s, logits = _post((hf, hb, proj, ml_norm_w[jx][None, :]), ml_w_out[jx].astype(BF16), xa, mod[l], g,
                                    wr2, n_ct, ml_heads=ml_heads)
        xa = _moe(h2s, logits, x1, mod[l], g, None, moe_w_gate[l], moe_w_up[l], moe_w_down[l], n_ctx, n_ct,
                  n_ct if last else 0)
    return xa
```

```python
import functools

import numpy as np
import jax
import jax.numpy as jnp
from jax import lax
from jax.experimental import pallas as pl
from jax.experimental.pallas import tpu as pltpu

F32 = jnp.float32
BF16 = jnp.bfloat16

LANE = 128
SUBLANE = 8
VMEM_LIMIT = 56 * 1024 * 1024

EPS = 1e-6
NEG = -1e30
GRID_W = 64
HEAD_DIM = 64
ML_V_DIM = 128
WIN_ROWS = 8
WIN_COLS = 16
ROPE_BASE = 10000.0
EC_CAPACITY = 2

TM = 256
Q_ROWS = TM // GRID_W
K_ROWS = Q_ROWS + WIN_ROWS
ML_CHUNK = 256


def _cp(sem):
    return pltpu.CompilerParams(dimension_semantics=sem, vmem_limit_bytes=VMEM_LIMIT)


def _dot(a, b):
    return jnp.dot(a, b, preferred_element_type=F32)


def _dot_nt(a, b):
    return lax.dot_general(a, b, (((1,), (1,)), ((), ())), preferred_element_type=F32)


def _split3(x):
    x1 = x.astype(BF16)
    r1 = x - x1.astype(F32)
    x2 = r1.astype(BF16)
    x3 = (r1 - x2.astype(F32)).astype(BF16)
    return x1, x2, x3


def _rms(x, g):
    return x * lax.rsqrt(jnp.mean(x * x, axis=-1, keepdims=True) + EPS) * g


def _silu(x):
    return x * jax.nn.sigmoid(x)


def _ada_kernel(c_ref, w_ref, b_ref, o_ref):
    s = _silu(c_ref[...]).astype(BF16)
    o_ref[0] = _dot(s, w_ref[0].astype(BF16)) + b_ref[0]


def _ada(c_all, ada_w, ada_b):
    depth, d, n = ada_w.shape
    rows = c_all.shape[0]
    tn = 1024 if n % 1024 == 0 else n
    return pl.pallas_call(
        _ada_kernel,
        grid=(depth, n // tn),
        in_specs=[pl.BlockSpec((rows, d), lambda l, j: (0, 0)),
                  pl.BlockSpec((1, d, tn), lambda l, j: (l, 0, j)),
                  pl.BlockSpec((1, 1, tn), lambda l, j: (l, 0, j))],
        out_specs=pl.BlockSpec((1, rows, tn), lambda l, j: (l, 0, j)),
        out_shape=jax.ShapeDtypeStruct((depth, rows, n), F32),
        compiler_params=_cp(("arbitrary", "arbitrary")),
        name="ada_mod",
    )(c_all, ada_w, ada_b.reshape(depth, 1, n))


def _mod_spec(batch, n_ctx_tiles, d):
    return pl.BlockSpec((1, 6, d), lambda b, i: (jnp.where(i < n_ctx_tiles, batch, b), 0, 0))


def _proj_na_kernel(x_ref, mod_ref, g_ref, w_ref, o_ref):
    m = mod_ref[0]
    h = (_rms(x_ref[0], g_ref[0:1, :]) * (1.0 + m[1:2, :]) + m[0:1, :]).astype(BF16)
    o_ref[0] = _dot(h, w_ref[...]).astype(BF16)


def _proj_na(xa, mod, g, w_bf16, n_ctx_tiles):
    batch, tt, d = xa.shape
    n = w_bf16.shape[1]
    return pl.pallas_call(
        _proj_na_kernel,
        grid=(batch, tt // TM),
        in_specs=[pl.BlockSpec((1, TM, d), lambda b, i: (b, i, 0)),
                  _mod_spec(batch, n_ctx_tiles, d),
                  pl.BlockSpec((4, d), lambda b, i: (0, 0)),
                  pl.BlockSpec((d, n), lambda b, i: (0, 0))],
        out_specs=pl.BlockSpec((1, TM, n), lambda b, i: (b, i, 0)),
        out_shape=jax.ShapeDtypeStruct((batch, tt, n), BF16),
        compiler_params=_cp(("arbitrary", "arbitrary")),
        name="proj_na",
    )(xa, mod, g, w_bf16)


def _proj_ml_kernel(x_ref, mod_ref, g_ref, w_ref, bg_ref, cos_ref, sin_ref, o_ref, gate_ref, *, qk_w, main_w):
    m = mod_ref[0]
    h = (_rms(x_ref[0], g_ref[0:1, :]) * (1.0 + m[1:2, :]) + m[0:1, :]).astype(BF16)
    r = _dot(h, w_ref[...])
    qk = r[:, :2 * qk_w]
    reps = 2 * qk_w // LANE
    cos = jnp.concatenate([cos_ref[...]] * reps, axis=1)
    sin = jnp.concatenate([sin_ref[...]] * reps, axis=1)
    lane = lax.broadcasted_iota(jnp.int32, qk.shape, 1)
    partner = jnp.where(lane % 32 < 16, pltpu.roll(qk, 2 * qk_w - 16, 1), pltpu.roll(qk, 16, 1))
    qk = qk * cos + partner * sin
    o_ref[0, :, :2 * qk_w] = qk.astype(BF16)
    o_ref[0, :, 2 * qk_w:] = r[:, 2 * qk_w:main_w].astype(BF16)
    gr = r[:, main_w:] + bg_ref[...]
    gate_ref[0, :, :LANE] = gr[:, :LANE]
    f = gr[:, LANE:]
    gate_ref[0, :, LANE:] = jnp.minimum(f, 0.0) - jnp.log1p(jnp.exp(-jnp.abs(f)))


def _proj_ml(xa, mod, g, w_bf16, bg, cos_t, sin_t, n_ctx_tiles, qk_w, main_w):
    batch, tt, d = xa.shape
    n = w_bf16.shape[1]
    kern = functools.partial(_proj_ml_kernel, qk_w=qk_w, main_w=main_w)
    return pl.pallas_call(
        kern,
        grid=(batch, tt // TM),
        in_specs=[pl.BlockSpec((1, TM, d), lambda b, i: (b, i, 0)),
                  _mod_spec(batch, n_ctx_tiles, d),
                  pl.BlockSpec((4, d), lambda b, i: (0, 0)),
                  pl.BlockSpec((d, n), lambda b, i: (0, 0)),
                  pl.BlockSpec((1, 2 * LANE), lambda b, i: (0, 0)),
                  pl.BlockSpec((TM, LANE), lambda b, i: (i, 0)),
                  pl.BlockSpec((TM, LANE), lambda b, i: (i, 0))],
        out_specs=[pl.BlockSpec((1, TM, main_w), lambda b, i: (b, i, 0)),
                   pl.BlockSpec((1, TM, 2 * LANE), lambda b, i: (b, i, 0))],
        out_shape=[jax.ShapeDtypeStruct((batch, tt, main_w), BF16),
                   jax.ShapeDtypeStruct((batch, tt, 2 * LANE), F32)],
        compiler_params=_cp(("arbitrary", "arbitrary")),
        name="proj_ml",
    )(xa, mod, g, w_bf16, bg, cos_t, sin_t)


def _na_kernel(q_ref, k_ref, v_ref, bias_ref, o_ref, *, ctx, rows):
    step = pl.program_id(2)
    n_keys = K_ROWS * GRID_W

    def attend(s_list, v_list):
        m = s_list[0].max(axis=1, keepdims=True)
        for s in s_list[1:]:
            m = jnp.maximum(m, s.max(axis=1, keepdims=True))
        num, den = 0.0, 0.0
        for s, v in zip(s_list, v_list):
            p = jnp.exp(s - m)
            den = den + p.sum(axis=1, keepdims=True)
            num = num + _dot(p.astype(BF16), v)
        return num / den

    @pl.when(step == 0)
    def _():
        outs = []
        for hh in range(2):
            sl = slice(hh * HEAD_DIM, (hh + 1) * HEAD_DIM)
            q = q_ref[0, :, sl]
            s_c = _dot_nt(q, k_ref[0, 0:ctx, sl])
            outs.append(attend([s_c], [v_ref[0, 0:ctx, sl]]))
        o_ref[0] = jnp.concatenate(outs, axis=1).astype(BF16)

    @pl.when(step > 0)
    def _():
        rb = step - 1
        start = jnp.clip(Q_ROWS * rb - WIN_ROWS // 2, 0, rows - K_ROWS)
        ks = pl.multiple_of(ctx + start * GRID_W, TM)
        outs = []
        for hh in range(2):
            sl = slice(hh * HEAD_DIM, (hh + 1) * HEAD_DIM)
            q = q_ref[0, :, sl]
            s_w = _dot_nt(q, k_ref[0, pl.ds(ks, n_keys), sl]) + bias_ref[hh, 0]
            s_c = _dot_nt(q, k_ref[0, 0:ctx, sl])
            outs.append(attend([s_w, s_c], [v_ref[0, pl.ds(ks, n_keys), sl], v_ref[0, 0:ctx, sl]]))
        o_ref[0] = jnp.concatenate(outs, axis=1).astype(BF16)


def _na_bias_table(rel_bias, rows):
    n_rb = rows // Q_ROWS
    n_dr, n_dc = 2 * WIN_ROWS - 1, 2 * WIN_COLS - 1
    qc = np.arange(GRID_W)
    c0 = np.clip(qc - WIN_COLS // 2, 0, GRID_W - WIN_COLS)
    col_ok = (qc[None, :] >= c0[:, None]) & (qc[None, :] < c0[:, None] + WIN_COLS)
    dc = np.clip(qc[None, :] - qc[:, None] + WIN_COLS - 1, 0, n_dc - 1)
    oh_c = ((np.arange(n_dc)[:, None, None] == dc[None]) & col_ok[None]).astype(np.float32)
    oh_r = np.zeros((3, n_dr, Q_ROWS, K_ROWS), np.float32)
    row_ok = np.zeros((3, Q_ROWS, K_ROWS), bool)
    for ci, rb in enumerate((0, 1, n_rb - 1)):
        start = int(np.clip(Q_ROWS * rb - WIN_ROWS // 2, 0, rows - K_ROWS))
        qr = Q_ROWS * rb + np.arange(Q_ROWS)
        kr = start + np.arange(K_ROWS)
        r0 = np.clip(qr - WIN_ROWS // 2, 0, rows - WIN_ROWS)
        row_ok[ci] = (kr[None, :] >= r0[:, None]) & (kr[None, :] < r0[:, None] + WIN_ROWS)
        dr = np.clip(kr[None, :] - qr[:, None] + WIN_ROWS - 1, 0, n_dr - 1)
        oh_r[ci] = (np.arange(n_dr)[:, None, None] == dr[None]) & row_ok[ci][None]
    hi = lax.Precision.HIGHEST
    t1 = jnp.einsum('hrc,cab->hrab', rel_bias.astype(F32), oh_c, precision=hi)
    t2 = jnp.einsum('hrab,srqk->hsqakb', t1, oh_r, precision=hi)
    ok = row_ok[:, :, None, :, None] & col_ok[None, None, :, None, :]
    return jnp.where(ok[None], t2, NEG).reshape(rel_bias.shape[0], 3, TM, K_ROWS * GRID_W)


def _na(qkv, bias_tab, ctx, rows):
    batch, tt, d3 = qkv.shape
    d = d3 // 3
    hp = d // LANE
    n_rb = rows // Q_ROWS
    n_ct = ctx // TM
    nk = K_ROWS * GRID_W

    def case(s):
        rb = s - 1
        return jnp.where(rb <= 0, 0, jnp.where(rb == n_rb - 1, 2, 1))

    kern = functools.partial(_na_kernel, ctx=ctx, rows=rows)
    return pl.pallas_call(
        kern,
        grid=(hp, batch, n_rb + 1),
        in_specs=[pl.BlockSpec((1, TM, LANE), lambda h, b, s: (b, jnp.where(s == 0, 0, s - 1 + n_ct), h)),
                  pl.BlockSpec((1, tt, LANE), lambda h, b, s: (b, 0, hp + h)),
                  pl.BlockSpec((1, tt, LANE), lambda h, b, s: (b, 0, 2 * hp + h)),
                  pl.BlockSpec((2, 1, TM, nk), lambda h, b, s: (h, case(s), 0, 0))],
        out_specs=pl.BlockSpec((1, TM, LANE), lambda h, b, s: (b, jnp.where(s == 0, 0, s - 1 + n_ct), h)),
        out_shape=jax.ShapeDtypeStruct((batch, tt, d), BF16),
        compiler_params=_cp(("arbitrary", "arbitrary", "arbitrary")),
        name="na_attention",
    )(qkv, qkv, qkv, bias_tab)


def _scan_kernel(q_ref, k_ref, v_ref, gate_ref, o_ref, c_ref, m_ref, *, heads, reverse):
    L = ML_CHUNK
    j = pl.program_id(1)

    @pl.when(j == 0)
    def _():
        c_ref[...] = jnp.zeros_like(c_ref)
        m_ref[...] = jnp.full_like(m_ref, NEG)

    row = lax.broadcasted_iota(jnp.int32, (L, L), 0)
    col = lax.broadcasted_iota(jnp.int32, (L, L), 1)
    mask = (row <= col) if reverse else (row >= col)
    tri = jnp.where(mask, 1.0, 0.0).astype(BF16)

    gi = gate_ref[0, :, :LANE]
    gf = gate_ref[0, :, LANE:]
    f1, f2, f3 = _split3(gf)
    b_col = _dot(tri, f1) + _dot(tri, f2) + _dot(tri, f3)
    b_row = b_col.T
    i_row = gi.T
    last = 0 if reverse else L - 1
    b_end = b_col[last:last + 1, :]
    k_t = k_ref[0].astype(F32).T
    m_all = m_ref[...]
    lane1 = lax.broadcasted_iota(jnp.int32, (1, LANE), 1)
    ones_col = jnp.where(lax.broadcasted_iota(jnp.int32, (L, LANE), 1) == 0, 1.0, 0.0).astype(BF16)

    for h in range(heads):
        c = (heads if reverse else 0) + h
        bc = b_col[:, c:c + 1]
        br = b_row[c:c + 1, :]
        ir = i_row[c:c + 1, :]
        mh = m_all[:, c:c + 1]
        be = b_end[:, c:c + 1]
        dmat = jnp.where(mask, bc - br + ir, NEG)
        inter = bc + mh
        m_t = jnp.maximum(dmat.max(axis=1, keepdims=True), inter)
        p = jnp.exp(dmat - m_t)
        w_inter = jnp.exp(inter - m_t)
        q = q_ref[0, :, h * HEAD_DIM:(h + 1) * HEAD_DIM]
        k = k_ref[0, :, h * HEAD_DIM:(h + 1) * HEAD_DIM]
        v_aug = jnp.concatenate([v_ref[0, :, h * ML_V_DIM:(h + 1) * ML_V_DIM], ones_col], axis=1)
        s = (_dot_nt(q, k) * p).astype(BF16)
        state = c_ref[h]
        r = _dot(s, v_aug) + w_inter * _dot(q, state.astype(BF16))
        den = r[:, ML_V_DIM:ML_V_DIM + 1]
        o_ref[0, :, h * ML_V_DIM:(h + 1) * ML_V_DIM] = r[:, :ML_V_DIM] / jnp.maximum(jnp.abs(den), jnp.exp(-m_t))
        g_row = be - br + ir
        m_new = jnp.maximum(be + mh, g_row.max(axis=1, keepdims=True))
        w_k = jnp.exp(g_row - m_new)
        decay = jnp.exp(be + mh - m_new)
        kw = (k_t[h * HEAD_DIM:(h + 1) * HEAD_DIM, :] * w_k).astype(BF16)
        c_ref[h] = decay * state + _dot(kw, v_aug)
        m_all = jnp.where(lane1 == c, m_new, m_all)
    m_ref[...] = m_all


def _scan(proj, gates, heads, ctx, reverse):
    batch, tt, _ = proj.shape
    L = ML_CHUNK
    nc = tt // L
    ncc = ctx // L
    qk_w = heads * HEAD_DIM
    v_w = heads * ML_V_DIM

    def chunk(j):
        if not reverse:
            return j
        return jnp.where(j < ncc, ncc - 1 - j, nc - 1 - (j - ncc))

    kern = functools.partial(_scan_kernel, heads=heads, reverse=reverse)
    return pl.pallas_call(
        kern,
        grid=(batch, nc),
        in_specs=[pl.BlockSpec((1, L, qk_w), lambda b, j: (b, chunk(j), 0)),
                  pl.BlockSpec((1, L, qk_w), lambda b, j: (b, chunk(j), 1)),
                  pl.BlockSpec((1, L, v_w), lambda b, j: (b, chunk(j), 2 * qk_w // v_w)),
                  pl.BlockSpec((1, L, 2 * LANE), lambda b, j: (b, chunk(j), 0))],
        out_specs=pl.BlockSpec((1, L, v_w), lambda b, j: (b, chunk(j), 0)),
        out_shape=jax.ShapeDtypeStruct((batch, tt, v_w), F32),
        scratch_shapes=[pltpu.VMEM((heads, HEAD_DIM, 2 * ML_V_DIM), F32),
                        pltpu.VMEM((1, LANE), F32)],
        compiler_params=_cp(("arbitrary", "arbitrary")),
        name="mlstm_scan_bwd" if reverse else "mlstm_scan_fwd",
    )(proj, proj, proj, gates)


def _post_common(y_in, w_ref, xa_ref, mod_ref, g_ref, wr_ref, x1_ref, h2s_ref, lg_ref):
    m = mod_ref[0]
    y = _dot(y_in, w_ref[...])
    x1 = xa_ref[0] + m[2:3, :] * _rms(y, g_ref[1:2, :])
    x1_ref[0] = x1
    h2 = _rms(x1, g_ref[2:3, :]) * (1.0 + m[4:5, :]) + m[3:4, :]
    d = h2.shape[1]
    p = d // LANE
    for jj in range(p):
        h2s_ref[0, pl.ds(jj, TM, stride=p), :] = h2[:, jj * LANE:(jj + 1) * LANE]
    h_hi = h2.astype(BF16)
    h_lo = (h2 - h_hi.astype(F32)).astype(BF16)
    lg_ref[0] = _dot(h_hi, wr_ref[0]) + _dot(h_hi, wr_ref[1]) + _dot(h_lo, wr_ref[0])


def _post_na_kernel(o_ref, w_ref, xa_ref, mod_ref, g_ref, wr_ref, x1_ref, h2s_ref, lg_ref):
    _post_common(o_ref[0], w_ref, xa_ref, mod_ref, g_ref, wr_ref, x1_ref, h2s_ref, lg_ref)


def _post_ml_kernel(hf_ref, hb_ref, og_ref, nw_ref, w_ref, xa_ref, mod_ref, g_ref, wr_ref,
                    x1_ref, h2s_ref, lg_ref, *, heads):
    hs = hf_ref[0] + hb_ref[0]
    parts = []
    for h in range(heads):
        t = hs[:, h * ML_V_DIM:(h + 1) * ML_V_DIM]
        parts.append(t * lax.rsqrt(jnp.mean(t * t, axis=-1, keepdims=True) + EPS))
    hn = jnp.concatenate(parts, axis=1) * nw_ref[...]
    y_in = (hn * jax.nn.sigmoid(og_ref[0].astype(F32))).astype(BF16)
    _post_common(y_in, w_ref, xa_ref, mod_ref, g_ref, wr_ref, x1_ref, h2s_ref, lg_ref)


def _post(mixer_inputs, w_out, xa, mod, g, wr, n_ctx_tiles, ml_heads=None):
    batch, tt, d = xa.shape
    p = d // LANE
    tile = lambda w: pl.BlockSpec((1, TM, w), lambda b, i: (b, i, 0))
    full2 = lambda a: pl.BlockSpec(a.shape, lambda b, i: (0,) * a.ndim)
    if ml_heads is None:
        (o,) = mixer_inputs
        kern = _post_na_kernel
        head_specs, head_args = [tile(d)], [o]
    else:
        hf, hb, proj, nw = mixer_inputs
        v_w = ml_heads * ML_V_DIM
        og_block = (2 * ml_heads * HEAD_DIM + v_w) // v_w
        kern = functools.partial(_post_ml_kernel, heads=ml_heads)
        head_specs = [tile(v_w), tile(v_w), pl.BlockSpec((1, TM, v_w), lambda b, i: (b, i, og_block)), full2(nw)]
        head_args = [hf, hb, proj, nw]
    return pl.pallas_call(
        kern,
        grid=(batch, tt // TM),
        in_specs=head_specs + [full2(w_out), tile(d), _mod_spec(batch, n_ctx_tiles, d), full2(g), full2(wr)],
        out_specs=[tile(d),
                   pl.BlockSpec((1, TM * p, LANE), lambda b, i: (b, i, 0)),
                   tile(LANE)],
        out_shape=[jax.ShapeDtypeStruct((batch, tt, d), F32),
                   jax.ShapeDtypeStruct((batch, tt * p, LANE), F32),
                   jax.ShapeDtypeStruct((batch, tt, LANE), F32)],
        compiler_params=_cp(("arbitrary", "arbitrary")),
        name="post_mixer",
    )(*head_args, w_out, xa, mod, g, wr)


def _route_kernel(lg_ref, idx_ref, pos_ref, col_ref, racc_ref, *, n_exp, cap):
    t = lg_ref.shape[1]
    n_tiles = t // LANE
    lt = lg_ref[0].T[:n_exp, :]
    e = jnp.exp(lt - lt.max(axis=0, keepdims=True))
    aff = e / e.sum(axis=0, keepdims=True)
    def count_ge(v):
        return jnp.where(aff >= v, 1.0, 0.0).sum(axis=1, keepdims=True)

    def bisect(i, cur):
        cand = cur | jnp.left_shift(jnp.int32(1), 30 - i)
        return jnp.where(count_ge(pltpu.bitcast(cand, F32)) >= cap, cand, cur)

    v_bits = lax.fori_loop(0, 31, bisect, jnp.zeros((n_exp, 1), jnp.int32))

    def refine(_, hi):
        pivot = jnp.where(aff < hi, aff, -1.0).max(axis=1, keepdims=True)
        return jnp.where(count_ge(pivot) >= cap, hi, pivot)

    min_normal_bits = 0x00800000
    hi = lax.fori_loop(0, 3, refine, pltpu.bitcast(jnp.maximum(v_bits + 1, min_normal_bits), F32))
    thr = jnp.where(aff < hi, aff, -1.0).max(axis=1, keepdims=True)
    gt = aff > thr
    eq = aff == thr
    need = cap - jnp.where(gt, 1.0, 0.0).sum(axis=1, keepdims=True)

    upper = jnp.where(lax.broadcasted_iota(jnp.int32, (LANE, LANE), 0)
                      <= lax.broadcasted_iota(jnp.int32, (LANE, LANE), 1), 1.0, 0.0).astype(BF16)

    def excl_cumsum(mask_f32):
        carry = jnp.zeros((n_exp, 1), F32)
        parts = []
        for jt in range(n_tiles):
            tile = mask_f32[:, jt * LANE:(jt + 1) * LANE]
            inc = _dot(tile.astype(BF16), upper)
            parts.append(inc - tile + carry)
            carry = carry + inc[:, LANE - 1:LANE]
        return jnp.concatenate(parts, axis=1)

    eq_f = jnp.where(eq, 1.0, 0.0)
    sel = gt | (eq & (excl_cumsum(eq_f) < need))
    sel_f = jnp.where(sel, 1.0, 0.0)
    pos_ref[...] = jnp.where(sel, excl_cumsum(sel_f), -1.0)

    t_iota = lax.broadcasted_iota(jnp.int32, (SUBLANE, LANE), 1).astype(F32)
    r_iota = lax.broadcasted_iota(jnp.int32, (SUBLANE, LANE), 0).astype(F32)
    col_ref[...] = jnp.zeros_like(col_ref)
    racc_ref[...] = jnp.zeros_like(racc_ref)
    unroll = 2
    for ex in range(n_exp):
        def ranks(it, _):
            for u in range(unroll):
                rc = it * unroll + u
                r0 = (rc * SUBLANE).astype(F32)
                acc = jnp.zeros((SUBLANE, LANE), F32)
                for jt in range(n_tiles):
                    prow = pos_ref[ex:ex + 1, jt * LANE:(jt + 1) * LANE]
                    acc = acc + jnp.where(prow == r_iota + r0, t_iota + float(jt * LANE), 0.0)
                racc_ref[pl.ds(pl.multiple_of(rc * SUBLANE, SUBLANE), SUBLANE), :] = acc
            return 0
        lax.fori_loop(0, cap // (SUBLANE * unroll), ranks, 0)
        col_ref[:, ex:ex + 1] = racc_ref[...].sum(axis=1, keepdims=True)
    idx_ref[0] = col_ref[...].T[:n_exp, :cap].astype(jnp.int32)


def _route(logits, n_exp, cap):
    batch, t, _ = logits.shape
    kern = functools.partial(_route_kernel, n_exp=n_exp, cap=cap)
    return pl.pallas_call(
        kern,
        grid=(batch,),
        in_specs=[pl.BlockSpec((1, t, LANE), lambda b: (b, 0, 0))],
        out_specs=pl.BlockSpec((1, n_exp, cap), lambda b: (b, 0, 0)),
        out_shape=jax.ShapeDtypeStruct((batch, n_exp, cap), jnp.int32),
        scratch_shapes=[pltpu.VMEM((n_exp, t), F32)] + [pltpu.VMEM((-(-cap // LANE) * LANE, LANE), F32)] * 2,
        compiler_params=_cp(("arbitrary",)),
        name="route",
    )(logits)


def _gather_kernel(idx_ref, src_ref, lg_ref, xe_ref, gate_ref, tile_ref, gl_ref, *, n_exp, slots, p, stride):
    unroll = 8

    def body(c, _):
        for u in range(unroll):
            r = c * unroll + u
            t = idx_ref[0, 0, 0, r]
            tile_ref[pl.ds(r, p, stride=stride), :] = src_ref[0, pl.ds(pl.multiple_of(t * p, p), p), :]
            gl_ref[pl.ds(r, 1), :] = lg_ref[0, pl.ds(t, 1), :]
        return 0

    lax.fori_loop(0, slots // unroll, body, 0)
    xe_ref[0, 0] = jnp.concatenate([tile_ref[pl.ds(jj * stride, slots), :] for jj in range(p)], axis=1).astype(BF16)
    lg = gl_ref[...]
    lane = lax.broadcasted_iota(jnp.int32, lg.shape, 1)
    lg = jnp.where(lane < n_exp, lg, NEG)
    ex = jnp.exp(lg - lg.max(axis=1, keepdims=True))
    gate_ref[0, 0] = ex / ex.sum(axis=1, keepdims=True)


def _gather(idx, h2s, logits, n_exp):
    batch, _, _, slots = idx.shape
    tt = logits.shape[1]
    p = h2s.shape[1] // tt
    d = p * LANE
    stride = slots + SUBLANE
    kern = functools.partial(_gather_kernel, n_exp=n_exp, slots=slots, p=p, stride=stride)
    return pl.pallas_call(
        kern,
        grid=(batch, n_exp),
        in_specs=[pl.BlockSpec((1, 1, 1, slots), lambda b, e: (b, e, 0, 0), memory_space=pltpu.SMEM),
                  pl.BlockSpec((1, tt * p, LANE), lambda b, e: (b, 0, 0), pipeline_mode=pl.Buffered(1)),
                  pl.BlockSpec((1, tt, LANE), lambda b, e: (b, 0, 0))],
        out_specs=[pl.BlockSpec((1, 1, slots, d), lambda b, e: (e, b, 0, 0)),
                   pl.BlockSpec((1, 1, slots, LANE), lambda b, e: (e, b, 0, 0))],
        out_shape=[jax.ShapeDtypeStruct((n_exp, batch, slots, d), BF16),
                   jax.ShapeDtypeStruct((n_exp, batch, slots, LANE), F32)],
        scratch_shapes=[pltpu.VMEM((p * stride, LANE), F32), pltpu.VMEM((slots, LANE), F32)],
        compiler_params=_cp(("arbitrary", "arbitrary")),
        name="moe_gather",
    )(idx, h2s, logits)


def _ffn_kernel(xe_ref, gate_ref, wg_ref, wu_ref, wd_ref, ye_ref, wgb_ref, wub_ref, wdb_ref, acc_ref, *, n_fc):
    e, fc, b = pl.program_id(0), pl.program_id(1), pl.program_id(2)

    @pl.when(b == 0)
    def _():
        wgb_ref[...] = wg_ref[0].astype(BF16)
        wub_ref[...] = wu_ref[0].astype(BF16)
        wdb_ref[...] = wd_ref[0].astype(BF16)

    x = xe_ref[0, 0]
    slots, d = x.shape
    a = _dot(x, wgb_ref[...])
    u = _dot(x, wub_ref[...])
    part = _dot((_silu(a) * u).astype(BF16), wdb_ref[...])

    @pl.when(fc == 0)
    def _():
        acc_ref[b] = part

    @pl.when((fc > 0) & (fc < n_fc - 1))
    def _():
        acc_ref[b] += part

    @pl.when(fc == n_fc - 1)
    def _():
        y = acc_ref[b] + part
        gt = gate_ref[0, 0]
        lane = lax.broadcasted_iota(jnp.int32, gt.shape, 1)
        y = y * jnp.where(lane == e, gt, 0.0).sum(axis=1, keepdims=True)
        p = d // LANE
        for jj in range(p):
            ye_ref[0, 0, pl.ds(jj, slots, stride=p), :] = y[:, jj * LANE:(jj + 1) * LANE]


def _ffn(xe, gates, wg, wu, wd):
    n_exp, batch, slots, d = xe.shape
    f = wg.shape[2]
    p = d // LANE
    fck = min(f // 2, 512)
    n_fc = f // fck
    kern = functools.partial(_ffn_kernel, n_fc=n_fc)
    out_b = lambda fc, b: jnp.where(fc == n_fc - 1, b, 0)
    return pl.pallas_call(
        kern,
        grid=(n_exp, n_fc, batch),
        in_specs=[pl.BlockSpec((1, 1, slots, d), lambda e, fc, b: (e, b, 0, 0)),
                  pl.BlockSpec((1, 1, slots, LANE), lambda e, fc, b: (e, b, 0, 0)),
                  pl.BlockSpec((1, d, fck), lambda e, fc, b: (e, 0, fc)),
                  pl.BlockSpec((1, d, fck), lambda e, fc, b: (e, 0, fc)),
                  pl.BlockSpec((1, fck, d), lambda e, fc, b: (e, fc, 0))],
        out_specs=pl.BlockSpec((1, 1, slots * p, LANE), lambda e, fc, b: (e, out_b(fc, b), 0, 0)),
        out_shape=jax.ShapeDtypeStruct((n_exp, batch, slots * p, LANE), F32),
        scratch_shapes=[pltpu.VMEM((d, fck), BF16), pltpu.VMEM((d, fck), BF16), pltpu.VMEM((fck, d), BF16),
                        pltpu.VMEM((batch, slots, d), F32)],
        compiler_params=_cp(("arbitrary", "arbitrary", "arbitrary")),
        name="moe_ffn",
    )(xe, gates, wg, wu, wd)


def _combine_kernel(idx_ref, ye_ref, acc_ref, *, slots, p):
    @pl.when(pl.program_id(1) == 0)
    def _():
        acc_ref[...] = jnp.zeros_like(acc_ref)

    unroll = 4

    def body(c, _):
        rows, vals = [], []
        for u in range(unroll):
            r = c * unroll + u
            row = pl.multiple_of(idx_ref[0, 0, 0, r] * p, p)
            rows.append(row)
            vals.append(acc_ref[0, pl.ds(row, p), :] + ye_ref[0, 0, pl.ds(pl.multiple_of(r * p, p), p), :])
        for row, val in zip(rows, vals):
            acc_ref[0, pl.ds(row, p), :] = val
        return 0

    lax.fori_loop(0, slots // unroll, body, 0)


def _combine(idx, ye, tt):
    n_exp, batch, sp, _ = ye.shape
    slots = idx.shape[3]
    p = sp // slots
    kern = functools.partial(_combine_kernel, slots=slots, p=p)
    return pl.pallas_call(
        kern,
        grid=(batch, n_exp),
        in_specs=[pl.BlockSpec((1, 1, 1, slots), lambda b, e: (b, e, 0, 0), memory_space=pltpu.SMEM),
                  pl.BlockSpec((1, 1, sp, LANE), lambda b, e: (e, b, 0, 0))],
        out_specs=pl.BlockSpec((1, tt * p, LANE), lambda b, e: (b, 0, 0)),
        out_shape=jax.ShapeDtypeStruct((batch, tt * p, LANE), F32),
        compiler_params=_cp(("arbitrary", "arbitrary")),
        name="moe_combine",
    )(idx, ye)


def _final_kernel(moe_ref, x1_ref, mod_ref, g_ref, o_ref):
    p = x1_ref.shape[2] // LANE
    moe = jnp.concatenate([moe_ref[0, pl.ds(jj, TM, stride=p), :] for jj in range(p)], axis=1)
    o_ref[0] = x1_ref[0] + mod_ref[0][5:6, :] * _rms(moe, g_ref[3:4, :])


def _final(moe_s, x1, mod, g, n_ctx_tiles, skip_tiles):
    batch, tt, d = x1.shape
    p = d // LANE
    n_tiles = tt // TM - skip_tiles
    return pl.pallas_call(
        _final_kernel,
        grid=(batch, n_tiles),
        in_specs=[pl.BlockSpec((1, TM * p, LANE), lambda b, i: (b, i + skip_tiles, 0)),
                  pl.BlockSpec((1, TM, d), lambda b, i: (b, i + skip_tiles, 0)),
                  pl.BlockSpec((1, 6, d), lambda b, i: (jnp.where(i + skip_tiles < n_ctx_tiles, batch, b), 0, 0)),
                  pl.BlockSpec((4, d), lambda b, i: (0, 0))],
        out_specs=pl.BlockSpec((1, TM, d), lambda b, i: (b, i, 0)),
        out_shape=jax.ShapeDtypeStruct((batch, n_tiles * TM, d), F32),
        compiler_params=_cp(("arbitrary", "arbitrary")),
        name="post_ffn",
    )(moe_s, x1, mod, g)


def _rope_tables(ctx, seq):
    half = HEAD_DIM // 2
    inv = ROPE_BASE ** (-np.arange(0, half, 2, dtype=np.float32) / half)
    t = np.arange(seq)
    ang_r = (t // GRID_W).astype(np.float32)[:, None] * inv[None, :]
    ang_c = (t % GRID_W).astype(np.float32)[:, None] * inv[None, :]
    ang = jnp.asarray(np.concatenate([ang_r, ang_r, ang_c, ang_c], axis=1))
    sign = np.tile(np.concatenate([-np.ones(16), np.ones(16)]), 2).astype(np.float32)
    cos = jnp.concatenate([jnp.ones((ctx, HEAD_DIM), F32), jnp.cos(ang)], axis=0)
    sin = jnp.concatenate([jnp.zeros((ctx, HEAD_DIM), F32), jnp.sin(ang) * sign[None, :]], axis=0)
    return jnp.tile(cos, (1, LANE // HEAD_DIM)), jnp.tile(sin, (1, LANE // HEAD_DIM))


def _moe(h2s, logits, x1, mod, g, wr_unused, w_gate, w_up, w_down, ctx, n_ctx_tiles, skip_tiles):
    batch, tt, d = x1.shape
    n_exp = w_gate.shape[0]
    seq = tt - ctx
    idx_ctx = _route(logits[:, :ctx], n_exp, EC_CAPACITY * ctx // n_exp)
    idx_lat = _route(logits[:, ctx:], n_exp, EC_CAPACITY * seq // n_exp)
    idx = jnp.concatenate([idx_ctx, idx_lat + ctx], axis=2)[:, :, None, :]
    xe, gates = _gather(idx, h2s, logits, n_exp)
    ye = _ffn(xe, gates, w_gate, w_up, w_down)
    moe_s = _combine(idx, ye, tt)
    return _final(moe_s, x1, mod, g, n_ctx_tiles, skip_tiles)


def kernel(x, c, ctx, c_ctx, ada_w, ada_b, norm_g, na_w_qkv, na_w_out, na_rel_bias, ml_w_in, ml_b_gate,
           ml_norm_w, ml_w_out, moe_w_router, moe_w_gate, moe_w_up, moe_w_down):
    batch, seq, d = x.shape
    n_ctx = ctx.shape[1]
    depth = ada_w.shape[0]
    rows = seq // GRID_W
    n_exp = moe_w_router.shape[-1]
    ml_heads = ml_norm_w.shape[-1] // ML_V_DIM
    assert n_ctx % TM == 0 and seq % TM == 0 and n_ctx % ML_CHUNK == 0 and seq % ML_CHUNK == 0
    assert rows >= K_ROWS and d % (2 * HEAD_DIM) == 0 and n_exp <= LANE and batch < 16
    n_ct = n_ctx // TM

    xa = jnp.concatenate([ctx, x], axis=1)
    c_all = jnp.zeros((16, d), F32).at[:batch].set(c).at[batch].set(c_ctx)
    mod = _ada(c_all, ada_w, ada_b).reshape(depth, 16, 6, d)
    cos_t, sin_t = _rope_tables(n_ctx, seq)

    for l in range(depth):
        last = l == depth - 1
        jx = l // 2
        g = norm_g[l]
        wr = jnp.zeros((d, LANE), F32).at[:, :n_exp].set(moe_w_router[l])
        wr_hi = wr.astype(BF16)
        wr2 = jnp.stack([wr_hi, (wr - wr_hi.astype(F32)).astype(BF16)])
        if l % 2 == 0:
            scale = HEAD_DIM ** -0.5
            w = jnp.concatenate([na_w_qkv[jx][:, :d] * scale, na_w_qkv[jx][:, d:]], axis=1).astype(BF16)
            qkv = _proj_na(xa, mod[l], g, w, n_ct)
            o = _na(qkv, _na_bias_table(na_rel_bias[jx], rows), n_ctx, rows)
            x1, h2s, logits = _post((o,), na_w_out[jx].astype(BF16), xa, mod[l], g, wr2, n_ct)
        else:
            qk_w = ml_heads * HEAD_DIM
            v_w = ml_heads * ML_V_DIM
            main_w = 2 * qk_w + 2 * v_w
            w_in = ml_w_in[jx]
            ng = 2 * ml_heads
            pad = jnp.zeros((d, LANE - ng), F32)
            w = jnp.concatenate([w_in[:, :qk_w], w_in[:, qk_w:2 * qk_w] * (HEAD_DIM ** -0.5), w_in[:, 2 * qk_w:main_w],
                                 w_in[:, main_w:main_w + ng], pad, w_in[:, main_w + ng:], pad], axis=1).astype(BF16)
            zpad = jnp.zeros((LANE - ng,), F32)
            bg = jnp.concatenate([ml_b_gate[jx][:ng], zpad, ml_b_gate[jx][ng:], zpad])[None, :]
            proj, gates = _proj_ml(xa, mod[l], g, w, bg, cos_t, sin_t, n_ct, qk_w, main_w)
            hf = _scan(proj, gates, ml_heads, n_ctx, reverse=False)
            hb = _scan(proj, gates, ml_heads, n_ctx, reverse=True)
            x1, h2s, logits = _post((hf, hb, proj, ml_norm_w[jx][None, :]), ml_w_out[jx].astype(BF16), xa, mod[l], g,
                                    wr2, n_ct, ml_heads=ml_heads)
        xa = _moe(h2s, logits, x1, mod[l], g, None, moe_w_gate[l], moe_w_up[l], moe_w_down[l], n_ctx, n_ct,
                  n_ct if last else 0)
    return xa
```

```python
import functools

import numpy as np
import jax
import jax.numpy as jnp
from jax import lax
from jax.experimental import pallas as pl
from jax.experimental.pallas import tpu as pltpu

F32 = jnp.float32
BF16 = jnp.bfloat16

LANE = 128
SUBLANE = 8
VMEM_LIMIT = 56 * 1024 * 1024

EPS = 1e-6
NEG = -1e30
GRID_W = 64
HEAD_DIM = 64
ML_V_DIM = 128
WIN_ROWS = 8
WIN_COLS = 16
ROPE_BASE = 10000.0
EC_CAPACITY = 2

TM = 256
Q_ROWS = TM // GRID_W
K_ROWS = Q_ROWS + WIN_ROWS
ML_CHUNK = 256


def _cp(sem):
    return pltpu.CompilerParams(dimension_semantics=sem, vmem_limit_bytes=VMEM_LIMIT)


def _dot(a, b):
    return jnp.dot(a, b, preferred_element_type=F32)


def _dot_nt(a, b):
    return lax.dot_general(a, b, (((1,), (1,)), ((), ())), preferred_element_type=F32)


def _split3(x):
    x1 = x.astype(BF16)
    r1 = x - x1.astype(F32)
    x2 = r1.astype(BF16)
    x3 = (r1 - x2.astype(F32)).astype(BF16)
    return x1, x2, x3


def _rms(x, g):
    return x * lax.rsqrt(jnp.mean(x * x, axis=-1, keepdims=True) + EPS) * g


def _silu(x):
    return x * jax.nn.sigmoid(x)


def _ada_kernel(c_ref, w_ref, b_ref, o_ref):
    s = _silu(c_ref[...]).astype(BF16)
    o_ref[0] = _dot(s, w_ref[0].astype(BF16)) + b_ref[0]


def _ada(c_all, ada_w, ada_b):
    depth, d, n = ada_w.shape
    rows = c_all.shape[0]
    tn = 1024 if n % 1024 == 0 else n
    return pl.pallas_call(
        _ada_kernel,
        grid=(depth, n // tn),
        in_specs=[pl.BlockSpec((rows, d), lambda l, j: (0, 0)),
                  pl.BlockSpec((1, d, tn), lambda l, j: (l, 0, j)),
                  pl.BlockSpec((1, 1, tn), lambda l, j: (l, 0, j))],
        out_specs=pl.BlockSpec((1, rows, tn), lambda l, j: (l, 0, j)),
        out_shape=jax.ShapeDtypeStruct((depth, rows, n), F32),
        compiler_params=_cp(("arbitrary", "arbitrary")),
        name="ada_mod",
    )(c_all, ada_w, ada_b.reshape(depth, 1, n))


def _mod_spec(batch, n_ctx_tiles, d):
    return pl.BlockSpec((1, 6, d), lambda b, i: (jnp.where(i < n_ctx_tiles, batch, b), 0, 0))


def _proj_na_kernel(x_ref, mod_ref, g_ref, w_ref, o_ref):
    m = mod_ref[0]
    h = (_rms(x_ref[0], g_ref[0:1, :]) * (1.0 + m[1:2, :]) + m[0:1, :]).astype(BF16)
    o_ref[0] = _dot(h, w_ref[...]).astype(BF16)


def _proj_na(xa, mod, g, w_bf16, n_ctx_tiles):
    batch, tt, d = xa.shape
    n = w_bf16.shape[1]
    return pl.pallas_call(
        _proj_na_kernel,
        grid=(batch, tt // TM),
        in_specs=[pl.BlockSpec((1, TM, d), lambda b, i: (b, i, 0)),
                  _mod_spec(batch, n_ctx_tiles, d),
                  pl.BlockSpec((4, d), lambda b, i: (0, 0)),
                  pl.BlockSpec((d, n), lambda b, i: (0, 0))],
        out_specs=pl.BlockSpec((1, TM, n), lambda b, i: (b, i, 0)),
        out_shape=jax.ShapeDtypeStruct((batch, tt, n), BF16),
        compiler_params=_cp(("arbitrary", "arbitrary")),
        name="proj_na",
    )(xa, mod, g, w_bf16)


def _proj_ml_kernel(x_ref, mod_ref, g_ref, w_ref, bg_ref, cos_ref, sin_ref, o_ref, gate_ref, *, qk_w, main_w):
    m = mod_ref[0]
    h = (_rms(x_ref[0], g_ref[0:1, :]) * (1.0 + m[1:2, :]) + m[0:1, :]).astype(BF16)
    r = _dot(h, w_ref[...])
    qk = r[:, :2 * qk_w]
    reps = 2 * qk_w // LANE
    cos = jnp.concatenate([cos_ref[...]] * reps, axis=1)
    sin = jnp.concatenate([sin_ref[...]] * reps, axis=1)
    lane = lax.broadcasted_iota(jnp.int32, qk.shape, 1)
    partner = jnp.where(lane % 32 < 16, pltpu.roll(qk, 2 * qk_w - 16, 1), pltpu.roll(qk, 16, 1))
    qk = qk * cos + partner * sin
    o_ref[0, :, :2 * qk_w] = qk.astype(BF16)
    o_ref[0, :, 2 * qk_w:] = r[:, 2 * qk_w:main_w].astype(BF16)
    gr = r[:, main_w:] + bg_ref[...]
    gate_ref[0, :, :LANE] = gr[:, :LANE]
    f = gr[:, LANE:]
    gate_ref[0, :, LANE:] = jnp.minimum(f, 0.0) - jnp.log1p(jnp.exp(-jnp.abs(f)))


def _proj_ml(xa, mod, g, w_bf16, bg, cos_t, sin_t, n_ctx_tiles, qk_w, main_w):
    batch, tt, d = xa.shape
    n = w_bf16.shape[1]
    kern = functools.partial(_proj_ml_kernel, qk_w=qk_w, main_w=main_w)
    return pl.pallas_call(
        kern,
        grid=(batch, tt // TM),
        in_specs=[pl.BlockSpec((1, TM, d), lambda b, i: (b, i, 0)),
                  _mod_spec(batch, n_ctx_tiles, d),
                  pl.BlockSpec((4, d), lambda b, i: (0, 0)),
                  pl.BlockSpec((d, n), lambda b, i: (0, 0)),
                  pl.BlockSpec((1, 2 * LANE), lambda b, i: (0, 0)),
                  pl.BlockSpec((TM, LANE), lambda b, i: (i, 0)),
                  pl.BlockSpec((TM, LANE), lambda b, i: (i, 0))],
        out_specs=[pl.BlockSpec((1, TM, main_w), lambda b, i: (b, i, 0)),
                   pl.BlockSpec((1, TM, 2 * LANE), lambda b, i: (b, i, 0))],
        out_shape=[jax.ShapeDtypeStruct((batch, tt, main_w), BF16),
                   jax.ShapeDtypeStruct((batch, tt, 2 * LANE), F32)],
        compiler_params=_cp(("arbitrary", "arbitrary")),
        name="proj_ml",
    )(xa, mod, g, w_bf16, bg, cos_t, sin_t)


def _na_kernel(q_ref, k_ref, v_ref, bias_ref, o_ref, *, ctx, rows):
    step = pl.program_id(2)
    n_keys = K_ROWS * GRID_W

    def attend(s_list, v_list):
        m = s_list[0].max(axis=1, keepdims=True)
        for s in s_list[1:]:
            m = jnp.maximum(m, s.max(axis=1, keepdims=True))
        num, den = 0.0, 0.0
        for s, v in zip(s_list, v_list):
            p = jnp.exp(s - m)
            den = den + p.sum(axis=1, keepdims=True)
            num = num + _dot(p.astype(BF16), v)
        return num / den

    @pl.when(step == 0)
    def _():
        outs = []
        for hh in range(2):
            sl = slice(hh * HEAD_DIM, (hh + 1) * HEAD_DIM)
            q = q_ref[0, :, sl]
            s_c = _dot_nt(q, k_ref[0, 0:ctx, sl])
            outs.append(attend([s_c], [v_ref[0, 0:ctx, sl]]))
        o_ref[0] = jnp.concatenate(outs, axis=1).astype(BF16)

    @pl.when(step > 0)
    def _():
        rb = step - 1
        start = jnp.clip(Q_ROWS * rb - WIN_ROWS // 2, 0, rows - K_ROWS)
        ks = pl.multiple_of(ctx + start * GRID_W, TM)
        outs = []
        for hh in range(2):
            sl = slice(hh * HEAD_DIM, (hh + 1) * HEAD_DIM)
            q = q_ref[0, :, sl]
            s_w = _dot_nt(q, k_ref[0, pl.ds(ks, n_keys), sl]) + bias_ref[hh, 0]
            s_c = _dot_nt(q, k_ref[0, 0:ctx, sl])
            outs.append(attend([s_w, s_c], [v_ref[0, pl.ds(ks, n_keys), sl], v_ref[0, 0:ctx, sl]]))
        o_ref[0] = jnp.concatenate(outs, axis=1).astype(BF16)


def _na_bias_table(rel_bias, rows):
    n_rb = rows // Q_ROWS
    n_dr, n_dc = 2 * WIN_ROWS - 1, 2 * WIN_COLS - 1
    qc = np.arange(GRID_W)
    c0 = np.clip(qc - WIN_COLS // 2, 0, GRID_W - WIN_COLS)
    col_ok = (qc[None, :] >= c0[:, None]) & (qc[None, :] < c0[:, None] + WIN_COLS)
    dc = np.clip(qc[None, :] - qc[:, None] + WIN_COLS - 1, 0, n_dc - 1)
    oh_c = ((np.arange(n_dc)[:, None, None] == dc[None]) & col_ok[None]).astype(np.float32)
    oh_r = np.zeros((3, n_dr, Q_ROWS, K_ROWS), np.float32)
    row_ok = np.zeros((3, Q_ROWS, K_ROWS), bool)
    for ci, rb in enumerate((0, 1, n_rb - 1)):
        start = int(np.clip(Q_ROWS * rb - WIN_ROWS // 2, 0, rows - K_ROWS))
        qr = Q_ROWS * rb + np.arange(Q_ROWS)
        kr = start + np.arange(K_ROWS)
        r0 = np.clip(qr - WIN_ROWS // 2, 0, rows - WIN_ROWS)
        row_ok[ci] = (kr[None, :] >= r0[:, None]) & (kr[None, :] < r0[:, None] + WIN_ROWS)
        dr = np.clip(kr[None, :] - qr[:, None] + WIN_ROWS - 1, 0, n_dr - 1)
        oh_r[ci] = (np.arange(n_dr)[:, None, None] == dr[None]) & row_ok[ci][None]
    hi = lax.Precision.HIGHEST
    t1 = jnp.einsum('hrc,cab->hrab', rel_bias.astype(F32), oh_c, precision=hi)
    t2 = jnp.einsum('hrab,srqk->hsqakb', t1, oh_r, precision=hi)
    ok = row_ok[:, :, None, :, None] & col_ok[None, None, :, None, :]
    return jnp.where(ok[None], t2, NEG).reshape(rel_bias.shape[0], 3, TM, K_ROWS * GRID_W)


def _na(qkv, bias_tab, ctx, rows):
    batch, tt, d3 = qkv.shape
    d = d3 // 3
    hp = d // LANE
    n_rb = rows // Q_ROWS
    n_ct = ctx // TM
    nk = K_ROWS * GRID_W

    def case(s):
        rb = s - 1
        return jnp.where(rb <= 0, 0, jnp.where(rb == n_rb - 1, 2, 1))

    kern = functools.partial(_na_kernel, ctx=ctx, rows=rows)
    return pl.pallas_call(
        kern,
        grid=(hp, batch, n_rb + 1),
        in_specs=[pl.BlockSpec((1, TM, LANE), lambda h, b, s: (b, jnp.where(s == 0, 0, s - 1 + n_ct), h)),
                  pl.BlockSpec((1, tt, LANE), lambda h, b, s: (b, 0, hp + h)),
                  pl.BlockSpec((1, tt, LANE), lambda h, b, s: (b, 0, 2 * hp + h)),
                  pl.BlockSpec((2, 1, TM, nk), lambda h, b, s: (h, case(s), 0, 0))],
        out_specs=pl.BlockSpec((1, TM, LANE), lambda h, b, s: (b, jnp.where(s == 0, 0, s - 1 + n_ct), h)),
        out_shape=jax.ShapeDtypeStruct((batch, tt, d), BF16),
        compiler_params=_cp(("arbitrary", "arbitrary", "arbitrary")),
        name="na_attention",
    )(qkv, qkv, qkv, bias_tab)


def _scan_kernel(q_ref, k_ref, v_ref, gate_ref, o_ref, c_ref, m_ref, *, heads, reverse):
    L = ML_CHUNK
    j = pl.program_id(1)

    @pl.when(j == 0)
    def _():
        c_ref[...] = jnp.zeros_like(c_ref)
        m_ref[...] = jnp.full_like(m_ref, NEG)

    row = lax.broadcasted_iota(jnp.int32, (L, L), 0)
    col = lax.broadcasted_iota(jnp.int32, (L, L), 1)
    mask = (row <= col) if reverse else (row >= col)
    tri = jnp.where(mask, 1.0, 0.0).astype(BF16)

    gi = gate_ref[0, :, :LANE]
    gf = gate_ref[0, :, LANE:]
    f1, f2, f3 = _split3(gf)
    b_col = _dot(tri, f1) + _dot(tri, f2) + _dot(tri, f3)
    b_row = b_col.T
    i_row = gi.T
    last = 0 if reverse else L - 1
    b_end = b_col[last:last + 1, :]
    k_t = k_ref[0].astype(F32).T
    m_all = m_ref[...]
    lane1 = lax.broadcasted_iota(jnp.int32, (1, LANE), 1)
    ones_col = jnp.where(lax.broadcasted_iota(jnp.int32, (L, LANE), 1) == 0, 1.0, 0.0).astype(BF16)

    for h in range(heads):
        c = (heads if reverse else 0) + h
        bc = b_col[:, c:c + 1]
        br = b_row[c:c + 1, :]
        ir = i_row[c:c + 1, :]
        mh = m_all[:, c:c + 1]
        be = b_end[:, c:c + 1]
        dmat = jnp.where(mask, bc - br + ir, NEG)
        inter = bc + mh
        m_t = jnp.maximum(dmat.max(axis=1, keepdims=True), inter)
        p = jnp.exp(dmat - m_t)
        w_inter = jnp.exp(inter - m_t)
        q = q_ref[0, :, h * HEAD_DIM:(h + 1) * HEAD_DIM]
        k = k_ref[0, :, h * HEAD_DIM:(h + 1) * HEAD_DIM]
        v_aug = jnp.concatenate([v_ref[0, :, h * ML_V_DIM:(h + 1) * ML_V_DIM], ones_col], axis=1)
        s = (_dot_nt(q, k) * p).astype(BF16)
        state = c_ref[h]
        r = _dot(s, v_aug) + w_inter * _dot(q, state.astype(BF16))
        den = r[:, ML_V_DIM:ML_V_DIM + 1]
        o_ref[0, :, h * ML_V_DIM:(h + 1) * ML_V_DIM] = r[:, :ML_V_DIM] / jnp.maximum(jnp.abs(den), jnp.exp(-m_t))
        g_row = be - br + ir
        m_new = jnp.maximum(be + mh, g_row.max(axis=1, keepdims=True))
        w_k = jnp.exp(g_row - m_new)
        decay = jnp.exp(be + mh - m_new)
        kw = (k_t[h * HEAD_DIM:(h + 1) * HEAD_DIM, :] * w_k).astype(BF16)
        c_ref[h] = decay * state + _dot(kw, v_aug)
        m_all = jnp.where(lane1 == c, m_new, m_all)
    m_ref[...] = m_all


def _scan(proj, gates, heads, ctx, reverse):
    batch, tt, _ = proj.shape
    L = ML_CHUNK
    nc = tt // L
    ncc = ctx // L
    qk_w = heads * HEAD_DIM
    v_w = heads * ML_V_DIM

    def chunk(j):
        if not reverse:
            return j
        return jnp.where(j < ncc, ncc - 1 - j, nc - 1 - (j - ncc))

    kern = functools.partial(_scan_kernel, heads=heads, reverse=reverse)
    return pl.pallas_call(
        kern,
        grid=(batch, nc),
        in_specs=[pl.BlockSpec((1, L, qk_w), lambda b, j: (b, chunk(j), 0)),
                  pl.BlockSpec((1, L, qk_w), lambda b, j: (b, chunk(j), 1)),
                  pl.BlockSpec((1, L, v_w), lambda b, j: (b, chunk(j), 2 * qk_w // v_w)),
                  pl.BlockSpec((1, L, 2 * LANE), lambda b, j: (b, chunk(j), 0))],
        out_specs=pl.BlockSpec((1, L, v_w), lambda b, j: (b, chunk(j), 0)),
        out_shape=jax.ShapeDtypeStruct((batch, tt, v_w), F32),
        scratch_shapes=[pltpu.VMEM((heads, HEAD_DIM, 2 * ML_V_DIM), F32),
                        pltpu.VMEM((1, LANE), F32)],
        compiler_params=_cp(("arbitrary", "arbitrary")),
        name="mlstm_scan_bwd" if reverse else "mlstm_scan_fwd",
    )(proj, proj, proj, gates)


def _post_common(y_in, w_ref, xa_ref, mod_ref, g_ref, wr_ref, x1_ref, h2s_ref, lg_ref):
    m = mod_ref[0]
    y = _dot(y_in, w_ref[...])
    x1 = xa_ref[0] + m[2:3, :] * _rms(y, g_ref[1:2, :])
    x1_ref[0] = x1
    h2 = _rms(x1, g_ref[2:3, :]) * (1.0 + m[4:5, :]) + m[3:4, :]
    d = h2.shape[1]
    p = d // LANE
    for jj in range(p):
        h2s_ref[0, pl.ds(jj, TM, stride=p), :] = h2[:, jj * LANE:(jj + 1) * LANE]
    h_hi = h2.astype(BF16)
    h_lo = (h2 - h_hi.astype(F32)).astype(BF16)
    lg_ref[0] = _dot(h_hi, wr_ref[0]) + _dot(h_hi, wr_ref[1]) + _dot(h_lo, wr_ref[0])


def _post_na_kernel(o_ref, w_ref, xa_ref, mod_ref, g_ref, wr_ref, x1_ref, h2s_ref, lg_ref):
    _post_common(o_ref[0], w_ref, xa_ref, mod_ref, g_ref, wr_ref, x1_ref, h2s_ref, lg_ref)


def _post_ml_kernel(hf_ref, hb_ref, og_ref, nw_ref, w_ref, xa_ref, mod_ref, g_ref, wr_ref,
                    x1_ref, h2s_ref, lg_ref, *, heads):
    hs = hf_ref[0] + hb_ref[0]
    parts = []
    for h in range(heads):
        t = hs[:, h * ML_V_DIM:(h + 1) * ML_V_DIM]
        parts.append(t * lax.rsqrt(jnp.mean(t * t, axis=-1, keepdims=True) + EPS))
    hn = jnp.concatenate(parts, axis=1) * nw_ref[...]
    y_in = (hn * jax.nn.sigmoid(og_ref[0].astype(F32))).astype(BF16)
    _post_common(y_in, w_ref, xa_ref, mod_ref, g_ref, wr_ref, x1_ref, h2s_ref, lg_ref)


def _post(mixer_inputs, w_out, xa, mod, g, wr, n_ctx_tiles, ml_heads=None):
    batch, tt, d = xa.shape
    p = d // LANE
    tile = lambda w: pl.BlockSpec((1, TM, w), lambda b, i: (b, i, 0))
    full2 = lambda a: pl.BlockSpec(a.shape, lambda b, i: (0,) * a.ndim)
    if ml_heads is None:
        (o,) = mixer_inputs
        kern = _post_na_kernel
        head_specs, head_args = [tile(d)], [o]
    else:
        hf, hb, proj, nw = mixer_inputs
        v_w = ml_heads * ML_V_DIM
        og_block = (2 * ml_heads * HEAD_DIM + v_w) // v_w
        kern = functools.partial(_post_ml_kernel, heads=ml_heads)
        head_specs = [tile(v_w), tile(v_w), pl.BlockSpec((1, TM, v_w), lambda b, i: (b, i, og_block)), full2(nw)]
        head_args = [hf, hb, proj, nw]
    return pl.pallas_call(
        kern,
        grid=(batch, tt // TM),
        in_specs=head_specs + [full2(w_out), tile(d), _mod_spec(batch, n_ctx_tiles, d), full2(g), full2(wr)],
        out_specs=[tile(d),
                   pl.BlockSpec((1, TM * p, LANE), lambda b, i: (b, i, 0)),
                   tile(LANE)],
        out_shape=[jax.ShapeDtypeStruct((batch, tt, d), F32),
                   jax.ShapeDtypeStruct((batch, tt * p, LANE), F32),
                   jax.ShapeDtypeStruct((batch, tt, LANE), F32)],
        compiler_params=_cp(("arbitrary", "arbitrary")),
        name="post_mixer",
    )(*head_args, w_out, xa, mod, g, wr)


def _route_kernel(lg_ref, idx_ref, pos_ref, col_ref, racc_ref, *, n_exp, cap):
    t = lg_ref.shape[1]
    n_tiles = t // LANE
    lt = lg_ref[0].T[:n_exp, :]
    e = jnp.exp(lt - lt.max(axis=0, keepdims=True))
    aff = e / e.sum(axis=0, keepdims=True)
    def count_ge(v):
        return jnp.where(aff >= v, 1.0, 0.0).sum(axis=1, keepdims=True)

    def bisect(i, cur):
        cand = cur | jnp.left_shift(jnp.int32(1), 30 - i)
        return jnp.where(count_ge(pltpu.bitcast(cand, F32)) >= cap, cand, cur)

    v_bits = lax.fori_loop(0, 31, bisect, jnp.zeros((n_exp, 1), jnp.int32))

    def refine(_, hi):
        pivot = jnp.where(aff < hi, aff, -1.0).max(axis=1, keepdims=True)
        return jnp.where(count_ge(pivot) >= cap, hi, pivot)

    min_normal_bits = 0x00800000
    hi = lax.fori_loop(0, 3, refine, pltpu.bitcast(jnp.maximum(v_bits + 1, min_normal_bits), F32))
    thr = jnp.where(aff < hi, aff, -1.0).max(axis=1, keepdims=True)
    gt = aff > thr
    eq = aff == thr
    need = cap - jnp.where(gt, 1.0, 0.0).sum(axis=1, keepdims=True)

    upper = jnp.where(lax.broadcasted_iota(jnp.int32, (LANE, LANE), 0)
                      <= lax.broadcasted_iota(jnp.int32, (LANE, LANE), 1), 1.0, 0.0).astype(BF16)

    def excl_cumsum(mask_f32):
        carry = jnp.zeros((n_exp, 1), F32)
        parts = []
        for jt in range(n_tiles):
            tile = mask_f32[:, jt * LANE:(jt + 1) * LANE]
            inc = _dot(tile.astype(BF16), upper)
            parts.append(inc - tile + carry)
            carry = carry + inc[:, LANE - 1:LANE]
        return jnp.concatenate(parts, axis=1)

    eq_f = jnp.where(eq, 1.0, 0.0)
    sel = gt | (eq & (excl_cumsum(eq_f) < need))
    sel_f = jnp.where(sel, 1.0, 0.0)
    pos_ref[...] = jnp.where(sel, excl_cumsum(sel_f), -1.0)

    t_iota = lax.broadcasted_iota(jnp.int32, (SUBLANE, LANE), 1).astype(F32)
    r_iota = lax.broadcasted_iota(jnp.int32, (SUBLANE, LANE), 0).astype(F32)
    col_ref[...] = jnp.zeros_like(col_ref)
    racc_ref[...] = jnp.zeros_like(racc_ref)
    unroll = 2
    for ex in range(n_exp):
        def ranks(it, _):
            for u in range(unroll):
                rc = it * unroll + u
                r0 = jnp.asarray(rc * SUBLANE, F32)
                acc = jnp.zeros((SUBLANE, LANE), F32)
                for jt in range(n_tiles):
                    prow = pos_ref[ex:ex + 1, jt * LANE:(jt + 1) * LANE]
                    acc = acc + jnp.where(prow == r_iota + r0, t_iota + float(jt * LANE), 0.0)
                racc_ref[pl.ds(pl.multiple_of(rc * SUBLANE, SUBLANE), SUBLANE), :] = acc
            return 0
        lax.fori_loop(0, cap // (SUBLANE * unroll), ranks, 0)
        col_ref[:, ex:ex + 1] = racc_ref[...].sum(axis=1, keepdims=True)
    idx_ref[0] = col_ref[...].T[:n_exp, :cap].astype(jnp.int32)


def _route(logits, n_exp, cap):
    batch, t, _ = logits.shape
    kern = functools.partial(_route_kernel, n_exp=n_exp, cap=cap)
    return pl.pallas_call(
        kern,
        grid=(batch,),
        in_specs=[pl.BlockSpec((1, t, LANE), lambda b: (b, 0, 0))],
        out_specs=pl.BlockSpec((1, n_exp, cap), lambda b: (b, 0, 0)),
        out_shape=jax.ShapeDtypeStruct((batch, n_exp, cap), jnp.int32),
        scratch_shapes=[pltpu.VMEM((n_exp, t), F32)] + [pltpu.VMEM((-(-cap // LANE) * LANE, LANE), F32)] * 2,
        compiler_params=_cp(("arbitrary",)),
        name="route",
    )(logits)


def _gather_kernel(idx_ref, src_ref, lg_ref, xe_ref, gate_ref, tile_ref, gl_ref, *, n_exp, slots, p, stride):
    unroll = 8

    def body(c, _):
        for u in range(unroll):
            r = c * unroll + u
            t = idx_ref[0, 0, 0, r]
            tile_ref[pl.ds(r, p, stride=stride), :] = src_ref[0, pl.ds(pl.multiple_of(t * p, p), p), :]
            gl_ref[pl.ds(r, 1), :] = lg_ref[0, pl.ds(t, 1), :]
        return 0

    lax.fori_loop(0, slots // unroll, body, 0)
    xe_ref[0, 0] = jnp.concatenate([tile_ref[pl.ds(jj * stride, slots), :] for jj in range(p)], axis=1).astype(BF16)
    lg = gl_ref[...]
    lane = lax.broadcasted_iota(jnp.int32, lg.shape, 1)
    lg = jnp.where(lane < n_exp, lg, NEG)
    ex = jnp.exp(lg - lg.max(axis=1, keepdims=True))
    gate_ref[0, 0] = ex / ex.sum(axis=1, keepdims=True)


def _gather(idx, h2s, logits, n_exp):
    batch, _, _, slots = idx.shape
    tt = logits.shape[1]
    p = h2s.shape[1] // tt
    d = p * LANE
    stride = slots + SUBLANE
    kern = functools.partial(_gather_kernel, n_exp=n_exp, slots=slots, p=p, stride=stride)
    return pl.pallas_call(
        kern,
        grid=(batch, n_exp),
        in_specs=[pl.BlockSpec((1, 1, 1, slots), lambda b, e: (b, e, 0, 0), memory_space=pltpu.SMEM),
                  pl.BlockSpec((1, tt * p, LANE), lambda b, e: (b, 0, 0), pipeline_mode=pl.Buffered(1)),
                  pl.BlockSpec((1, tt, LANE), lambda b, e: (b, 0, 0))],
        out_specs=[pl.BlockSpec((1, 1, slots, d), lambda b, e: (e, b, 0, 0)),
                   pl.BlockSpec((1, 1, slots, LANE), lambda b, e: (e, b, 0, 0))],
        out_shape=[jax.ShapeDtypeStruct((n_exp, batch, slots, d), BF16),
                   jax.ShapeDtypeStruct((n_exp, batch, slots, LANE), F32)],
        scratch_shapes=[pltpu.VMEM((p * stride, LANE), F32), pltpu.VMEM((slots, LANE), F32)],
        compiler_params=_cp(("arbitrary", "arbitrary")),
        name="moe_gather",
    )(idx, h2s, logits)


def _ffn_kernel(xe_ref, gate_ref, wg_hbm, wu_hbm, wd_hbm, ye_ref, wg_st, wu_st, wd_st, wgb_ref, wub_ref, wdb_ref,
                sem, *, layer, n_exp, f_chunk, cast_rows):
    e, b = pl.program_id(0), pl.program_id(1)
    stage = ((wg_hbm, wg_st, wgb_ref), (wu_hbm, wu_st, wub_ref), (wd_hbm, wd_st, wdb_ref))

    def weight_copies(ex):
        return [pltpu.make_async_copy(hbm.at[layer, ex], st, sem.at[i]) for i, (hbm, st, _) in enumerate(stage)]

    @pl.when((e == 0) & (b == 0))
    def _():
        for cp in weight_copies(0):
            cp.start()

    @pl.when(b == 0)
    def _():
        for cp in weight_copies(e):
            cp.wait()
        for _, st, dst in stage:
            def cast(r, _, st=st, dst=dst):
                rows = pl.ds(pl.multiple_of(r * cast_rows, cast_rows), cast_rows)
                dst[rows, :] = st[rows, :].astype(BF16)
                return 0
            lax.fori_loop(0, st.shape[0] // cast_rows, cast, 0)

        @pl.when(e + 1 < n_exp)
        def _():
            for cp in weight_copies(e + 1):
                cp.start()

    x = xe_ref[0, 0]
    slots, d = x.shape
    f = wgb_ref.shape[1]
    y = jnp.zeros((slots, d), F32)
    for c in range(f // f_chunk):
        cs = slice(c * f_chunk, (c + 1) * f_chunk)
        a = _dot(x, wgb_ref[:, cs])
        u = _dot(x, wub_ref[:, cs])
        y = y + _dot((_silu(a) * u).astype(BF16), wdb_ref[cs, :])
    gt = gate_ref[0, 0]
    lane = lax.broadcasted_iota(jnp.int32, gt.shape, 1)
    y = y * jnp.where(lane == e, gt, 0.0).sum(axis=1, keepdims=True)
    p = d // LANE
    for jj in range(p):
        ye_ref[0, 0, pl.ds(jj, slots, stride=p), :] = y[:, jj * LANE:(jj + 1) * LANE]


def _ffn(xe, gates, wg, wu, wd, layer):
    n_exp, batch, slots, d = xe.shape
    f = wg.shape[3]
    p = d // LANE
    kern = functools.partial(_ffn_kernel, layer=layer, n_exp=n_exp, f_chunk=min(f, 512), cast_rows=min(d, 256))
    hbm = pl.BlockSpec(memory_space=pl.ANY)
    return pl.pallas_call(
        kern,
        grid=(n_exp, batch),
        in_specs=[pl.BlockSpec((1, 1, slots, d), lambda e, b: (e, b, 0, 0)),
                  pl.BlockSpec((1, 1, slots, LANE), lambda e, b: (e, b, 0, 0)),
                  hbm, hbm, hbm],
        out_specs=pl.BlockSpec((1, 1, slots * p, LANE), lambda e, b: (e, b, 0, 0)),
        out_shape=jax.ShapeDtypeStruct((n_exp, batch, slots * p, LANE), F32),
        scratch_shapes=[pltpu.VMEM((d, f), F32), pltpu.VMEM((d, f), F32), pltpu.VMEM((f, d), F32),
                        pltpu.VMEM((d, f), BF16), pltpu.VMEM((d, f), BF16), pltpu.VMEM((f, d), BF16),
                        pltpu.SemaphoreType.DMA((3,))],
        compiler_params=_cp(("arbitrary", "arbitrary")),
        name="moe_ffn",
    )(xe, gates, wg, wu, wd)


def _combine_kernel(idx_ref, ye_ref, acc_ref, *, slots, p):
    @pl.when(pl.program_id(1) == 0)
    def _():
        acc_ref[...] = jnp.zeros_like(acc_ref)

    unroll = 4

    def body(c, _):
        rows, vals = [], []
        for u in range(unroll):
            r = c * unroll + u
            row = pl.multiple_of(idx_ref[0, 0, 0, r] * p, p)
            rows.append(row)
            vals.append(acc_ref[0, pl.ds(row, p), :] + ye_ref[0, 0, pl.ds(pl.multiple_of(r * p, p), p), :])
        for row, val in zip(rows, vals):
            acc_ref[0, pl.ds(row, p), :] = val
        return 0

    lax.fori_loop(0, slots // unroll, body, 0)


def _combine(idx, ye, tt):
    n_exp, batch, sp, _ = ye.shape
    slots = idx.shape[3]
    p = sp // slots
    kern = functools.partial(_combine_kernel, slots=slots, p=p)
    return pl.pallas_call(
        kern,
        grid=(batch, n_exp),
        in_specs=[pl.BlockSpec((1, 1, 1, slots), lambda b, e: (b, e, 0, 0), memory_space=pltpu.SMEM),
                  pl.BlockSpec((1, 1, sp, LANE), lambda b, e: (e, b, 0, 0))],
        out_specs=pl.BlockSpec((1, tt * p, LANE), lambda b, e: (b, 0, 0)),
        out_shape=jax.ShapeDtypeStruct((batch, tt * p, LANE), F32),
        compiler_params=_cp(("arbitrary", "arbitrary")),
        name="moe_combine",
    )(idx, ye)


def _final_kernel(moe_ref, x1_ref, mod_ref, g_ref, o_ref):
    p = x1_ref.shape[2] // LANE
    moe = jnp.concatenate([moe_ref[0, pl.ds(jj, TM, stride=p), :] for jj in range(p)], axis=1)
    o_ref[0] = x1_ref[0] + mod_ref[0][5:6, :] * _rms(moe, g_ref[3:4, :])


def _final(moe_s, x1, mod, g, n_ctx_tiles, skip_tiles):
    batch, tt, d = x1.shape
    p = d // LANE
    n_tiles = tt // TM - skip_tiles
    return pl.pallas_call(
        _final_kernel,
        grid=(batch, n_tiles),
        in_specs=[pl.BlockSpec((1, TM * p, LANE), lambda b, i: (b, i + skip_tiles, 0)),
                  pl.BlockSpec((1, TM, d), lambda b, i: (b, i + skip_tiles, 0)),
                  pl.BlockSpec((1, 6, d), lambda b, i: (jnp.where(i + skip_tiles < n_ctx_tiles, batch, b), 0, 0)),
                  pl.BlockSpec((4, d), lambda b, i: (0, 0))],
        out_specs=pl.BlockSpec((1, TM, d), lambda b, i: (b, i, 0)),
        out_shape=jax.ShapeDtypeStruct((batch, n_tiles * TM, d), F32),
        compiler_params=_cp(("arbitrary", "arbitrary")),
        name="post_ffn",
    )(moe_s, x1, mod, g)


def _rope_tables(ctx, seq):
    half = HEAD_DIM // 2
    inv = ROPE_BASE ** (-np.arange(0, half, 2, dtype=np.float32) / half)
    t = np.arange(seq)
    ang_r = (t // GRID_W).astype(np.float32)[:, None] * inv[None, :]
    ang_c = (t % GRID_W).astype(np.float32)[:, None] * inv[None, :]
    ang = jnp.asarray(np.concatenate([ang_r, ang_r, ang_c, ang_c], axis=1))
    sign = np.tile(np.concatenate([-np.ones(16), np.ones(16)]), 2).astype(np.float32)
    cos = jnp.concatenate([jnp.ones((ctx, HEAD_DIM), F32), jnp.cos(ang)], axis=0)
    sin = jnp.concatenate([jnp.zeros((ctx, HEAD_DIM), F32), jnp.sin(ang) * sign[None, :]], axis=0)
    return jnp.tile(cos, (1, LANE // HEAD_DIM)), jnp.tile(sin, (1, LANE // HEAD_DIM))


def _moe(h2s, logits, x1, mod, g, layer, w_gate, w_up, w_down, ctx, n_ctx_tiles, skip_tiles):
    batch, tt, d = x1.shape
    n_exp = w_gate.shape[1]
    seq = tt - ctx
    idx_ctx = _route(logits[:, :ctx], n_exp, EC_CAPACITY * ctx // n_exp)
    idx_lat = _route(logits[:, ctx:], n_exp, EC_CAPACITY * seq // n_exp)
    idx = jnp.concatenate([idx_ctx, idx_lat + ctx], axis=2)[:, :, None, :]
    xe, gates = _gather(idx, h2s, logits, n_exp)
    ye = _ffn(xe, gates, w_gate, w_up, w_down, layer)
    moe_s = _combine(idx, ye, tt)
    return _final(moe_s, x1, mod, g, n_ctx_tiles, skip_tiles)


def kernel(x, c, ctx, c_ctx, ada_w, ada_b, norm_g, na_w_qkv, na_w_out, na_rel_bias, ml_w_in, ml_b_gate,
           ml_norm_w, ml_w_out, moe_w_router, moe_w_gate, moe_w_up, moe_w_down):
    batch, seq, d = x.shape
    n_ctx = ctx.shape[1]
    depth = ada_w.shape[0]
    rows = seq // GRID_W
    n_exp = moe_w_router.shape[-1]
    ml_heads = ml_norm_w.shape[-1] // ML_V_DIM
    assert n_ctx % TM == 0 and seq % TM == 0 and n_ctx % ML_CHUNK == 0 and seq % ML_CHUNK == 0
    assert rows >= K_ROWS and d % (2 * HEAD_DIM) == 0 and n_exp <= LANE and batch < 16
    n_ct = n_ctx // TM

    xa = jnp.concatenate([ctx, x], axis=1)
    c_all = jnp.zeros((16, d), F32).at[:batch].set(c).at[batch].set(c_ctx)
    mod = _ada(c_all, ada_w, ada_b).reshape(depth, 16, 6, d)
    cos_t, sin_t = _rope_tables(n_ctx, seq)

    for l in range(depth):
        last = l == depth - 1
        jx = l // 2
        g = norm_g[l]
        wr = jnp.zeros((d, LANE), F32).at[:, :n_exp].set(moe_w_router[l])
        wr_hi = wr.astype(BF16)
        wr2 = jnp.stack([wr_hi, (wr - wr_hi.astype(F32)).astype(BF16)])
        if l % 2 == 0:
            scale = HEAD_DIM ** -0.5
            w = jnp.concatenate([na_w_qkv[jx][:, :d] * scale, na_w_qkv[jx][:, d:]], axis=1).astype(BF16)
            qkv = _proj_na(xa, mod[l], g, w, n_ct)
            o = _na(qkv, _na_bias_table(na_rel_bias[jx], rows), n_ctx, rows)
            x1, h2s, logits = _post((o,), na_w_out[jx].astype(BF16), xa, mod[l], g, wr2, n_ct)
        else:
            qk_w = ml_heads * HEAD_DIM
            v_w = ml_heads * ML_V_DIM
            main_w = 2 * qk_w + 2 * v_w
            w_in = ml_w_in[jx]
            ng = 2 * ml_heads
            pad = jnp.zeros((d, LANE - ng), F32)
            w = jnp.concatenate([w_in[:, :qk_w], w_in[:, qk_w:2 * qk_w] * (HEAD_DIM ** -0.5), w_in[:, 2 * qk_w:main_w],
                                 w_in[:, main_w:main_w + ng], pad, w_in[:, main_w + ng:], pad], axis=1).astype(BF16)
            zpad = jnp.zeros((LANE - ng,), F32)
            bg = jnp.concatenate([ml_b_gate[jx][:ng], zpad, ml_b_gate[jx][ng:], zpad])[None, :]
            proj, gates = _proj_ml(xa, mod[l], g, w, bg, cos_t, sin_t, n_ct, qk_w, main_w)
            hf = _scan(proj, gates, ml_heads, n_ctx, reverse=False)
            hb = _scan(proj, gates, ml_heads, n_ctx, reverse=True)
            x1, h2s, logits = _post((hf, hb, proj, ml_norm_w[jx][None, :]), ml_w_out[jx].astype(BF16), xa, mod[l], g,
                                    wr2, n_ct, ml_heads=ml_heads)
        xa = _moe(h2s, logits, x1, mod[l], g, l, moe_w_gate, moe_w_up, moe_w_down, n_ctx, n_ct,
                  n_ct if last else 0)
    return xa
```

```python
import functools

import numpy as np
import jax
import jax.numpy as jnp
from jax import lax
from jax.experimental import pallas as pl
from jax.experimental.pallas import tpu as pltpu

F32 = jnp.float32
BF16 = jnp.bfloat16

LANE = 128
SUBLANE = 8
VMEM_LIMIT = 56 * 1024 * 1024

EPS = 1e-6
NEG = -1e30
GRID_W = 64
HEAD_DIM = 64
ML_V_DIM = 128
WIN_ROWS = 8
WIN_COLS = 16
ROPE_BASE = 10000.0
EC_CAPACITY = 2

TM = 256
Q_ROWS = TM // GRID_W
K_ROWS = Q_ROWS + WIN_ROWS
ML_CHUNK = 256


def _cp(sem):
    return pltpu.CompilerParams(dimension_semantics=sem, vmem_limit_bytes=VMEM_LIMIT)


def _dot(a, b):
    return jnp.dot(a, b, preferred_element_type=F32)


def _dot_nt(a, b):
    return lax.dot_general(a, b, (((1,), (1,)), ((), ())), preferred_element_type=F32)


def _split3(x):
    x1 = x.astype(BF16)
    r1 = x - x1.astype(F32)
    x2 = r1.astype(BF16)
    x3 = (r1 - x2.astype(F32)).astype(BF16)
    return x1, x2, x3


def _rms(x, g):
    return x * lax.rsqrt(jnp.mean(x * x, axis=-1, keepdims=True) + EPS) * g


def _silu(x):
    return x * jax.nn.sigmoid(x)


def _ada_kernel(c_ref, w_ref, b_ref, o_ref):
    s = _silu(c_ref[...]).astype(BF16)
    o_ref[0] = _dot(s, w_ref[0].astype(BF16)) + b_ref[0]


def _ada(c_all, ada_w, ada_b):
    depth, d, n = ada_w.shape
    rows = c_all.shape[0]
    tn = 1024 if n % 1024 == 0 else n
    return pl.pallas_call(
        _ada_kernel,
        grid=(depth, n // tn),
        in_specs=[pl.BlockSpec((rows, d), lambda l, j: (0, 0)),
                  pl.BlockSpec((1, d, tn), lambda l, j: (l, 0, j)),
                  pl.BlockSpec((1, 1, tn), lambda l, j: (l, 0, j))],
        out_specs=pl.BlockSpec((1, rows, tn), lambda l, j: (l, 0, j)),
        out_shape=jax.ShapeDtypeStruct((depth, rows, n), F32),
        compiler_params=_cp(("arbitrary", "arbitrary")),
        name="ada_mod",
    )(c_all, ada_w, ada_b.reshape(depth, 1, n))


def _mod_spec(batch, n_ctx_tiles, d):
    return pl.BlockSpec((1, 6, d), lambda b, i: (jnp.where(i < n_ctx_tiles, batch, b), 0, 0))


def _proj_na_kernel(x_ref, mod_ref, g_ref, w_ref, o_ref):
    m = mod_ref[0]
    h = (_rms(x_ref[0], g_ref[0:1, :]) * (1.0 + m[1:2, :]) + m[0:1, :]).astype(BF16)
    o_ref[0] = _dot(h, w_ref[...]).astype(BF16)


def _proj_na(xa, mod, g, w_bf16, n_ctx_tiles):
    batch, tt, d = xa.shape
    n = w_bf16.shape[1]
    return pl.pallas_call(
        _proj_na_kernel,
        grid=(batch, tt // TM),
        in_specs=[pl.BlockSpec((1, TM, d), lambda b, i: (b, i, 0)),
                  _mod_spec(batch, n_ctx_tiles, d),
                  pl.BlockSpec((4, d), lambda b, i: (0, 0)),
                  pl.BlockSpec((d, n), lambda b, i: (0, 0))],
        out_specs=pl.BlockSpec((1, TM, n), lambda b, i: (b, i, 0)),
        out_shape=jax.ShapeDtypeStruct((batch, tt, n), BF16),
        compiler_params=_cp(("arbitrary", "arbitrary")),
        name="proj_na",
    )(xa, mod, g, w_bf16)


def _log_sigmoid(f):
    return jnp.minimum(f, 0.0) - jnp.log1p(jnp.exp(-jnp.abs(f)))


def _proj_ml_kernel(x_ref, mod_ref, g_ref, wk_ref, wt_ref, bg_ref, bgt_ref, cos_ref, sin_ref, cost_ref, sint_ref,
                    k_ref, gate_ref, qt_ref, vt_ref, ot_ref, gatet_ref, *, qk_w, v_w):
    m = mod_ref[0]
    h = (_rms(x_ref[0], g_ref[0:1, :]) * (1.0 + m[1:2, :]) + m[0:1, :]).astype(BF16)
    r = _dot(h, wk_ref[...])
    k = r[:, :qk_w]
    reps = qk_w // LANE
    lane = lax.broadcasted_iota(jnp.int32, k.shape, 1)
    partner = jnp.where(lane % 32 < 16, pltpu.roll(k, qk_w - 16, 1), pltpu.roll(k, 16, 1))
    k = k * jnp.concatenate([cos_ref[...]] * reps, axis=1) + partner * jnp.concatenate([sin_ref[...]] * reps, axis=1)
    k_ref[0] = k.astype(BF16)
    gr = r[:, qk_w:] + bg_ref[...]
    gate_ref[0, :, :LANE] = gr[:, :LANE]
    gate_ref[0, :, LANE:] = _log_sigmoid(gr[:, LANE:])

    rt = _dot_nt(wt_ref[...], h)
    q = rt[:qk_w]
    reps = qk_w // HEAD_DIM
    row = lax.broadcasted_iota(jnp.int32, q.shape, 0)
    partner = jnp.where(row % 32 < 16, pltpu.roll(q, qk_w - 16, 0), pltpu.roll(q, 16, 0))
    q = q * jnp.concatenate([cost_ref[...]] * reps, axis=0) + partner * jnp.concatenate([sint_ref[...]] * reps, axis=0)
    qt_ref[0] = q.astype(BF16)
    vt_ref[0] = rt[qk_w:qk_w + v_w].astype(BF16)
    ot_ref[0] = rt[qk_w + v_w:qk_w + 2 * v_w].astype(BF16)
    gt = rt[qk_w + 2 * v_w:] + bgt_ref[...]
    gatet_ref[0, :LANE, :] = gt[:LANE]
    gatet_ref[0, LANE:, :] = _log_sigmoid(gt[LANE:])


def _proj_ml(xa, mod, g, wk, wt, bg, bgt, tables, n_ctx_tiles, qk_w, v_w):
    batch, tt, d = xa.shape
    cos_t, sin_t, cos_tt, sin_tt = tables
    kern = functools.partial(_proj_ml_kernel, qk_w=qk_w, v_w=v_w)
    const = lambda a: pl.BlockSpec(a.shape, lambda b, i: (0,) * a.ndim)
    tok = lambda w: pl.BlockSpec((1, TM, w), lambda b, i: (b, i, 0))
    feat = lambda w: pl.BlockSpec((1, w, TM), lambda b, i: (b, 0, i))
    return pl.pallas_call(
        kern,
        grid=(batch, tt // TM),
        in_specs=[tok(d), _mod_spec(batch, n_ctx_tiles, d), const(g), const(wk), const(wt), const(bg), const(bgt),
                  pl.BlockSpec((TM, LANE), lambda b, i: (i, 0)),
                  pl.BlockSpec((TM, LANE), lambda b, i: (i, 0)),
                  pl.BlockSpec((HEAD_DIM, TM), lambda b, i: (0, i)),
                  pl.BlockSpec((HEAD_DIM, TM), lambda b, i: (0, i))],
        out_specs=[tok(qk_w), tok(2 * LANE), feat(qk_w), feat(v_w), feat(v_w), feat(2 * LANE)],
        out_shape=[jax.ShapeDtypeStruct((batch, tt, qk_w), BF16),
                   jax.ShapeDtypeStruct((batch, tt, 2 * LANE), F32),
                   jax.ShapeDtypeStruct((batch, qk_w, tt), BF16),
                   jax.ShapeDtypeStruct((batch, v_w, tt), BF16),
                   jax.ShapeDtypeStruct((batch, v_w, tt), BF16),
                   jax.ShapeDtypeStruct((batch, 2 * LANE, tt), F32)],
        compiler_params=_cp(("arbitrary", "arbitrary")),
        name="proj_ml",
    )(xa, mod, g, wk, wt, bg, bgt, cos_t, sin_t, cos_tt, sin_tt)


def _na_kernel(q_ref, k_ref, v_ref, bias_ref, o_ref, *, ctx, rows):
    step = pl.program_id(2)
    n_keys = K_ROWS * GRID_W

    def attend(s_list, v_list):
        m = s_list[0].max(axis=1, keepdims=True)
        for s in s_list[1:]:
            m = jnp.maximum(m, s.max(axis=1, keepdims=True))
        num, den = 0.0, 0.0
        for s, v in zip(s_list, v_list):
            p = jnp.exp(s - m)
            den = den + p.sum(axis=1, keepdims=True)
            num = num + _dot(p.astype(BF16), v)
        return num / den

    @pl.when(step == 0)
    def _():
        outs = []
        for hh in range(2):
            sl = slice(hh * HEAD_DIM, (hh + 1) * HEAD_DIM)
            q = q_ref[0, :, sl]
            s_c = _dot_nt(q, k_ref[0, 0:ctx, sl])
            outs.append(attend([s_c], [v_ref[0, 0:ctx, sl]]))
        o_ref[0] = jnp.concatenate(outs, axis=1).astype(BF16)

    @pl.when(step > 0)
    def _():
        rb = step - 1
        start = jnp.clip(Q_ROWS * rb - WIN_ROWS // 2, 0, rows - K_ROWS)
        ks = pl.multiple_of(ctx + start * GRID_W, TM)
        outs = []
        for hh in range(2):
            sl = slice(hh * HEAD_DIM, (hh + 1) * HEAD_DIM)
            q = q_ref[0, :, sl]
            s_w = _dot_nt(q, k_ref[0, pl.ds(ks, n_keys), sl]) + bias_ref[hh, 0]
            s_c = _dot_nt(q, k_ref[0, 0:ctx, sl])
            outs.append(attend([s_w, s_c], [v_ref[0, pl.ds(ks, n_keys), sl], v_ref[0, 0:ctx, sl]]))
        o_ref[0] = jnp.concatenate(outs, axis=1).astype(BF16)


def _na_bias_table(rel_bias, rows):
    n_rb = rows // Q_ROWS
    n_dr, n_dc = 2 * WIN_ROWS - 1, 2 * WIN_COLS - 1
    qc = np.arange(GRID_W)
    c0 = np.clip(qc - WIN_COLS // 2, 0, GRID_W - WIN_COLS)
    col_ok = (qc[None, :] >= c0[:, None]) & (qc[None, :] < c0[:, None] + WIN_COLS)
    dc = np.clip(qc[None, :] - qc[:, None] + WIN_COLS - 1, 0, n_dc - 1)
    oh_c = ((np.arange(n_dc)[:, None, None] == dc[None]) & col_ok[None]).astype(np.float32)
    oh_r = np.zeros((3, n_dr, Q_ROWS, K_ROWS), np.float32)
    row_ok = np.zeros((3, Q_ROWS, K_ROWS), bool)
    for ci, rb in enumerate((0, 1, n_rb - 1)):
        start = int(np.clip(Q_ROWS * rb - WIN_ROWS // 2, 0, rows - K_ROWS))
        qr = Q_ROWS * rb + np.arange(Q_ROWS)
        kr = start + np.arange(K_ROWS)
        r0 = np.clip(qr - WIN_ROWS // 2, 0, rows - WIN_ROWS)
        row_ok[ci] = (kr[None, :] >= r0[:, None]) & (kr[None, :] < r0[:, None] + WIN_ROWS)
        dr = np.clip(kr[None, :] - qr[:, None] + WIN_ROWS - 1, 0, n_dr - 1)
        oh_r[ci] = (np.arange(n_dr)[:, None, None] == dr[None]) & row_ok[ci][None]
    hi = lax.Precision.HIGHEST
    t1 = jnp.einsum('hrc,cab->hrab', rel_bias.astype(F32), oh_c, precision=hi)
    t2 = jnp.einsum('hrab,srqk->hsqakb', t1, oh_r, precision=hi)
    ok = row_ok[:, :, None, :, None] & col_ok[None, None, :, None, :]
    return jnp.where(ok[None], t2, NEG).reshape(rel_bias.shape[0], 3, TM, K_ROWS * GRID_W)


def _na(qkv, bias_tab, ctx, rows):
    batch, tt, d3 = qkv.shape
    d = d3 // 3
    hp = d // LANE
    n_rb = rows // Q_ROWS
    n_ct = ctx // TM
    nk = K_ROWS * GRID_W

    def case(s):
        rb = s - 1
        return jnp.where(rb <= 0, 0, jnp.where(rb == n_rb - 1, 2, 1))

    kern = functools.partial(_na_kernel, ctx=ctx, rows=rows)
    return pl.pallas_call(
        kern,
        grid=(hp, batch, n_rb + 1),
        in_specs=[pl.BlockSpec((1, TM, LANE), lambda h, b, s: (b, jnp.where(s == 0, 0, s - 1 + n_ct), h)),
                  pl.BlockSpec((1, tt, LANE), lambda h, b, s: (b, 0, hp + h)),
                  pl.BlockSpec((1, tt, LANE), lambda h, b, s: (b, 0, 2 * hp + h)),
                  pl.BlockSpec((2, 1, TM, nk), lambda h, b, s: (h, case(s), 0, 0))],
        out_specs=pl.BlockSpec((1, TM, LANE), lambda h, b, s: (b, jnp.where(s == 0, 0, s - 1 + n_ct), h)),
        out_shape=jax.ShapeDtypeStruct((batch, tt, d), BF16),
        compiler_params=_cp(("arbitrary", "arbitrary", "arbitrary")),
        name="na_attention",
    )(qkv, qkv, qkv, bias_tab)


def _scan_kernel(qt_ref, k_ref, vt_ref, gate_ref, gatet_ref, o_ref, c_ref, m_ref, *, heads, reverse):
    L = ML_CHUNK
    j = pl.program_id(1)

    @pl.when(j == 0)
    def _():
        c_ref[...] = jnp.zeros_like(c_ref)
        m_ref[...] = jnp.full_like(m_ref, NEG)

    r_i = lax.broadcasted_iota(jnp.int32, (L, L), 0)
    c_i = lax.broadcasted_iota(jnp.int32, (L, L), 1)
    seen = (r_i >= c_i) if reverse else (r_i <= c_i)
    tri_row = jnp.where(seen, 1.0, 0.0).astype(BF16)
    tri_col = jnp.where((r_i <= c_i) if reverse else (r_i >= c_i), 1.0, 0.0).astype(BF16)

    f1, f2, f3 = _split3(gate_ref[0, :, LANE:])
    b_col = _dot(tri_col, f1) + _dot(tri_col, f2) + _dot(tri_col, f3)
    u_col = gate_ref[0, :, :LANE] - b_col
    f1, f2, f3 = _split3(gatet_ref[0, LANE:, :])
    b_row = _dot(f1, tri_row) + _dot(f2, tri_row) + _dot(f3, tri_row)
    i_row = gatet_ref[0, :LANE, :]
    last = 0 if reverse else L - 1
    ones_rows = jnp.where(lax.broadcasted_iota(jnp.int32, (ML_V_DIM, L), 0) == 0, 1.0, 0.0).astype(BF16)

    for h in range(heads):
        c = (heads if reverse else 0) + h
        br = b_row[c:c + 1, :]
        ir = i_row[c:c + 1, :]
        mh = m_ref[h]
        mh_row = jnp.concatenate([mh] * (L // LANE), axis=1)
        a_t = jnp.where(seen, jnp.broadcast_to(u_col[:, c:c + 1], (L, L)), NEG)
        mu = jnp.maximum(a_t.max(axis=0, keepdims=True), mh_row)
        p_t = jnp.exp(a_t - mu)
        w_inter = jnp.exp(mh_row - mu)
        kh = k_ref[0, :, h * HEAD_DIM:(h + 1) * HEAD_DIM]
        state = c_ref[h]
        q_t = qt_ref[0, h * HEAD_DIM:(h + 1) * HEAD_DIM, :]
        r1 = _dot(jnp.concatenate([kh, state.astype(BF16)], axis=0), q_t)
        sp = (r1[:L] * p_t).astype(BF16)
        v_aug = jnp.concatenate([vt_ref[0, h * ML_V_DIM:(h + 1) * ML_V_DIM, :], ones_rows], axis=0)
        rt = _dot(v_aug, sp) + w_inter * r1[L:]
        den = rt[ML_V_DIM:ML_V_DIM + 1]
        o_ref[0, h * ML_V_DIM:(h + 1) * ML_V_DIM, :] = rt[:ML_V_DIM] / jnp.maximum(jnp.abs(den), jnp.exp(-(br + mu)))
        be = br[:, last:last + 1]
        m1 = mh[:, 0:1]
        g_row = be - br + ir
        m_new = jnp.maximum(be + m1, g_row.max(axis=1, keepdims=True))
        w_k = jnp.exp(g_row - m_new)
        decay = jnp.exp(be + m1 - m_new)
        c_ref[h] = decay * state + _dot((v_aug.astype(F32) * w_k).astype(BF16), kh)
        m_ref[h] = jnp.broadcast_to(m_new, (1, LANE))


def _scan(qt, k, vt, gates, gates_t, heads, ctx, reverse):
    batch, tt, qk_w = k.shape
    L = ML_CHUNK
    nc = tt // L
    ncc = ctx // L
    v_w = heads * ML_V_DIM

    def chunk(j):
        if not reverse:
            return j
        return jnp.where(j < ncc, ncc - 1 - j, nc - 1 - (j - ncc))

    kern = functools.partial(_scan_kernel, heads=heads, reverse=reverse)
    feat = lambda w: pl.BlockSpec((1, w, L), lambda b, j: (b, 0, chunk(j)))
    tok = lambda w: pl.BlockSpec((1, L, w), lambda b, j: (b, chunk(j), 0))
    return pl.pallas_call(
        kern,
        grid=(batch, nc),
        in_specs=[feat(qk_w), tok(qk_w), feat(v_w), tok(2 * LANE), feat(2 * LANE)],
        out_specs=feat(v_w),
        out_shape=jax.ShapeDtypeStruct((batch, v_w, tt), F32),
        scratch_shapes=[pltpu.VMEM((heads, 2 * ML_V_DIM, HEAD_DIM), F32),
                        pltpu.VMEM((heads, 1, LANE), F32)],
        compiler_params=_cp(("arbitrary", "arbitrary")),
        name="mlstm_scan_bwd" if reverse else "mlstm_scan_fwd",
    )(qt, k, vt, gates, gates_t)


def _post_common(y, xa_ref, mod_ref, g_ref, wr_ref, x1_ref, h2s_ref, lg_ref):
    m = mod_ref[0]
    x1 = xa_ref[0] + m[2:3, :] * _rms(y, g_ref[1:2, :])
    x1_ref[0] = x1
    h2 = _rms(x1, g_ref[2:3, :]) * (1.0 + m[4:5, :]) + m[3:4, :]
    d = h2.shape[1]
    p = d // LANE
    for jj in range(p):
        h2s_ref[0, pl.ds(jj, TM, stride=p), :] = h2[:, jj * LANE:(jj + 1) * LANE]
    h_hi = h2.astype(BF16)
    h_lo = (h2 - h_hi.astype(F32)).astype(BF16)
    lg_ref[0] = _dot(h_hi, wr_ref[0]) + _dot(h_hi, wr_ref[1]) + _dot(h_lo, wr_ref[0])


def _post_na_kernel(o_ref, w_ref, xa_ref, mod_ref, g_ref, wr_ref, x1_ref, h2s_ref, lg_ref):
    _post_common(_dot(o_ref[0], w_ref[...]), xa_ref, mod_ref, g_ref, wr_ref, x1_ref, h2s_ref, lg_ref)


def _post_ml_kernel(hf_ref, hb_ref, og_ref, nw_ref, wt_ref, xa_ref, mod_ref, g_ref, wr_ref,
                    x1_ref, h2s_ref, lg_ref, *, heads):
    hs = hf_ref[0] + hb_ref[0]
    parts = []
    for h in range(heads):
        t = hs[h * ML_V_DIM:(h + 1) * ML_V_DIM, :]
        parts.append(t * lax.rsqrt(jnp.mean(t * t, axis=0, keepdims=True) + EPS))
    nw = jnp.concatenate([nw_ref[...]] * (TM // LANE), axis=1)
    hn = jnp.concatenate(parts, axis=0) * nw
    y_in = (hn * jax.nn.sigmoid(og_ref[0].astype(F32))).astype(BF16)
    y = _dot(wt_ref[...], y_in).T
    _post_common(y, xa_ref, mod_ref, g_ref, wr_ref, x1_ref, h2s_ref, lg_ref)


def _post(mixer_inputs, w_out, xa, mod, g, wr, n_ctx_tiles, ml_heads=None):
    batch, tt, d = xa.shape
    p = d // LANE
    tile = lambda w: pl.BlockSpec((1, TM, w), lambda b, i: (b, i, 0))
    full2 = lambda a: pl.BlockSpec(a.shape, lambda b, i: (0,) * a.ndim)
    if ml_heads is None:
        (o,) = mixer_inputs
        kern = _post_na_kernel
        head_specs, head_args = [tile(d)], [o]
    else:
        hf, hb, og, nw = mixer_inputs
        v_w = ml_heads * ML_V_DIM
        kern = functools.partial(_post_ml_kernel, heads=ml_heads)
        feat = pl.BlockSpec((1, v_w, TM), lambda b, i: (b, 0, i))
        head_specs = [feat, feat, feat, full2(nw)]
        head_args = [hf, hb, og, nw]
    return pl.pallas_call(
        kern,
        grid=(batch, tt // TM),
        in_specs=head_specs + [full2(w_out), tile(d), _mod_spec(batch, n_ctx_tiles, d), full2(g), full2(wr)],
        out_specs=[tile(d),
                   pl.BlockSpec((1, TM * p, LANE), lambda b, i: (b, i, 0)),
                   tile(LANE)],
        out_shape=[jax.ShapeDtypeStruct((batch, tt, d), F32),
                   jax.ShapeDtypeStruct((batch, tt * p, LANE), F32),
                   jax.ShapeDtypeStruct((batch, tt, LANE), F32)],
        compiler_params=_cp(("arbitrary", "arbitrary")),
        name="post_mixer",
    )(*head_args, w_out, xa, mod, g, wr)


def _route_kernel(lg_ref, idx_ref, pos_ref, col_ref, racc_ref, *, n_exp, cap):
    t = lg_ref.shape[1]
    n_tiles = t // LANE
    lt = lg_ref[0].T[:n_exp, :]
    e = jnp.exp(lt - lt.max(axis=0, keepdims=True))
    aff = e / e.sum(axis=0, keepdims=True)
    def count_ge(v):
        return jnp.where(aff >= v, 1.0, 0.0).sum(axis=1, keepdims=True)

    def bisect(i, cur):
        cand = cur | jnp.left_shift(jnp.int32(1), 30 - i)
        return jnp.where(count_ge(pltpu.bitcast(cand, F32)) >= cap, cand, cur)

    v_bits = lax.fori_loop(0, 31, bisect, jnp.zeros((n_exp, 1), jnp.int32))

    def refine(_, hi):
        pivot = jnp.where(aff < hi, aff, -1.0).max(axis=1, keepdims=True)
        return jnp.where(count_ge(pivot) >= cap, hi, pivot)

    min_normal_bits = 0x00800000
    hi = lax.fori_loop(0, 3, refine, pltpu.bitcast(jnp.maximum(v_bits + 1, min_normal_bits), F32))
    thr = jnp.where(aff < hi, aff, -1.0).max(axis=1, keepdims=True)
    gt = aff > thr
    eq = aff == thr
    need = cap - jnp.where(gt, 1.0, 0.0).sum(axis=1, keepdims=True)

    upper = jnp.where(lax.broadcasted_iota(jnp.int32, (LANE, LANE), 0)
                      <= lax.broadcasted_iota(jnp.int32, (LANE, LANE), 1), 1.0, 0.0).astype(BF16)

    def excl_cumsum(mask_f32):
        carry = jnp.zeros((n_exp, 1), F32)
        parts = []
        for jt in range(n_tiles):
            tile = mask_f32[:, jt * LANE:(jt + 1) * LANE]
            inc = _dot(tile.astype(BF16), upper)
            parts.append(inc - tile + carry)
            carry = carry + inc[:, LANE - 1:LANE]
        return jnp.concatenate(parts, axis=1)

    eq_f = jnp.where(eq, 1.0, 0.0)
    sel = gt | (eq & (excl_cumsum(eq_f) < need))
    sel_f = jnp.where(sel, 1.0, 0.0)
    pos_ref[...] = jnp.where(sel, excl_cumsum(sel_f), -1.0)

    t_iota = lax.broadcasted_iota(jnp.int32, (SUBLANE, LANE), 1).astype(F32)
    r_iota = lax.broadcasted_iota(jnp.int32, (SUBLANE, LANE), 0).astype(F32)
    col_ref[...] = jnp.zeros_like(col_ref)
    racc_ref[...] = jnp.zeros_like(racc_ref)
    unroll = 2
    for ex in range(n_exp):
        def ranks(it, _):
            for u in range(unroll):
                rc = it * unroll + u
                r0 = jnp.asarray(rc * SUBLANE, F32)
                acc = jnp.zeros((SUBLANE, LANE), F32)
                for jt in range(n_tiles):
                    prow = pos_ref[ex:ex + 1, jt * LANE:(jt + 1) * LANE]
                    acc = acc + jnp.where(prow == r_iota + r0, t_iota + float(jt * LANE), 0.0)
                racc_ref[pl.ds(pl.multiple_of(rc * SUBLANE, SUBLANE), SUBLANE), :] = acc
            return 0
        lax.fori_loop(0, cap // (SUBLANE * unroll), ranks, 0)
        col_ref[:, ex:ex + 1] = racc_ref[...].sum(axis=1, keepdims=True)
    idx_ref[0] = col_ref[...].T[:n_exp, :cap].astype(jnp.int32)


def _route(logits, n_exp, cap):
    batch, t, _ = logits.shape
    kern = functools.partial(_route_kernel, n_exp=n_exp, cap=cap)
    return pl.pallas_call(
        kern,
        grid=(batch,),
        in_specs=[pl.BlockSpec((1, t, LANE), lambda b: (b, 0, 0))],
        out_specs=pl.BlockSpec((1, n_exp, cap), lambda b: (b, 0, 0)),
        out_shape=jax.ShapeDtypeStruct((batch, n_exp, cap), jnp.int32),
        scratch_shapes=[pltpu.VMEM((n_exp, t), F32)] + [pltpu.VMEM((-(-cap // LANE) * LANE, LANE), F32)] * 2,
        compiler_params=_cp(("arbitrary",)),
        name="route",
    )(logits)


def _gather_kernel(idx_ref, src_ref, lg_ref, xe_ref, gate_ref, tile_ref, gl_ref, *, n_exp, slots, p, stride):
    unroll = 8

    def body(c, _):
        for u in range(unroll):
            r = c * unroll + u
            t = idx_ref[0, 0, 0, r]
            tile_ref[pl.ds(r, p, stride=stride), :] = src_ref[0, pl.ds(pl.multiple_of(t * p, p), p), :]
            gl_ref[pl.ds(r, 1), :] = lg_ref[0, pl.ds(t, 1), :]
        return 0

    lax.fori_loop(0, slots // unroll, body, 0)
    xe_ref[0, 0] = jnp.concatenate([tile_ref[pl.ds(jj * stride, slots), :] for jj in range(p)], axis=1).astype(BF16)
    lg = gl_ref[...]
    lane = lax.broadcasted_iota(jnp.int32, lg.shape, 1)
    lg = jnp.where(lane < n_exp, lg, NEG)
    ex = jnp.exp(lg - lg.max(axis=1, keepdims=True))
    gate_ref[0, 0] = ex / ex.sum(axis=1, keepdims=True)


def _gather(idx, h2s, logits, n_exp):
    batch, _, _, slots = idx.shape
    tt = logits.shape[1]
    p = h2s.shape[1] // tt
    d = p * LANE
    stride = slots + SUBLANE
    kern = functools.partial(_gather_kernel, n_exp=n_exp, slots=slots, p=p, stride=stride)
    return pl.pallas_call(
        kern,
        grid=(batch, n_exp),
        in_specs=[pl.BlockSpec((1, 1, 1, slots), lambda b, e: (b, e, 0, 0), memory_space=pltpu.SMEM),
                  pl.BlockSpec((1, tt * p, LANE), lambda b, e: (b, 0, 0), pipeline_mode=pl.Buffered(1)),
                  pl.BlockSpec((1, tt, LANE), lambda b, e: (b, 0, 0))],
        out_specs=[pl.BlockSpec((1, 1, slots, d), lambda b, e: (e, b, 0, 0)),
                   pl.BlockSpec((1, 1, slots, LANE), lambda b, e: (e, b, 0, 0))],
        out_shape=[jax.ShapeDtypeStruct((n_exp, batch, slots, d), BF16),
                   jax.ShapeDtypeStruct((n_exp, batch, slots, LANE), F32)],
        scratch_shapes=[pltpu.VMEM((p * stride, LANE), F32), pltpu.VMEM((slots, LANE), F32)],
        compiler_params=_cp(("arbitrary", "arbitrary")),
        name="moe_gather",
    )(idx, h2s, logits)


def _ffn_kernel(xe_ref, gate_ref, wg_hbm, wu_hbm, wd_hbm, ye_ref, wg_st, wu_st, wd_st, wgb_ref, wub_ref, wdb_ref,
                sem, *, layer, n_exp, f_chunk, cast_rows):
    e, b = pl.program_id(0), pl.program_id(1)
    stage = ((wg_hbm, wg_st, wgb_ref), (wu_hbm, wu_st, wub_ref), (wd_hbm, wd_st, wdb_ref))

    def weight_copies(ex):
        return [pltpu.make_async_copy(hbm.at[layer, ex], st, sem.at[i]) for i, (hbm, st, _) in enumerate(stage)]

    @pl.when((e == 0) & (b == 0))
    def _():
        for cp in weight_copies(0):
            cp.start()

    @pl.when(b == 0)
    def _():
        for cp in weight_copies(e):
            cp.wait()
        for _, st, dst in stage:
            def cast(r, _, st=st, dst=dst):
                rows = pl.ds(pl.multiple_of(r * cast_rows, cast_rows), cast_rows)
                dst[rows, :] = st[rows, :].astype(BF16)
                return 0
            lax.fori_loop(0, st.shape[0] // cast_rows, cast, 0)

        @pl.when(e + 1 < n_exp)
        def _():
            for cp in weight_copies(e + 1):
                cp.start()

    x = xe_ref[0, 0]
    slots, d = x.shape
    f = wgb_ref.shape[1]
    y = jnp.zeros((slots, d), F32)
    for c in range(f // f_chunk):
        cs = slice(c * f_chunk, (c + 1) * f_chunk)
        a = _dot(x, wgb_ref[:, cs])
        u = _dot(x, wub_ref[:, cs])
        y = y + _dot((_silu(a) * u).astype(BF16), wdb_ref[cs, :])
    gt = gate_ref[0, 0]
    lane = lax.broadcasted_iota(jnp.int32, gt.shape, 1)
    y = y * jnp.where(lane == e, gt, 0.0).sum(axis=1, keepdims=True)
    p = d // LANE
    for jj in range(p):
        ye_ref[0, 0, pl.ds(jj, slots, stride=p), :] = y[:, jj * LANE:(jj + 1) * LANE]


def _ffn(xe, gates, wg, wu, wd, layer):
    n_exp, batch, slots, d = xe.shape
    f = wg.shape[3]
    p = d // LANE
    kern = functools.partial(_ffn_kernel, layer=layer, n_exp=n_exp, f_chunk=min(f, 512), cast_rows=min(d, 256))
    hbm = pl.BlockSpec(memory_space=pl.ANY)
    return pl.pallas_call(
        kern,
        grid=(n_exp, batch),
        in_specs=[pl.BlockSpec((1, 1, slots, d), lambda e, b: (e, b, 0, 0)),
                  pl.BlockSpec((1, 1, slots, LANE), lambda e, b: (e, b, 0, 0)),
                  hbm, hbm, hbm],
        out_specs=pl.BlockSpec((1, 1, slots * p, LANE), lambda e, b: (e, b, 0, 0)),
        out_shape=jax.ShapeDtypeStruct((n_exp, batch, slots * p, LANE), F32),
        scratch_shapes=[pltpu.VMEM((d, f), F32), pltpu.VMEM((d, f), F32), pltpu.VMEM((f, d), F32),
                        pltpu.VMEM((d, f), BF16), pltpu.VMEM((d, f), BF16), pltpu.VMEM((f, d), BF16),
                        pltpu.SemaphoreType.DMA((3,))],
        compiler_params=_cp(("arbitrary", "arbitrary")),
        name="moe_ffn",
    )(xe, gates, wg, wu, wd)


def _combine_kernel(idx_ref, ye_ref, acc_ref, *, slots, p):
    @pl.when(pl.program_id(1) == 0)
    def _():
        acc_ref[...] = jnp.zeros_like(acc_ref)

    unroll = 4

    def body(c, _):
        rows, vals = [], []
        for u in range(unroll):
            r = c * unroll + u
            row = pl.multiple_of(idx_ref[0, 0, 0, r] * p, p)
            rows.append(row)
            vals.append(acc_ref[0, pl.ds(row, p), :] + ye_ref[0, 0, pl.ds(pl.multiple_of(r * p, p), p), :])
        for row, val in zip(rows, vals):
            acc_ref[0, pl.ds(row, p), :] = val
        return 0

    lax.fori_loop(0, slots // unroll, body, 0)


def _combine(idx, ye, tt):
    n_exp, batch, sp, _ = ye.shape
    slots = idx.shape[3]
    p = sp // slots
    kern = functools.partial(_combine_kernel, slots=slots, p=p)
    return pl.pallas_call(
        kern,
        grid=(batch, n_exp),
        in_specs=[pl.BlockSpec((1, 1, 1, slots), lambda b, e: (b, e, 0, 0), memory_space=pltpu.SMEM),
                  pl.BlockSpec((1, 1, sp, LANE), lambda b, e: (e, b, 0, 0))],
        out_specs=pl.BlockSpec((1, tt * p, LANE), lambda b, e: (b, 0, 0)),
        out_shape=jax.ShapeDtypeStruct((batch, tt * p, LANE), F32),
        compiler_params=_cp(("arbitrary", "arbitrary")),
        name="moe_combine",
    )(idx, ye)


def _final_kernel(moe_ref, x1_ref, mod_ref, g_ref, o_ref):
    p = x1_ref.shape[2] // LANE
    moe = jnp.concatenate([moe_ref[0, pl.ds(jj, TM, stride=p), :] for jj in range(p)], axis=1)
    o_ref[0] = x1_ref[0] + mod_ref[0][5:6, :] * _rms(moe, g_ref[3:4, :])


def _final(moe_s, x1, mod, g, n_ctx_tiles, skip_tiles):
    batch, tt, d = x1.shape
    p = d // LANE
    n_tiles = tt // TM - skip_tiles
    return pl.pallas_call(
        _final_kernel,
        grid=(batch, n_tiles),
        in_specs=[pl.BlockSpec((1, TM * p, LANE), lambda b, i: (b, i + skip_tiles, 0)),
                  pl.BlockSpec((1, TM, d), lambda b, i: (b, i + skip_tiles, 0)),
                  pl.BlockSpec((1, 6, d), lambda b, i: (jnp.where(i + skip_tiles < n_ctx_tiles, batch, b), 0, 0)),
                  pl.BlockSpec((4, d), lambda b, i: (0, 0))],
        out_specs=pl.BlockSpec((1, TM, d), lambda b, i: (b, i, 0)),
        out_shape=jax.ShapeDtypeStruct((batch, n_tiles * TM, d), F32),
        compiler_params=_cp(("arbitrary", "arbitrary")),
        name="post_ffn",
    )(moe_s, x1, mod, g)


def _rope_tables(ctx, seq):
    half = HEAD_DIM // 2
    inv = ROPE_BASE ** (-np.arange(0, half, 2, dtype=np.float32) / half)
    t = np.arange(seq)
    ang_r = (t // GRID_W).astype(np.float32)[:, None] * inv[None, :]
    ang_c = (t % GRID_W).astype(np.float32)[:, None] * inv[None, :]
    ang = jnp.asarray(np.concatenate([ang_r, ang_r, ang_c, ang_c], axis=1))
    sign = np.tile(np.concatenate([-np.ones(16), np.ones(16)]), 2).astype(np.float32)
    cos = jnp.concatenate([jnp.ones((ctx, HEAD_DIM), F32), jnp.cos(ang)], axis=0)
    sin = jnp.concatenate([jnp.zeros((ctx, HEAD_DIM), F32), jnp.sin(ang) * sign[None, :]], axis=0)
    return jnp.tile(cos, (1, LANE // HEAD_DIM)), jnp.tile(sin, (1, LANE // HEAD_DIM)), cos.T, sin.T


def _moe(h2s, logits, x1, mod, g, layer, w_gate, w_up, w_down, ctx, n_ctx_tiles, skip_tiles):
    batch, tt, d = x1.shape
    n_exp = w_gate.shape[1]
    seq = tt - ctx
    idx_ctx = _route(logits[:, :ctx], n_exp, EC_CAPACITY * ctx // n_exp)
    idx_lat = _route(logits[:, ctx:], n_exp, EC_CAPACITY * seq // n_exp)
    idx = jnp.concatenate([idx_ctx, idx_lat + ctx], axis=2)[:, :, None, :]
    xe, gates = _gather(idx, h2s, logits, n_exp)
    ye = _ffn(xe, gates, w_gate, w_up, w_down, layer)
    moe_s = _combine(idx, ye, tt)
    return _final(moe_s, x1, mod, g, n_ctx_tiles, skip_tiles)


def kernel(x, c, ctx, c_ctx, ada_w, ada_b, norm_g, na_w_qkv, na_w_out, na_rel_bias, ml_w_in, ml_b_gate,
           ml_norm_w, ml_w_out, moe_w_router, moe_w_gate, moe_w_up, moe_w_down):
    batch, seq, d = x.shape
    n_ctx = ctx.shape[1]
    depth = ada_w.shape[0]
    rows = seq // GRID_W
    n_exp = moe_w_router.shape[-1]
    ml_heads = ml_norm_w.shape[-1] // ML_V_DIM
    assert n_ctx % TM == 0 and seq % TM == 0 and n_ctx % ML_CHUNK == 0 and seq % ML_CHUNK == 0
    assert rows >= K_ROWS and d % (2 * HEAD_DIM) == 0 and n_exp <= LANE and batch < 16
    n_ct = n_ctx // TM

    xa = jnp.concatenate([ctx, x], axis=1)
    c_all = jnp.zeros((16, d), F32).at[:batch].set(c).at[batch].set(c_ctx)
    mod = _ada(c_all, ada_w, ada_b).reshape(depth, 16, 6, d)
    rope = _rope_tables(n_ctx, seq)

    for l in range(depth):
        last = l == depth - 1
        jx = l // 2
        g = norm_g[l]
        wr = jnp.zeros((d, LANE), F32).at[:, :n_exp].set(moe_w_router[l])
        wr_hi = wr.astype(BF16)
        wr2 = jnp.stack([wr_hi, (wr - wr_hi.astype(F32)).astype(BF16)])
        if l % 2 == 0:
            scale = HEAD_DIM ** -0.5
            w = jnp.concatenate([na_w_qkv[jx][:, :d] * scale, na_w_qkv[jx][:, d:]], axis=1).astype(BF16)
            qkv = _proj_na(xa, mod[l], g, w, n_ct)
            o = _na(qkv, _na_bias_table(na_rel_bias[jx], rows), n_ctx, rows)
            x1, h2s, logits = _post((o,), na_w_out[jx].astype(BF16), xa, mod[l], g, wr2, n_ct)
        else:
            qk_w = ml_heads * HEAD_DIM
            v_w = ml_heads * ML_V_DIM
            main_w = 2 * qk_w + 2 * v_w
            w_in = ml_w_in[jx]
            ng = 2 * ml_heads
            pad = jnp.zeros((d, LANE - ng), F32)
            w_gates = [w_in[:, main_w:main_w + ng], pad, w_in[:, main_w + ng:], pad]
            wk = jnp.concatenate([w_in[:, qk_w:2 * qk_w] * (HEAD_DIM ** -0.5)] + w_gates, axis=1).astype(BF16)
            wt = jnp.concatenate([w_in[:, :qk_w], w_in[:, 2 * qk_w:main_w]] + w_gates, axis=1).T.astype(BF16)
            zpad = jnp.zeros((LANE - ng,), F32)
            bg = jnp.concatenate([ml_b_gate[jx][:ng], zpad, ml_b_gate[jx][ng:], zpad])[None, :]
            bgt = jnp.broadcast_to(bg.T, (2 * LANE, TM))
            k, gates, qt, vt, ot, gates_t = _proj_ml(xa, mod[l], g, wk, wt, bg, bgt, rope, n_ct, qk_w, v_w)
            hf = _scan(qt, k, vt, gates, gates_t, ml_heads, n_ctx, reverse=False)
            hb = _scan(qt, k, vt, gates, gates_t, ml_heads, n_ctx, reverse=True)
            nw = jnp.broadcast_to(ml_norm_w[jx][:, None], (v_w, LANE))
            x1, h2s, logits = _post((hf, hb, ot, nw), ml_w_out[jx].T.astype(BF16), xa, mod[l], g,
                                    wr2, n_ct, ml_heads=ml_heads)
        xa = _moe(h2s, logits, x1, mod[l], g, l, moe_w_gate, moe_w_up, moe_w_down, n_ctx, n_ct,
                  n_ct if last else 0)
    return xa
```

```python
import functools

import numpy as np
import jax
import jax.numpy as jnp
from jax import lax
from jax.experimental import pallas as pl
from jax.experimental.pallas import tpu as pltpu

F32 = jnp.float32
BF16 = jnp.bfloat16

LANE = 128
SUBLANE = 8
VMEM_LIMIT = 56 * 1024 * 1024

EPS = 1e-6
NEG = -1e30
LOG2E = 1.4426950408889634
GRID_W = 64
HEAD_DIM = 64
ML_V_DIM = 128
WIN_ROWS = 8
WIN_COLS = 16
ROPE_BASE = 10000.0
EC_CAPACITY = 2

TM = 256
Q_ROWS = TM // GRID_W
K_ROWS = Q_ROWS + WIN_ROWS
ML_CHUNK = 256
RANK_RADIX = 32
RANK_HI = 16
TOKEN_SPLIT = 64


def _cp(sem):
    return pltpu.CompilerParams(dimension_semantics=sem, vmem_limit_bytes=VMEM_LIMIT)


def _dot(a, b):
    return jnp.dot(a, b, preferred_element_type=F32)


def _dot_nt(a, b):
    return lax.dot_general(a, b, (((1,), (1,)), ((), ())), preferred_element_type=F32)


def _split3(x):
    x1 = x.astype(BF16)
    r1 = x - x1.astype(F32)
    x2 = r1.astype(BF16)
    x3 = (r1 - x2.astype(F32)).astype(BF16)
    return x1, x2, x3


def _rms(x, g):
    return x * lax.rsqrt(jnp.mean(x * x, axis=-1, keepdims=True) + EPS) * g


def _silu(x):
    return x * jax.nn.sigmoid(x)


def _ada_kernel(c_ref, w_ref, b_ref, o_ref):
    s = _silu(c_ref[...]).astype(BF16)
    o_ref[0] = _dot(s, w_ref[0].astype(BF16)) + b_ref[0]


def _ada(c_all, ada_w, ada_b):
    depth, d, n = ada_w.shape
    rows = c_all.shape[0]
    tn = 1024 if n % 1024 == 0 else n
    return pl.pallas_call(
        _ada_kernel,
        grid=(depth, n // tn),
        in_specs=[pl.BlockSpec((rows, d), lambda l, j: (0, 0)),
                  pl.BlockSpec((1, d, tn), lambda l, j: (l, 0, j)),
                  pl.BlockSpec((1, 1, tn), lambda l, j: (l, 0, j))],
        out_specs=pl.BlockSpec((1, rows, tn), lambda l, j: (l, 0, j)),
        out_shape=jax.ShapeDtypeStruct((depth, rows, n), F32),
        compiler_params=_cp(("arbitrary", "arbitrary")),
        name="ada_mod",
    )(c_all, ada_w, ada_b.reshape(depth, 1, n))


def _mod_spec(batch, n_ctx_tiles, d):
    return pl.BlockSpec((1, 6, d), lambda b, i: (jnp.where(i < n_ctx_tiles, batch, b), 0, 0))


def _proj_na_kernel(x_ref, mod_ref, g_ref, wk_ref, wt_ref, k_ref, qt_ref, vt_ref):
    m = mod_ref[0]
    h = (_rms(x_ref[0], g_ref[0:1, :]) * (1.0 + m[1:2, :]) + m[0:1, :]).astype(BF16)
    d = h.shape[1]
    k_ref[0] = _dot(h, wk_ref[...]).astype(BF16)
    rt = _dot_nt(wt_ref[...], h)
    qt_ref[0, 0] = rt[:d].astype(BF16)
    vt_ref[0, 0] = rt[d:].astype(BF16)


def _proj_na(xa, mod, g, wk, wt, n_ctx_tiles):
    batch, tt, d = xa.shape
    n_tiles = tt // TM
    const = lambda a: pl.BlockSpec(a.shape, lambda b, i: (0,) * a.ndim)
    feat = pl.BlockSpec((1, 1, d, TM), lambda b, i: (b, i, 0, 0))
    return pl.pallas_call(
        _proj_na_kernel,
        grid=(batch, n_tiles),
        in_specs=[pl.BlockSpec((1, TM, d), lambda b, i: (b, i, 0)),
                  _mod_spec(batch, n_ctx_tiles, d), const(g), const(wk), const(wt)],
        out_specs=[pl.BlockSpec((1, TM, d), lambda b, i: (b, i, 0)), feat, feat],
        out_shape=[jax.ShapeDtypeStruct((batch, tt, d), BF16),
                   jax.ShapeDtypeStruct((batch, n_tiles, d, TM), BF16),
                   jax.ShapeDtypeStruct((batch, n_tiles, d, TM), BF16)],
        compiler_params=_cp(("arbitrary", "arbitrary")),
        name="proj_na",
    )(xa, mod, g, wk, wt)


def _log_sigmoid(f):
    return jnp.minimum(f, 0.0) - jnp.log1p(jnp.exp(-jnp.abs(f)))


def _proj_ml_kernel(x_ref, mod_ref, g_ref, wk_ref, wt_ref, bg_ref, bgt_ref, cos_ref, sin_ref, cost_ref, sint_ref,
                    k_ref, gate_ref, qt_ref, vt_ref, ot_ref, gatet_ref, *, qk_w, v_w):
    m = mod_ref[0]
    h = (_rms(x_ref[0], g_ref[0:1, :]) * (1.0 + m[1:2, :]) + m[0:1, :]).astype(BF16)
    r = _dot(h, wk_ref[...])
    k = r[:, :qk_w]
    reps = qk_w // LANE
    lane = lax.broadcasted_iota(jnp.int32, k.shape, 1)
    partner = jnp.where(lane % 32 < 16, pltpu.roll(k, qk_w - 16, 1), pltpu.roll(k, 16, 1))
    k = k * jnp.concatenate([cos_ref[...]] * reps, axis=1) + partner * jnp.concatenate([sin_ref[...]] * reps, axis=1)
    k_ref[0] = k.astype(BF16)
    gr = r[:, qk_w:] + bg_ref[...]
    gate_ref[0, :, :LANE] = gr[:, :LANE]
    gate_ref[0, :, LANE:] = _log_sigmoid(gr[:, LANE:])

    rt = _dot_nt(wt_ref[...], h)
    q = rt[:qk_w]
    reps = qk_w // HEAD_DIM
    row = lax.broadcasted_iota(jnp.int32, q.shape, 0)
    partner = jnp.where(row % 32 < 16, pltpu.roll(q, qk_w - 16, 0), pltpu.roll(q, 16, 0))
    q = q * jnp.concatenate([cost_ref[...]] * reps, axis=0) + partner * jnp.concatenate([sint_ref[...]] * reps, axis=0)
    qt_ref[0] = q.astype(BF16)
    vt_ref[0] = rt[qk_w:qk_w + v_w].astype(BF16)
    ot_ref[0] = rt[qk_w + v_w:qk_w + 2 * v_w].astype(BF16)
    gt = rt[qk_w + 2 * v_w:] + bgt_ref[...]
    gatet_ref[0, :LANE, :] = gt[:LANE]
    gatet_ref[0, LANE:, :] = _log_sigmoid(gt[LANE:])


def _proj_ml(xa, mod, g, wk, wt, bg, bgt, tables, n_ctx_tiles, qk_w, v_w):
    batch, tt, d = xa.shape
    cos_t, sin_t, cos_tt, sin_tt = tables
    kern = functools.partial(_proj_ml_kernel, qk_w=qk_w, v_w=v_w)
    const = lambda a: pl.BlockSpec(a.shape, lambda b, i: (0,) * a.ndim)
    tok = lambda w: pl.BlockSpec((1, TM, w), lambda b, i: (b, i, 0))
    feat = lambda w: pl.BlockSpec((1, w, TM), lambda b, i: (b, 0, i))
    return pl.pallas_call(
        kern,
        grid=(batch, tt // TM),
        in_specs=[tok(d), _mod_spec(batch, n_ctx_tiles, d), const(g), const(wk), const(wt), const(bg), const(bgt),
                  pl.BlockSpec((TM, LANE), lambda b, i: (i, 0)),
                  pl.BlockSpec((TM, LANE), lambda b, i: (i, 0)),
                  pl.BlockSpec((HEAD_DIM, TM), lambda b, i: (0, i)),
                  pl.BlockSpec((HEAD_DIM, TM), lambda b, i: (0, i))],
        out_specs=[tok(qk_w), tok(2 * LANE), feat(qk_w), feat(v_w), feat(v_w), feat(2 * LANE)],
        out_shape=[jax.ShapeDtypeStruct((batch, tt, qk_w), BF16),
                   jax.ShapeDtypeStruct((batch, tt, 2 * LANE), F32),
                   jax.ShapeDtypeStruct((batch, qk_w, tt), BF16),
                   jax.ShapeDtypeStruct((batch, v_w, tt), BF16),
                   jax.ShapeDtypeStruct((batch, v_w, tt), BF16),
                   jax.ShapeDtypeStruct((batch, 2 * LANE, tt), F32)],
        compiler_params=_cp(("arbitrary", "arbitrary")),
        name="proj_ml",
    )(xa, mod, g, wk, wt, bg, bgt, cos_t, sin_t, cos_tt, sin_tt)


def _na_kernel(qt_ref, k_ref, vt_ref, tab_ref, sel_ref, o_ref, *, ctx, rows):
    step = pl.program_id(2)
    n_ct = ctx // TM
    win_tiles = K_ROWS * GRID_W // TM
    ones_rows = jnp.where(lax.broadcasted_iota(jnp.int32, (16, TM), 0) == 0, 1.0, 0.0).astype(BF16)

    def attend(chunks):
        m, acc = None, None
        for s, v_tile in chunks:
            m_new = s.max(axis=0, keepdims=True)
            if m is not None:
                m_new = jnp.maximum(m, m_new)
            p = jnp.exp2(s - m_new).astype(BF16)
            part = _dot(jnp.concatenate([v_tile, ones_rows], axis=0), p)
            acc = part if acc is None else acc * jnp.exp2(m - m_new) + part
            m = m_new
        return acc[:HEAD_DIM] / acc[HEAD_DIM:HEAD_DIM + 1]

    def ctx_chunks(hh, q_t):
        sl = slice(hh * HEAD_DIM, (hh + 1) * HEAD_DIM)
        return [(_dot(k_ref[0, t * TM:(t + 1) * TM, sl], q_t), vt_ref[0, t, sl, :]) for t in range(n_ct)]

    @pl.when(step == 0)
    def _():
        for hh in range(2):
            sl = slice(hh * HEAD_DIM, (hh + 1) * HEAD_DIM)
            o_ref[0, 0, sl, :] = attend(ctx_chunks(hh, qt_ref[0, 0, sl, :])).astype(BF16)

    @pl.when(step > 0)
    def _():
        rb = step - 1
        start = jnp.clip(Q_ROWS * rb - WIN_ROWS // 2, 0, rows - K_ROWS)
        t0 = n_ct + start // Q_ROWS
        ks = pl.multiple_of(t0 * TM, TM)
        kr = start + lax.broadcasted_iota(jnp.int32, (16, TM), 0)
        r = Q_ROWS * rb + lax.broadcasted_iota(jnp.int32, (16, TM), 1) // GRID_W
        r0 = jnp.clip(r - WIN_ROWS // 2, 0, rows - WIN_ROWS)
        pen = jnp.where((kr >= r0) & (kr < r0 + WIN_ROWS), 0.0, NEG).astype(BF16)
        pen = jnp.concatenate([pen, jnp.zeros((HEAD_DIM - 16, TM), BF16)], axis=0)
        k_both = k_ref[0, pl.ds(ks, win_tiles * TM), :]
        first_half = lax.broadcasted_iota(jnp.int32, k_both.shape, 1) < HEAD_DIM

        for hh in range(2):
            sl = slice(hh * HEAD_DIM, (hh + 1) * HEAD_DIM)
            q_t = qt_ref[0, 0, sl, :]
            if hh == 0:
                k_aug = jnp.where(first_half, k_both, sel_ref[...])
                q_aug = jnp.concatenate([q_t, pen], axis=0)
            else:
                k_aug = jnp.where(first_half, sel_ref[...], k_both)
                q_aug = jnp.concatenate([pen, q_t], axis=0)
            chunks = []
            for i in range(win_tiles):
                blocks = []
                for krl in range(i * Q_ROWS, (i + 1) * Q_ROWS):
                    tiles = []
                    for u in range(Q_ROWS // 2):
                        dr_e = start + krl - (Q_ROWS * rb + 2 * u) + WIN_ROWS - 1
                        tiles.append(tab_ref[hh, jnp.clip(dr_e, -1, 2 * WIN_ROWS - 1) + 1])
                    blocks.append(jnp.concatenate(tiles, axis=1))
                s_i = _dot(k_aug[i * TM:(i + 1) * TM], q_aug) + jnp.concatenate(blocks, axis=0)
                chunks.append((s_i, vt_ref[0, t0 + i, sl, :]))
            o_ref[0, 0, sl, :] = attend(ctx_chunks(hh, q_t) + chunks).astype(BF16)


def _na_bias_table(rel_bias):
    n_dr, n_dc = 2 * WIN_ROWS - 1, 2 * WIN_COLS - 1
    col = np.arange(GRID_W)
    c0 = np.clip(col - WIN_COLS // 2, 0, GRID_W - WIN_COLS)
    col_ok = (col[:, None] >= c0[None, :]) & (col[:, None] < c0[None, :] + WIN_COLS)
    dc = np.clip(col[:, None] - col[None, :] + WIN_COLS - 1, 0, n_dc - 1)
    oh = ((np.arange(n_dc)[:, None, None] == dc[None]) & col_ok[None]).astype(np.float32)
    cb = jnp.einsum('hrc,ckq->hrkq', rel_bias.astype(F32) * LOG2E, oh, precision=lax.Precision.HIGHEST)
    cb = jnp.where(col_ok[None, None], cb, NEG)
    neg = jnp.full((cb.shape[0], 2, GRID_W, GRID_W), NEG, F32)
    ext = jnp.concatenate([neg, cb, neg[:, :1]], axis=1)
    n_tiles = 2 * WIN_ROWS + 1
    return jnp.concatenate([ext[:, 1:1 + n_tiles], ext[:, 0:n_tiles]], axis=-1)


def _na(k, qt, vt, bias_tab, ctx, rows):
    batch, n_tiles, d, _ = qt.shape
    tt = k.shape[1]
    hp = d // LANE
    n_rb = rows // Q_ROWS
    n_ct = ctx // TM
    assert n_ct == 1
    q_tile = lambda h, b, s: (b, jnp.where(s == 0, 0, s - 1 + n_ct), h, 0)
    key_row = np.arange(K_ROWS * GRID_W)[:, None] // GRID_W
    key_row_sel = jnp.asarray(key_row == (np.arange(LANE)[None, :] % HEAD_DIM), BF16)
    kern = functools.partial(_na_kernel, ctx=ctx, rows=rows)
    return pl.pallas_call(
        kern,
        grid=(hp, batch, n_rb + 1),
        in_specs=[pl.BlockSpec((1, 1, LANE, TM), q_tile),
                  pl.BlockSpec((1, tt, LANE), lambda h, b, s: (b, 0, h)),
                  pl.BlockSpec((1, n_tiles, LANE, TM), lambda h, b, s: (b, 0, h, 0)),
                  pl.BlockSpec((2,) + bias_tab.shape[1:], lambda h, b, s: (h, 0, 0, 0)),
                  pl.BlockSpec(key_row_sel.shape, lambda h, b, s: (0, 0))],
        out_specs=pl.BlockSpec((1, 1, LANE, TM), q_tile),
        out_shape=jax.ShapeDtypeStruct((batch, n_tiles, d, TM), BF16),
        compiler_params=_cp(("arbitrary", "arbitrary", "arbitrary")),
        name="na_attention",
    )(qt, k, vt, bias_tab, key_row_sel)


def _scan_kernel(qt_ref, k_ref, vt_ref, gate_ref, gatet_ref, o_ref, c_ref, m_ref, *, heads, reverse):
    L = ML_CHUNK
    j = pl.program_id(1)

    @pl.when(j == 0)
    def _():
        c_ref[...] = jnp.zeros_like(c_ref)
        m_ref[...] = jnp.full_like(m_ref, NEG)

    r_i = lax.broadcasted_iota(jnp.int32, (L, L), 0)
    c_i = lax.broadcasted_iota(jnp.int32, (L, L), 1)
    seen = (r_i >= c_i) if reverse else (r_i <= c_i)
    tri_row = jnp.where(seen, 1.0, 0.0).astype(BF16)
    tri_col = jnp.where((r_i <= c_i) if reverse else (r_i >= c_i), 1.0, 0.0).astype(BF16)

    f1, f2, f3 = _split3(gate_ref[0, :, LANE:])
    b_col = _dot(tri_col, f1) + _dot(tri_col, f2) + _dot(tri_col, f3)
    u_col = gate_ref[0, :, :LANE] - b_col
    f1, f2, f3 = _split3(gatet_ref[0, LANE:, :])
    b_row = _dot(f1, tri_row) + _dot(f2, tri_row) + _dot(f3, tri_row)
    i_row = gatet_ref[0, :LANE, :]
    last = 0 if reverse else L - 1
    ones_rows = jnp.where(lax.broadcasted_iota(jnp.int32, (ML_V_DIM, L), 0) == 0, 1.0, 0.0).astype(BF16)

    for h in range(heads):
        c = (heads if reverse else 0) + h
        br = b_row[c:c + 1, :]
        ir = i_row[c:c + 1, :]
        mh = m_ref[h]
        mh_row = jnp.concatenate([mh] * (L // LANE), axis=1)
        a_t = jnp.where(seen, jnp.broadcast_to(u_col[:, c:c + 1], (L, L)), NEG)
        mu = jnp.maximum(a_t.max(axis=0, keepdims=True), mh_row)
        p_t = jnp.exp(a_t - mu)
        w_inter = jnp.exp(mh_row - mu)
        kh = k_ref[0, :, h * HEAD_DIM:(h + 1) * HEAD_DIM]
        state = c_ref[h]
        q_t = qt_ref[0, h * HEAD_DIM:(h + 1) * HEAD_DIM, :]
        r1 = _dot(jnp.concatenate([kh, state.astype(BF16)], axis=0), q_t)
        sp = (r1[:L] * p_t).astype(BF16)
        v_aug = jnp.concatenate([vt_ref[0, h * ML_V_DIM:(h + 1) * ML_V_DIM, :], ones_rows], axis=0)
        rt = _dot(v_aug, sp) + w_inter * r1[L:]
        den = rt[ML_V_DIM:ML_V_DIM + 1]
        o_ref[0, h * ML_V_DIM:(h + 1) * ML_V_DIM, :] = rt[:ML_V_DIM] / jnp.maximum(jnp.abs(den), jnp.exp(-(br + mu)))
        be = br[:, last:last + 1]
        m1 = mh[:, 0:1]
        g_row = be - br + ir
        m_new = jnp.maximum(be + m1, g_row.max(axis=1, keepdims=True))
        w_k = jnp.exp(g_row - m_new)
        decay = jnp.exp(be + m1 - m_new)
        c_ref[h] = decay * state + _dot((v_aug.astype(F32) * w_k).astype(BF16), kh)
        m_ref[h] = jnp.broadcast_to(m_new, (1, LANE))


def _scan(qt, k, vt, gates, gates_t, heads, ctx, reverse):
    batch, tt, qk_w = k.shape
    L = ML_CHUNK
    nc = tt // L
    ncc = ctx // L
    v_w = heads * ML_V_DIM

    def chunk(j):
        if not reverse:
            return j
        return jnp.where(j < ncc, ncc - 1 - j, nc - 1 - (j - ncc))

    kern = functools.partial(_scan_kernel, heads=heads, reverse=reverse)
    feat = lambda w: pl.BlockSpec((1, w, L), lambda b, j: (b, 0, chunk(j)))
    tok = lambda w: pl.BlockSpec((1, L, w), lambda b, j: (b, chunk(j), 0))
    return pl.pallas_call(
        kern,
        grid=(batch, nc),
        in_specs=[feat(qk_w), tok(qk_w), feat(v_w), tok(2 * LANE), feat(2 * LANE)],
        out_specs=feat(v_w),
        out_shape=jax.ShapeDtypeStruct((batch, v_w, tt), F32),
        scratch_shapes=[pltpu.VMEM((heads, 2 * ML_V_DIM, HEAD_DIM), F32),
                        pltpu.VMEM((heads, 1, LANE), F32)],
        compiler_params=_cp(("arbitrary", "arbitrary")),
        name="mlstm_scan_bwd" if reverse else "mlstm_scan_fwd",
    )(qt, k, vt, gates, gates_t)


def _post_common(y, xa_ref, mod_ref, g_ref, wr_ref, x1_ref, h2s_ref, lg_ref):
    m = mod_ref[0]
    x1 = xa_ref[0] + m[2:3, :] * _rms(y, g_ref[1:2, :])
    x1_ref[0] = x1
    h2 = _rms(x1, g_ref[2:3, :]) * (1.0 + m[4:5, :]) + m[3:4, :]
    d = h2.shape[1]
    p = d // LANE
    for jj in range(p):
        h2s_ref[0, pl.ds(jj, TM, stride=p), :] = h2[:, jj * LANE:(jj + 1) * LANE]
    h_hi = h2.astype(BF16)
    h_lo = (h2 - h_hi.astype(F32)).astype(BF16)
    lg_ref[0] = _dot(h_hi, wr_ref[0]) + _dot(h_hi, wr_ref[1]) + _dot(h_lo, wr_ref[0])


def _post_na_kernel(ot_ref, wt_ref, xa_ref, mod_ref, g_ref, wr_ref, x1_ref, h2s_ref, lg_ref):
    y = _dot(wt_ref[...], ot_ref[0, 0]).T
    _post_common(y, xa_ref, mod_ref, g_ref, wr_ref, x1_ref, h2s_ref, lg_ref)


def _post_ml_kernel(hf_ref, hb_ref, og_ref, nw_ref, wt_ref, xa_ref, mod_ref, g_ref, wr_ref,
                    x1_ref, h2s_ref, lg_ref, *, heads):
    hs = hf_ref[0] + hb_ref[0]
    parts = []
    for h in range(heads):
        t = hs[h * ML_V_DIM:(h + 1) * ML_V_DIM, :]
        parts.append(t * lax.rsqrt(jnp.mean(t * t, axis=0, keepdims=True) + EPS))
    nw = jnp.concatenate([nw_ref[...]] * (TM // LANE), axis=1)
    hn = jnp.concatenate(parts, axis=0) * nw
    y_in = (hn * jax.nn.sigmoid(og_ref[0].astype(F32))).astype(BF16)
    y = _dot(wt_ref[...], y_in).T
    _post_common(y, xa_ref, mod_ref, g_ref, wr_ref, x1_ref, h2s_ref, lg_ref)


def _post(mixer_inputs, w_out, xa, mod, g, wr, n_ctx_tiles, ml_heads=None):
    batch, tt, d = xa.shape
    p = d // LANE
    tile = lambda w: pl.BlockSpec((1, TM, w), lambda b, i: (b, i, 0))
    full2 = lambda a: pl.BlockSpec(a.shape, lambda b, i: (0,) * a.ndim)
    if ml_heads is None:
        (o,) = mixer_inputs
        kern = _post_na_kernel
        head_specs, head_args = [pl.BlockSpec((1, 1, d, TM), lambda b, i: (b, i, 0, 0))], [o]
    else:
        hf, hb, og, nw = mixer_inputs
        v_w = ml_heads * ML_V_DIM
        kern = functools.partial(_post_ml_kernel, heads=ml_heads)
        feat = pl.BlockSpec((1, v_w, TM), lambda b, i: (b, 0, i))
        head_specs = [feat, feat, feat, full2(nw)]
        head_args = [hf, hb, og, nw]
    return pl.pallas_call(
        kern,
        grid=(batch, tt // TM),
        in_specs=head_specs + [full2(w_out), tile(d), _mod_spec(batch, n_ctx_tiles, d), full2(g), full2(wr)],
        out_specs=[tile(d),
                   pl.BlockSpec((1, TM * p, LANE), lambda b, i: (b, i, 0)),
                   tile(LANE)],
        out_shape=[jax.ShapeDtypeStruct((batch, tt, d), F32),
                   jax.ShapeDtypeStruct((batch, tt * p, LANE), F32),
                   jax.ShapeDtypeStruct((batch, tt, LANE), F32)],
        compiler_params=_cp(("arbitrary", "arbitrary")),
        name="post_mixer",
    )(*head_args, w_out, xa, mod, g, wr)


def _route_kernel(lg_ref, idx_ref, pos_ref, *, n_exp, cap):
    t = lg_ref.shape[1]
    n_tiles = t // LANE
    lt = lg_ref[0].T[:n_exp, :]
    e = jnp.exp(lt - lt.max(axis=0, keepdims=True))
    aff = e / e.sum(axis=0, keepdims=True)
    def count_ge(v):
        return jnp.where(aff >= v, 1.0, 0.0).sum(axis=1, keepdims=True)

    def bisect(i, cur):
        cand = cur | jnp.left_shift(jnp.int32(1), 30 - i)
        return jnp.where(count_ge(pltpu.bitcast(cand, F32)) >= cap, cand, cur)

    v_bits = lax.fori_loop(0, 31, bisect, jnp.zeros((n_exp, 1), jnp.int32))

    def refine(_, hi):
        pivot = jnp.where(aff < hi, aff, -1.0).max(axis=1, keepdims=True)
        return jnp.where(count_ge(pivot) >= cap, hi, pivot)

    min_normal_bits = 0x00800000
    hi = lax.fori_loop(0, 3, refine, pltpu.bitcast(jnp.maximum(v_bits + 1, min_normal_bits), F32))
    thr = jnp.where(aff < hi, aff, -1.0).max(axis=1, keepdims=True)
    gt = aff > thr
    eq = aff == thr
    need = cap - jnp.where(gt, 1.0, 0.0).sum(axis=1, keepdims=True)

    upper = jnp.where(lax.broadcasted_iota(jnp.int32, (LANE, LANE), 0)
                      <= lax.broadcasted_iota(jnp.int32, (LANE, LANE), 1), 1.0, 0.0).astype(BF16)

    def excl_cumsum(mask_f32):
        carry = jnp.zeros((n_exp, 1), F32)
        parts = []
        for jt in range(n_tiles):
            tile = mask_f32[:, jt * LANE:(jt + 1) * LANE]
            inc = _dot(tile.astype(BF16), upper)
            parts.append(inc - tile + carry)
            carry = carry + inc[:, LANE - 1:LANE]
        return jnp.concatenate(parts, axis=1)

    eq_f = jnp.where(eq, 1.0, 0.0)
    sel = gt | (eq & (excl_cumsum(eq_f) < need))
    sel_f = jnp.where(sel, 1.0, 0.0)
    pos_ref[...] = jnp.where(sel, excl_cumsum(sel_f), -1.0)

    hi_iota = lax.broadcasted_iota(jnp.int32, (RANK_HI, t), 0).astype(F32)
    lo_iota = lax.broadcasted_iota(jnp.int32, (RANK_RADIX, t), 0).astype(F32)
    t_idx = lax.broadcasted_iota(jnp.int32, (1, t), 1)
    t_hi = (t_idx // TOKEN_SPLIT).astype(F32)
    t_lo = (t_idx % TOKEN_SPLIT).astype(F32)
    for ex in range(n_exp):
        pos = pos_ref[ex:ex + 1, :]
        hi = jnp.floor(pos * (1.0 / RANK_RADIX))
        lo = pos - hi * RANK_RADIX
        hit = hi_iota == hi
        lhs = jnp.concatenate([jnp.where(hit, t_hi, 0.0), jnp.where(hit, t_lo, 0.0)], axis=0).astype(BF16)
        rhs = jnp.where(lo_iota == lo, 1.0, 0.0).astype(BF16)
        r = _dot_nt(lhs, rhs)
        idx_ref[0, ex] = (r[:RANK_HI] * TOKEN_SPLIT + r[RANK_HI:]).astype(jnp.int32)


def _route(logits, n_exp, cap):
    batch, t, _ = logits.shape
    assert cap % RANK_RADIX == 0 and cap <= RANK_HI * RANK_RADIX and t <= LANE * TOKEN_SPLIT
    kern = functools.partial(_route_kernel, n_exp=n_exp, cap=cap)
    idx = pl.pallas_call(
        kern,
        grid=(batch,),
        in_specs=[pl.BlockSpec((1, t, LANE), lambda b: (b, 0, 0))],
        out_specs=pl.BlockSpec((1, n_exp, RANK_HI, RANK_RADIX), lambda b: (b, 0, 0, 0)),
        out_shape=jax.ShapeDtypeStruct((batch, n_exp, RANK_HI, RANK_RADIX), jnp.int32),
        scratch_shapes=[pltpu.VMEM((n_exp, t), F32)],
        compiler_params=_cp(("arbitrary",)),
        name="route",
    )(logits)
    return idx.reshape(batch, n_exp, RANK_HI * RANK_RADIX)[:, :, :cap]


def _gather_kernel(idx_ref, src_ref, lg_ref, xe_ref, gate_ref, tile_ref, gl_ref, *, n_exp, slots, p, stride):
    unroll = 8

    def body(c, _):
        for u in range(unroll):
            r = c * unroll + u
            t = idx_ref[0, 0, 0, r]
            tile_ref[pl.ds(r, p, stride=stride), :] = src_ref[0, pl.ds(pl.multiple_of(t * p, p), p), :]
            gl_ref[pl.ds(r, 1), :] = lg_ref[0, pl.ds(t, 1), :]
        return 0

    lax.fori_loop(0, slots // unroll, body, 0)
    xe_ref[0, 0] = jnp.concatenate([tile_ref[pl.ds(jj * stride, slots), :] for jj in range(p)], axis=1).astype(BF16)
    lg = gl_ref[...]
    lane = lax.broadcasted_iota(jnp.int32, lg.shape, 1)
    lg = jnp.where(lane < n_exp, lg, NEG)
    ex = jnp.exp(lg - lg.max(axis=1, keepdims=True))
    gate_ref[0, 0] = ex / ex.sum(axis=1, keepdims=True)


def _gather(idx, h2s, logits, n_exp):
    batch, _, _, slots = idx.shape
    tt = logits.shape[1]
    p = h2s.shape[1] // tt
    d = p * LANE
    stride = slots + SUBLANE
    kern = functools.partial(_gather_kernel, n_exp=n_exp, slots=slots, p=p, stride=stride)
    return pl.pallas_call(
        kern,
        grid=(batch, n_exp),
        in_specs=[pl.BlockSpec((1, 1, 1, slots), lambda b, e: (b, e, 0, 0), memory_space=pltpu.SMEM),
                  pl.BlockSpec((1, tt * p, LANE), lambda b, e: (b, 0, 0), pipeline_mode=pl.Buffered(1)),
                  pl.BlockSpec((1, tt, LANE), lambda b, e: (b, 0, 0))],
        out_specs=[pl.BlockSpec((1, 1, slots, d), lambda b, e: (e, b, 0, 0)),
                   pl.BlockSpec((1, 1, slots, LANE), lambda b, e: (e, b, 0, 0))],
        out_shape=[jax.ShapeDtypeStruct((n_exp, batch, slots, d), BF16),
                   jax.ShapeDtypeStruct((n_exp, batch, slots, LANE), F32)],
        scratch_shapes=[pltpu.VMEM((p * stride, LANE), F32), pltpu.VMEM((slots, LANE), F32)],
        compiler_params=_cp(("arbitrary", "arbitrary")),
        name="moe_gather",
    )(idx, h2s, logits)


def _ffn_kernel(xe_ref, gate_ref, wg_hbm, wu_hbm, wd_hbm, ye_ref, wg_st, wu_st, wd_st, wgb_ref, wub_ref, wdb_ref,
                sem, *, layer, n_exp, f_chunk, cast_rows):
    e, b = pl.program_id(0), pl.program_id(1)
    stage = ((wg_hbm, wg_st, wgb_ref), (wu_hbm, wu_st, wub_ref), (wd_hbm, wd_st, wdb_ref))

    def weight_copies(ex):
        return [pltpu.make_async_copy(hbm.at[layer, ex], st, sem.at[i]) for i, (hbm, st, _) in enumerate(stage)]

    @pl.when((e == 0) & (b == 0))
    def _():
        for cp in weight_copies(0):
            cp.start()

    @pl.when(b == 0)
    def _():
        for cp in weight_copies(e):
            cp.wait()
        for _, st, dst in stage:
            def cast(r, _, st=st, dst=dst):
                rows = pl.ds(pl.multiple_of(r * cast_rows, cast_rows), cast_rows)
                dst[rows, :] = st[rows, :].astype(BF16)
                return 0
            lax.fori_loop(0, st.shape[0] // cast_rows, cast, 0)

        @pl.when(e + 1 < n_exp)
        def _():
            for cp in weight_copies(e + 1):
                cp.start()

    x = xe_ref[0, 0]
    slots, d = x.shape
    f = wgb_ref.shape[1]
    y = jnp.zeros((slots, d), F32)
    for c in range(f // f_chunk):
        cs = slice(c * f_chunk, (c + 1) * f_chunk)
        a = _dot(x, wgb_ref[:, cs])
        u = _dot(x, wub_ref[:, cs])
        y = y + _dot((_silu(a) * u).astype(BF16), wdb_ref[cs, :])
    gt = gate_ref[0, 0]
    lane = lax.broadcasted_iota(jnp.int32, gt.shape, 1)
    y = y * jnp.where(lane == e, gt, 0.0).sum(axis=1, keepdims=True)
    p = d // LANE
    for jj in range(p):
        ye_ref[0, 0, pl.ds(jj, slots, stride=p), :] = y[:, jj * LANE:(jj + 1) * LANE]


def _ffn(xe, gates, wg, wu, wd, layer):
    n_exp, batch, slots, d = xe.shape
    f = wg.shape[3]
    p = d // LANE
    kern = functools.partial(_ffn_kernel, layer=layer, n_exp=n_exp, f_chunk=min(f, 512), cast_rows=min(d, 256))
    hbm = pl.BlockSpec(memory_space=pl.ANY)
    return pl.pallas_call(
        kern,
        grid=(n_exp, batch),
        in_specs=[pl.BlockSpec((1, 1, slots, d), lambda e, b: (e, b, 0, 0)),
                  pl.BlockSpec((1, 1, slots, LANE), lambda e, b: (e, b, 0, 0)),
                  hbm, hbm, hbm],
        out_specs=pl.BlockSpec((1, 1, slots * p, LANE), lambda e, b: (e, b, 0, 0)),
        out_shape=jax.ShapeDtypeStruct((n_exp, batch, slots * p, LANE), F32),
        scratch_shapes=[pltpu.VMEM((d, f), F32), pltpu.VMEM((d, f), F32), pltpu.VMEM((f, d), F32),
                        pltpu.VMEM((d, f), BF16), pltpu.VMEM((d, f), BF16), pltpu.VMEM((f, d), BF16),
                        pltpu.SemaphoreType.DMA((3,))],
        compiler_params=_cp(("arbitrary", "arbitrary")),
        name="moe_ffn",
    )(xe, gates, wg, wu, wd)


def _combine_kernel(idx_ref, ye_ref, acc_ref, *, slots, p):
    @pl.when(pl.program_id(1) == 0)
    def _():
        acc_ref[...] = jnp.zeros_like(acc_ref)

    unroll = 4

    def body(c, _):
        rows, vals = [], []
        for u in range(unroll):
            r = c * unroll + u
            row = pl.multiple_of(idx_ref[0, 0, 0, r] * p, p)
            rows.append(row)
            vals.append(acc_ref[0, pl.ds(row, p), :] + ye_ref[0, 0, pl.ds(pl.multiple_of(r * p, p), p), :])
        for row, val in zip(rows, vals):
            acc_ref[0, pl.ds(row, p), :] = val
        return 0

    lax.fori_loop(0, slots // unroll, body, 0)


def _combine(idx, ye, tt):
    n_exp, batch, sp, _ = ye.shape
    slots = idx.shape[3]
    p = sp // slots
    kern = functools.partial(_combine_kernel, slots=slots, p=p)
    return pl.pallas_call(
        kern,
        grid=(batch, n_exp),
        in_specs=[pl.BlockSpec((1, 1, 1, slots), lambda b, e: (b, e, 0, 0), memory_space=pltpu.SMEM),
                  pl.BlockSpec((1, 1, sp, LANE), lambda b, e: (e, b, 0, 0))],
        out_specs=pl.BlockSpec((1, tt * p, LANE), lambda b, e: (b, 0, 0)),
        out_shape=jax.ShapeDtypeStruct((batch, tt * p, LANE), F32),
        compiler_params=_cp(("arbitrary", "arbitrary")),
        name="moe_combine",
    )(idx, ye)


def _final_kernel(moe_ref, x1_ref, mod_ref, g_ref, o_ref):
    p = x1_ref.shape[2] // LANE
    moe = jnp.concatenate([moe_ref[0, pl.ds(jj, TM, stride=p), :] for jj in range(p)], axis=1)
    o_ref[0] = x1_ref[0] + mod_ref[0][5:6, :] * _rms(moe, g_ref[3:4, :])


def _final(moe_s, x1, mod, g, n_ctx_tiles, skip_tiles):
    batch, tt, d = x1.shape
    p = d // LANE
    n_tiles = tt // TM - skip_tiles
    return pl.pallas_call(
        _final_kernel,
        grid=(batch, n_tiles),
        in_specs=[pl.BlockSpec((1, TM * p, LANE), lambda b, i: (b, i + skip_tiles, 0)),
                  pl.BlockSpec((1, TM, d), lambda b, i: (b, i + skip_tiles, 0)),
                  pl.BlockSpec((1, 6, d), lambda b, i: (jnp.where(i + skip_tiles < n_ctx_tiles, batch, b), 0, 0)),
                  pl.BlockSpec((4, d), lambda b, i: (0, 0))],
        out_specs=pl.BlockSpec((1, TM, d), lambda b, i: (b, i, 0)),
        out_shape=jax.ShapeDtypeStruct((batch, n_tiles * TM, d), F32),
        compiler_params=_cp(("arbitrary", "arbitrary")),
        name="post_ffn",
    )(moe_s, x1, mod, g)


def _rope_tables(ctx, seq):
    half = HEAD_DIM // 2
    inv = ROPE_BASE ** (-np.arange(0, half, 2, dtype=np.float32) / half)
    t = np.arange(seq)
    ang_r = (t // GRID_W).astype(np.float32)[:, None] * inv[None, :]
    ang_c = (t % GRID_W).astype(np.float32)[:, None] * inv[None, :]
    ang = jnp.asarray(np.concatenate([ang_r, ang_r, ang_c, ang_c], axis=1))
    sign = np.tile(np.concatenate([-np.ones(16), np.ones(16)]), 2).astype(np.float32)
    cos = jnp.concatenate([jnp.ones((ctx, HEAD_DIM), F32), jnp.cos(ang)], axis=0)
    sin = jnp.concatenate([jnp.zeros((ctx, HEAD_DIM), F32), jnp.sin(ang) * sign[None, :]], axis=0)
    return jnp.tile(cos, (1, LANE // HEAD_DIM)), jnp.tile(sin, (1, LANE // HEAD_DIM)), cos.T, sin.T


def _moe(h2s, logits, x1, mod, g, layer, w_gate, w_up, w_down, ctx, n_ctx_tiles, skip_tiles):
    batch, tt, d = x1.shape
    n_exp = w_gate.shape[1]
    seq = tt - ctx
    idx_ctx = _route(logits[:, :ctx], n_exp, EC_CAPACITY * ctx // n_exp)
    idx_lat = _route(logits[:, ctx:], n_exp, EC_CAPACITY * seq // n_exp)
    idx = jnp.concatenate([idx_ctx, idx_lat + ctx], axis=2)[:, :, None, :]
    xe, gates = _gather(idx, h2s, logits, n_exp)
    ye = _ffn(xe, gates, w_gate, w_up, w_down, layer)
    moe_s = _combine(idx, ye, tt)
    return _final(moe_s, x1, mod, g, n_ctx_tiles, skip_tiles)


def kernel(x, c, ctx, c_ctx, ada_w, ada_b, norm_g, na_w_qkv, na_w_out, na_rel_bias, ml_w_in, ml_b_gate,
           ml_norm_w, ml_w_out, moe_w_router, moe_w_gate, moe_w_up, moe_w_down):
    batch, seq, d = x.shape
    n_ctx = ctx.shape[1]
    depth = ada_w.shape[0]
    rows = seq // GRID_W
    n_exp = moe_w_router.shape[-1]
    ml_heads = ml_norm_w.shape[-1] // ML_V_DIM
    assert n_ctx % TM == 0 and seq % TM == 0 and n_ctx % ML_CHUNK == 0 and seq % ML_CHUNK == 0
    assert rows >= K_ROWS and d % (2 * HEAD_DIM) == 0 and n_exp <= LANE and batch < 16
    n_ct = n_ctx // TM

    xa = jnp.concatenate([ctx, x], axis=1)
    c_all = jnp.zeros((16, d), F32).at[:batch].set(c).at[batch].set(c_ctx)
    mod = _ada(c_all, ada_w, ada_b).reshape(depth, 16, 6, d)
    rope = _rope_tables(n_ctx, seq)

    for l in range(depth):
        last = l == depth - 1
        jx = l // 2
        g = norm_g[l]
        wr = jnp.zeros((d, LANE), F32).at[:, :n_exp].set(moe_w_router[l])
        wr_hi = wr.astype(BF16)
        wr2 = jnp.stack([wr_hi, (wr - wr_hi.astype(F32)).astype(BF16)])
        if l % 2 == 0:
            scale = HEAD_DIM ** -0.5
            w_qkv = na_w_qkv[jx]
            wk = w_qkv[:, d:2 * d].astype(BF16)
            wt = jnp.concatenate([w_qkv[:, :d] * (scale * LOG2E), w_qkv[:, 2 * d:]], axis=1).T.astype(BF16)
            k, qt, vt = _proj_na(xa, mod[l], g, wk, wt, n_ct)
            o = _na(k, qt, vt, _na_bias_table(na_rel_bias[jx]), n_ctx, rows)
            x1, h2s, logits = _post((o,), na_w_out[jx].T.astype(BF16), xa, mod[l], g, wr2, n_ct)
        else:
            qk_w = ml_heads * HEAD_DIM
            v_w = ml_heads * ML_V_DIM
            main_w = 2 * qk_w + 2 * v_w
            w_in = ml_w_in[jx]
            ng = 2 * ml_heads
            pad = jnp.zeros((d, LANE - ng), F32)
            w_gates = [w_in[:, main_w:main_w + ng], pad, w_in[:, main_w + ng:], pad]
            wk = jnp.concatenate([w_in[:, qk_w:2 * qk_w] * (HEAD_DIM ** -0.5)] + w_gates, axis=1).astype(BF16)
            wt = jnp.concatenate([w_in[:, :qk_w], w_in[:, 2 * qk_w:main_w]] + w_gates, axis=1).T.astype(BF16)
            zpad = jnp.zeros((LANE - ng,), F32)
            bg = jnp.concatenate([ml_b_gate[jx][:ng], zpad, ml_b_gate[jx][ng:], zpad])[None, :]
            bgt = jnp.broadcast_to(bg.T, (2 * LANE, TM))
            k, gates, qt, vt, ot, gates_t = _proj_ml(xa, mod[l], g, wk, wt, bg, bgt, rope, n_ct, qk_w, v_w)
            hf = _scan(qt, k, vt, gates, gates_t, ml_heads, n_ctx, reverse=False)
            hb = _scan(qt, k, vt, gates, gates_t, ml_heads, n_ctx, reverse=True)
            nw = jnp.broadcast_to(ml_norm_w[jx][:, None], (v_w, LANE))
            x1, h2s, logits = _post((hf, hb, ot, nw), ml_w_out[jx].T.astype(BF16), xa, mod[l], g,
                                    wr2, n_ct, ml_heads=ml_heads)
        xa = _moe(h2s, logits, x1, mod[l], g, l, moe_w_gate, moe_w_up, moe_w_down, n_ctx, n_ct,
                  n_ct if last else 0)
    return xa
```

```python
import functools

import numpy as np
import jax
import jax.numpy as jnp
from jax import lax
from jax.experimental import pallas as pl
from jax.experimental.pallas import tpu as pltpu

F32 = jnp.float32
BF16 = jnp.bfloat16

LANE = 128
SUBLANE = 8
VMEM_LIMIT = 56 * 1024 * 1024

EPS = 1e-6
NEG = -1e30
LOG2E = 1.4426950408889634
GRID_W = 64
HEAD_DIM = 64
ML_V_DIM = 128
WIN_ROWS = 8
WIN_COLS = 16
ROPE_BASE = 10000.0
EC_CAPACITY = 2

TM = 256
Q_ROWS = TM // GRID_W
K_ROWS = Q_ROWS + WIN_ROWS
ML_CHUNK = 256
NA_PAIRS = 2
RANK_RADIX = 32
RANK_HI = 16
TOKEN_SPLIT = 64


def _cp(sem):
    return pltpu.CompilerParams(dimension_semantics=sem, vmem_limit_bytes=VMEM_LIMIT)


def _dot(a, b):
    return jnp.dot(a, b, preferred_element_type=F32)


def _dot_nt(a, b):
    return lax.dot_general(a, b, (((1,), (1,)), ((), ())), preferred_element_type=F32)


def _dot_tn(a, b):
    return lax.dot_general(a, b, (((0,), (0,)), ((), ())), preferred_element_type=F32)


def _split3(x):
    x1 = x.astype(BF16)
    r1 = x - x1.astype(F32)
    x2 = r1.astype(BF16)
    x3 = (r1 - x2.astype(F32)).astype(BF16)
    return x1, x2, x3


def _rms(x, g):
    return x * lax.rsqrt(jnp.mean(x * x, axis=-1, keepdims=True) + EPS) * g


def _silu(x):
    return x * jax.nn.sigmoid(x)


def _ada_kernel(c_ref, w_ref, b_ref, o_ref):
    s = _silu(c_ref[...]).astype(BF16)
    o_ref[0] = _dot(s, w_ref[0].astype(BF16)) + b_ref[0]


def _ada(c_all, ada_w, ada_b):
    depth, d, n = ada_w.shape
    rows = c_all.shape[0]
    tn = 1024 if n % 1024 == 0 else n
    return pl.pallas_call(
        _ada_kernel,
        grid=(depth, n // tn),
        in_specs=[pl.BlockSpec((rows, d), lambda l, j: (0, 0)),
                  pl.BlockSpec((1, d, tn), lambda l, j: (l, 0, j)),
                  pl.BlockSpec((1, 1, tn), lambda l, j: (l, 0, j))],
        out_specs=pl.BlockSpec((1, rows, tn), lambda l, j: (l, 0, j)),
        out_shape=jax.ShapeDtypeStruct((depth, rows, n), F32),
        compiler_params=_cp(("arbitrary", "arbitrary")),
        name="ada_mod",
    )(c_all, ada_w, ada_b.reshape(depth, 1, n))


def _mod_spec(batch, n_ctx_tiles, d):
    return pl.BlockSpec((1, 6, d), lambda b, i: (jnp.where(i < n_ctx_tiles, batch, b), 0, 0))


def _proj_na_kernel(x_ref, mod_ref, g_ref, wk_ref, wt_ref, k_ref, qt_ref, vt_ref):
    m = mod_ref[0]
    h = (_rms(x_ref[0], g_ref[0:1, :]) * (1.0 + m[1:2, :]) + m[0:1, :]).astype(BF16)
    d = h.shape[1]
    k_ref[0] = _dot(h, wk_ref[...]).astype(BF16)
    rt = _dot_nt(wt_ref[...], h)
    qt_ref[0, 0] = rt[:d].astype(BF16)
    vt_ref[0, 0] = rt[d:].astype(BF16)


def _proj_na(xa, mod, g, wk, wt, n_ctx_tiles):
    batch, tt, d = xa.shape
    n_tiles = tt // TM
    const = lambda a: pl.BlockSpec(a.shape, lambda b, i: (0,) * a.ndim)
    feat = pl.BlockSpec((1, 1, d, TM), lambda b, i: (b, i, 0, 0))
    return pl.pallas_call(
        _proj_na_kernel,
        grid=(batch, n_tiles),
        in_specs=[pl.BlockSpec((1, TM, d), lambda b, i: (b, i, 0)),
                  _mod_spec(batch, n_ctx_tiles, d), const(g), const(wk), const(wt)],
        out_specs=[pl.BlockSpec((1, TM, d), lambda b, i: (b, i, 0)), feat, feat],
        out_shape=[jax.ShapeDtypeStruct((batch, tt, d), BF16),
                   jax.ShapeDtypeStruct((batch, n_tiles, d, TM), BF16),
                   jax.ShapeDtypeStruct((batch, n_tiles, d, TM), BF16)],
        compiler_params=_cp(("arbitrary", "arbitrary")),
        name="proj_na",
    )(xa, mod, g, wk, wt)


def _log_sigmoid(f):
    return jnp.minimum(f, 0.0) - jnp.log1p(jnp.exp(-jnp.abs(f)))


def _proj_ml_kernel(x_ref, mod_ref, g_ref, wk_ref, wt_ref, bg_ref, bgt_ref, cos_ref, sin_ref, cost_ref, sint_ref,
                    k_ref, gate_ref, qt_ref, vt_ref, ot_ref, gatet_ref, *, qk_w, v_w):
    m = mod_ref[0]
    h = (_rms(x_ref[0], g_ref[0:1, :]) * (1.0 + m[1:2, :]) + m[0:1, :]).astype(BF16)
    r = _dot(h, wk_ref[...])
    k = r[:, :qk_w]
    reps = qk_w // LANE
    lane = lax.broadcasted_iota(jnp.int32, k.shape, 1)
    partner = jnp.where(lane % 32 < 16, pltpu.roll(k, qk_w - 16, 1), pltpu.roll(k, 16, 1))
    k = k * jnp.concatenate([cos_ref[...]] * reps, axis=1) + partner * jnp.concatenate([sin_ref[...]] * reps, axis=1)
    k_ref[0] = k.astype(BF16)
    gr = r[:, qk_w:] + bg_ref[...]
    gate_ref[0, :, :LANE] = gr[:, :LANE]
    gate_ref[0, :, LANE:] = _log_sigmoid(gr[:, LANE:])

    rt = _dot_nt(wt_ref[...], h)
    q = rt[:qk_w]
    reps = qk_w // HEAD_DIM
    row = lax.broadcasted_iota(jnp.int32, q.shape, 0)
    partner = jnp.where(row % 32 < 16, pltpu.roll(q, qk_w - 16, 0), pltpu.roll(q, 16, 0))
    q = q * jnp.concatenate([cost_ref[...]] * reps, axis=0) + partner * jnp.concatenate([sint_ref[...]] * reps, axis=0)
    qt_ref[0] = q.astype(BF16)
    vt_ref[0] = rt[qk_w:qk_w + v_w].astype(BF16)
    ot_ref[0] = rt[qk_w + v_w:qk_w + 2 * v_w].astype(BF16)
    gt = rt[qk_w + 2 * v_w:] + bgt_ref[...]
    gatet_ref[0, :LANE, :] = gt[:LANE]
    gatet_ref[0, LANE:, :] = _log_sigmoid(gt[LANE:])


def _proj_ml(xa, mod, g, wk, wt, bg, bgt, tables, n_ctx_tiles, qk_w, v_w):
    batch, tt, d = xa.shape
    cos_t, sin_t, cos_tt, sin_tt = tables
    kern = functools.partial(_proj_ml_kernel, qk_w=qk_w, v_w=v_w)
    const = lambda a: pl.BlockSpec(a.shape, lambda b, i: (0,) * a.ndim)
    tok = lambda w: pl.BlockSpec((1, TM, w), lambda b, i: (b, i, 0))
    feat = lambda w: pl.BlockSpec((1, w, TM), lambda b, i: (b, 0, i))
    return pl.pallas_call(
        kern,
        grid=(batch, tt // TM),
        in_specs=[tok(d), _mod_spec(batch, n_ctx_tiles, d), const(g), const(wk), const(wt), const(bg), const(bgt),
                  pl.BlockSpec((TM, LANE), lambda b, i: (i, 0)),
                  pl.BlockSpec((TM, LANE), lambda b, i: (i, 0)),
                  pl.BlockSpec((HEAD_DIM, TM), lambda b, i: (0, i)),
                  pl.BlockSpec((HEAD_DIM, TM), lambda b, i: (0, i))],
        out_specs=[tok(qk_w), tok(2 * LANE), feat(qk_w), feat(v_w), feat(v_w), feat(2 * LANE)],
        out_shape=[jax.ShapeDtypeStruct((batch, tt, qk_w), BF16),
                   jax.ShapeDtypeStruct((batch, tt, 2 * LANE), F32),
                   jax.ShapeDtypeStruct((batch, qk_w, tt), BF16),
                   jax.ShapeDtypeStruct((batch, v_w, tt), BF16),
                   jax.ShapeDtypeStruct((batch, v_w, tt), BF16),
                   jax.ShapeDtypeStruct((batch, 2 * LANE, tt), F32)],
        compiler_params=_cp(("arbitrary", "arbitrary")),
        name="proj_ml",
    )(xa, mod, g, wk, wt, bg, bgt, cos_t, sin_t, cos_tt, sin_tt)


def _na_kernel(qt_ref, k_ref, vt_ref, tab_ref, sel_ref, o_ref, *, ctx, rows):
    step = pl.program_id(2)
    n_ct = ctx // TM
    win_tiles = K_ROWS * GRID_W // TM
    ones_rows = jnp.where(lax.broadcasted_iota(jnp.int32, (16, TM), 0) == 0, 1.0, 0.0).astype(BF16)

    def attend(chunks):
        m, acc = None, None
        for s, v_tile in chunks:
            m_new = s.max(axis=0, keepdims=True)
            if m is not None:
                m_new = jnp.maximum(m, m_new)
            p = jnp.exp2(s - m_new).astype(BF16)
            part = _dot(jnp.concatenate([v_tile, ones_rows], axis=0), p)
            acc = part if acc is None else acc * jnp.exp2(m - m_new) + part
            m = m_new
        return acc[:HEAD_DIM] / acc[HEAD_DIM:HEAD_DIM + 1]

    def ctx_chunks(hh, q_t):
        sl = slice(hh * HEAD_DIM, (hh + 1) * HEAD_DIM)
        return [(_dot(k_ref[0, t * TM:(t + 1) * TM, sl], q_t), vt_ref[0, t, sl, :]) for t in range(n_ct)]

    @pl.when(step == 0)
    def _():
        for hd in range(2 * NA_PAIRS):
            sl = slice(hd * HEAD_DIM, (hd + 1) * HEAD_DIM)
            o_ref[0, 0, sl, :] = attend(ctx_chunks(hd, qt_ref[0, 0, sl, :])).astype(BF16)

    @pl.when(step > 0)
    def _():
        rb = step - 1
        start = jnp.clip(Q_ROWS * rb - WIN_ROWS // 2, 0, rows - K_ROWS)
        t0 = n_ct + start // Q_ROWS
        ks = pl.multiple_of(t0 * TM, TM)
        kr = start + lax.broadcasted_iota(jnp.int32, (16, TM), 0)
        r = Q_ROWS * rb + lax.broadcasted_iota(jnp.int32, (16, TM), 1) // GRID_W
        r0 = jnp.clip(r - WIN_ROWS // 2, 0, rows - WIN_ROWS)
        pen = jnp.where((kr >= r0) & (kr < r0 + WIN_ROWS), 0.0, NEG).astype(BF16)
        pen = jnp.concatenate([pen, jnp.zeros((HEAD_DIM - 16, TM), BF16)], axis=0)
        first_half = lax.broadcasted_iota(jnp.int32, (win_tiles * TM, LANE), 1) < HEAD_DIM

        for hd in range(2 * NA_PAIRS):
            hh = hd % 2
            sl = slice(hd * HEAD_DIM, (hd + 1) * HEAD_DIM)
            q_t = qt_ref[0, 0, sl, :]
            k_both = k_ref[0, pl.ds(ks, win_tiles * TM), (hd // 2) * LANE:(hd // 2 + 1) * LANE]
            if hh == 0:
                k_aug = jnp.where(first_half, k_both, sel_ref[...])
                q_aug = jnp.concatenate([q_t, pen], axis=0)
            else:
                k_aug = jnp.where(first_half, sel_ref[...], k_both)
                q_aug = jnp.concatenate([pen, q_t], axis=0)
            chunks = []
            for i in range(win_tiles):
                blocks = []
                for krl in range(i * Q_ROWS, (i + 1) * Q_ROWS):
                    tiles = []
                    for u in range(Q_ROWS // 2):
                        dr_e = start + krl - (Q_ROWS * rb + 2 * u) + WIN_ROWS - 1
                        tiles.append(tab_ref[hd, jnp.clip(dr_e, -1, 2 * WIN_ROWS - 1) + 1])
                    blocks.append(jnp.concatenate(tiles, axis=1))
                s_i = _dot(k_aug[i * TM:(i + 1) * TM], q_aug) + jnp.concatenate(blocks, axis=0)
                chunks.append((s_i, vt_ref[0, t0 + i, sl, :]))
            o_ref[0, 0, sl, :] = attend(ctx_chunks(hd, q_t) + chunks).astype(BF16)


def _na_bias_table(rel_bias):
    n_dr, n_dc = 2 * WIN_ROWS - 1, 2 * WIN_COLS - 1
    col = np.arange(GRID_W)
    c0 = np.clip(col - WIN_COLS // 2, 0, GRID_W - WIN_COLS)
    col_ok = (col[:, None] >= c0[None, :]) & (col[:, None] < c0[None, :] + WIN_COLS)
    dc = np.clip(col[:, None] - col[None, :] + WIN_COLS - 1, 0, n_dc - 1)
    oh = ((np.arange(n_dc)[:, None, None] == dc[None]) & col_ok[None]).astype(np.float32)
    cb = jnp.einsum('hrc,ckq->hrkq', rel_bias.astype(F32) * LOG2E, oh, precision=lax.Precision.HIGHEST)
    cb = jnp.where(col_ok[None, None], cb, NEG)
    neg = jnp.full((cb.shape[0], 2, GRID_W, GRID_W), NEG, F32)
    ext = jnp.concatenate([neg, cb, neg[:, :1]], axis=1)
    n_tiles = 2 * WIN_ROWS + 1
    return jnp.concatenate([ext[:, 1:1 + n_tiles], ext[:, 0:n_tiles]], axis=-1)


def _na(k, qt, vt, bias_tab, ctx, rows):
    batch, n_tiles, d, _ = qt.shape
    tt = k.shape[1]
    gw = NA_PAIRS * LANE
    hp = d // gw
    n_rb = rows // Q_ROWS
    n_ct = ctx // TM
    assert n_ct == 1
    q_tile = lambda h, b, s: (b, jnp.where(s == 0, 0, s - 1 + n_ct), h, 0)
    key_row = np.arange(K_ROWS * GRID_W)[:, None] // GRID_W
    key_row_sel = jnp.asarray(key_row == (np.arange(LANE)[None, :] % HEAD_DIM), BF16)
    kern = functools.partial(_na_kernel, ctx=ctx, rows=rows)
    return pl.pallas_call(
        kern,
        grid=(hp, batch, n_rb + 1),
        in_specs=[pl.BlockSpec((1, 1, gw, TM), q_tile),
                  pl.BlockSpec((1, tt, gw), lambda h, b, s: (b, 0, h)),
                  pl.BlockSpec((1, n_tiles, gw, TM), lambda h, b, s: (b, 0, h, 0)),
                  pl.BlockSpec((2 * NA_PAIRS,) + bias_tab.shape[1:], lambda h, b, s: (h, 0, 0, 0)),
                  pl.BlockSpec(key_row_sel.shape, lambda h, b, s: (0, 0))],
        out_specs=pl.BlockSpec((1, 1, gw, TM), q_tile),
        out_shape=jax.ShapeDtypeStruct((batch, n_tiles, d, TM), BF16),
        compiler_params=_cp(("arbitrary", "arbitrary", "arbitrary")),
        name="na_attention",
    )(qt, k, vt, bias_tab, key_row_sel)


def _scan_kernel(qt_ref, k_ref, vt_ref, gate_ref, gatet_ref, o_ref, c_ref, m_ref, *, heads, reverse):
    L = ML_CHUNK
    j = pl.program_id(1)

    @pl.when(j == 0)
    def _():
        c_ref[...] = jnp.zeros_like(c_ref)
        m_ref[...] = jnp.full_like(m_ref, NEG)

    r_i = lax.broadcasted_iota(jnp.int32, (L, L), 0)
    c_i = lax.broadcasted_iota(jnp.int32, (L, L), 1)
    seen = (r_i >= c_i) if reverse else (r_i <= c_i)
    tri_row = jnp.where(seen, 1.0, 0.0).astype(BF16)
    tri_col = jnp.where((r_i <= c_i) if reverse else (r_i >= c_i), 1.0, 0.0).astype(BF16)

    f1, f2, f3 = _split3(gate_ref[0, :, LANE:])
    b_col = _dot(tri_col, f1) + _dot(tri_col, f2) + _dot(tri_col, f3)
    u_col = gate_ref[0, :, :LANE] - b_col
    f1, f2, f3 = _split3(gatet_ref[0, LANE:, :])
    b_row = _dot(f1, tri_row) + _dot(f2, tri_row) + _dot(f3, tri_row)
    i_row = gatet_ref[0, :LANE, :]
    last = 0 if reverse else L - 1
    ones_rows = jnp.where(lax.broadcasted_iota(jnp.int32, (ML_V_DIM, L), 0) == 0, 1.0, 0.0).astype(BF16)

    for h in range(heads):
        c = (heads if reverse else 0) + h
        br = b_row[c:c + 1, :]
        ir = i_row[c:c + 1, :]
        mh = m_ref[h]
        mh_row = jnp.concatenate([mh] * (L // LANE), axis=1)
        a_t = jnp.where(seen, jnp.broadcast_to(u_col[:, c:c + 1], (L, L)), NEG)
        mu = jnp.maximum(a_t.max(axis=0, keepdims=True), mh_row)
        p_t = jnp.exp(a_t - mu)
        w_inter = jnp.exp(mh_row - mu)
        kh = k_ref[0, :, h * HEAD_DIM:(h + 1) * HEAD_DIM]
        state = c_ref[h]
        q_t = qt_ref[0, h * HEAD_DIM:(h + 1) * HEAD_DIM, :]
        r1 = _dot(jnp.concatenate([kh, state.astype(BF16)], axis=0), q_t)
        sp = (r1[:L] * p_t).astype(BF16)
        v_aug = jnp.concatenate([vt_ref[0, h * ML_V_DIM:(h + 1) * ML_V_DIM, :], ones_rows], axis=0)
        rt = _dot(v_aug, sp) + w_inter * r1[L:]
        den = rt[ML_V_DIM:ML_V_DIM + 1]
        o_ref[0, h * ML_V_DIM:(h + 1) * ML_V_DIM, :] = rt[:ML_V_DIM] / jnp.maximum(jnp.abs(den), jnp.exp(-(br + mu)))
        be = br[:, last:last + 1]
        m1 = mh[:, 0:1]
        g_row = be - br + ir
        m_new = jnp.maximum(be + m1, g_row.max(axis=1, keepdims=True))
        w_k = jnp.exp(g_row - m_new)
        decay = jnp.exp(be + m1 - m_new)
        c_ref[h] = decay * state + _dot((v_aug.astype(F32) * w_k).astype(BF16), kh)
        m_ref[h] = jnp.broadcast_to(m_new, (1, LANE))


def _scan(qt, k, vt, gates, gates_t, heads, ctx, reverse):
    batch, tt, qk_w = k.shape
    L = ML_CHUNK
    nc = tt // L
    ncc = ctx // L
    v_w = heads * ML_V_DIM

    def chunk(j):
        if not reverse:
            return j
        return jnp.where(j < ncc, ncc - 1 - j, nc - 1 - (j - ncc))

    kern = functools.partial(_scan_kernel, heads=heads, reverse=reverse)
    feat = lambda w: pl.BlockSpec((1, w, L), lambda b, j: (b, 0, chunk(j)))
    tok = lambda w: pl.BlockSpec((1, L, w), lambda b, j: (b, chunk(j), 0))
    return pl.pallas_call(
        kern,
        grid=(batch, nc),
        in_specs=[feat(qk_w), tok(qk_w), feat(v_w), tok(2 * LANE), feat(2 * LANE)],
        out_specs=feat(v_w),
        out_shape=jax.ShapeDtypeStruct((batch, v_w, tt), F32),
        scratch_shapes=[pltpu.VMEM((heads, 2 * ML_V_DIM, HEAD_DIM), F32),
                        pltpu.VMEM((heads, 1, LANE), F32)],
        compiler_params=_cp(("arbitrary", "arbitrary")),
        name="mlstm_scan_bwd" if reverse else "mlstm_scan_fwd",
    )(qt, k, vt, gates, gates_t)


def _post_common(y, xa_ref, mod_ref, g_ref, wr_ref, x1_ref, h2s_ref, lg_ref):
    m = mod_ref[0]
    x1 = xa_ref[0] + m[2:3, :] * _rms(y, g_ref[1:2, :])
    x1_ref[0] = x1
    h2 = _rms(x1, g_ref[2:3, :]) * (1.0 + m[4:5, :]) + m[3:4, :]
    d = h2.shape[1]
    p = d // LANE
    for jj in range(p):
        h2s_ref[0, pl.ds(jj, TM, stride=p), :] = h2[:, jj * LANE:(jj + 1) * LANE]
    h_hi = h2.astype(BF16)
    h_lo = (h2 - h_hi.astype(F32)).astype(BF16)
    lg_ref[0] = _dot(h_hi, wr_ref[0]) + _dot(h_hi, wr_ref[1]) + _dot(h_lo, wr_ref[0])


def _post_na_kernel(ot_ref, w_ref, xa_ref, mod_ref, g_ref, wr_ref, x1_ref, h2s_ref, lg_ref):
    y = _dot_tn(ot_ref[0, 0], w_ref[...])
    _post_common(y, xa_ref, mod_ref, g_ref, wr_ref, x1_ref, h2s_ref, lg_ref)


def _post_ml_kernel(hf_ref, hb_ref, og_ref, nw_ref, w_ref, xa_ref, mod_ref, g_ref, wr_ref,
                    x1_ref, h2s_ref, lg_ref, *, heads):
    hs = hf_ref[0] + hb_ref[0]
    parts = []
    for h in range(heads):
        t = hs[h * ML_V_DIM:(h + 1) * ML_V_DIM, :]
        parts.append(t * lax.rsqrt(jnp.mean(t * t, axis=0, keepdims=True) + EPS))
    nw = jnp.concatenate([nw_ref[...]] * (TM // LANE), axis=1)
    hn = jnp.concatenate(parts, axis=0) * nw
    y_in = (hn * jax.nn.sigmoid(og_ref[0].astype(F32))).astype(BF16)
    y = _dot_tn(y_in, w_ref[...])
    _post_common(y, xa_ref, mod_ref, g_ref, wr_ref, x1_ref, h2s_ref, lg_ref)


def _post(mixer_inputs, w_out, xa, mod, g, wr, n_ctx_tiles, ml_heads=None):
    batch, tt, d = xa.shape
    p = d // LANE
    tile = lambda w: pl.BlockSpec((1, TM, w), lambda b, i: (b, i, 0))
    full2 = lambda a: pl.BlockSpec(a.shape, lambda b, i: (0,) * a.ndim)
    if ml_heads is None:
        (o,) = mixer_inputs
        kern = _post_na_kernel
        head_specs, head_args = [pl.BlockSpec((1, 1, d, TM), lambda b, i: (b, i, 0, 0))], [o]
    else:
        hf, hb, og, nw = mixer_inputs
        v_w = ml_heads * ML_V_DIM
        kern = functools.partial(_post_ml_kernel, heads=ml_heads)
        feat = pl.BlockSpec((1, v_w, TM), lambda b, i: (b, 0, i))
        head_specs = [feat, feat, feat, full2(nw)]
        head_args = [hf, hb, og, nw]
    return pl.pallas_call(
        kern,
        grid=(batch, tt // TM),
        in_specs=head_specs + [full2(w_out), tile(d), _mod_spec(batch, n_ctx_tiles, d), full2(g), full2(wr)],
        out_specs=[tile(d),
                   pl.BlockSpec((1, TM * p, LANE), lambda b, i: (b, i, 0)),
                   tile(LANE)],
        out_shape=[jax.ShapeDtypeStruct((batch, tt, d), F32),
                   jax.ShapeDtypeStruct((batch, tt * p, LANE), F32),
                   jax.ShapeDtypeStruct((batch, tt, LANE), F32)],
        compiler_params=_cp(("arbitrary", "arbitrary")),
        name="post_mixer",
    )(*head_args, w_out, xa, mod, g, wr)


def _route_kernel(lg_ref, idx_ref, gate_ref, pos_ref, *, n_exp, cap):
    t = lg_ref.shape[1]
    n_tiles = t // LANE
    lt = lg_ref[0].T[:n_exp, :]
    e = jnp.exp(lt - lt.max(axis=0, keepdims=True))
    aff = e / e.sum(axis=0, keepdims=True)
    def count_ge(v):
        return jnp.where(aff >= v, 1.0, 0.0).sum(axis=1, keepdims=True)

    def bisect(i, cur):
        cand = cur | jnp.left_shift(jnp.int32(1), 30 - i)
        return jnp.where(count_ge(pltpu.bitcast(cand, F32)) >= cap, cand, cur)

    v_bits = lax.fori_loop(0, 31, bisect, jnp.zeros((n_exp, 1), jnp.int32))

    def refine(_, hi):
        pivot = jnp.where(aff < hi, aff, -1.0).max(axis=1, keepdims=True)
        return jnp.where(count_ge(pivot) >= cap, hi, pivot)

    min_normal_bits = 0x00800000
    hi = lax.fori_loop(0, 3, refine, pltpu.bitcast(jnp.maximum(v_bits + 1, min_normal_bits), F32))
    thr = jnp.where(aff < hi, aff, -1.0).max(axis=1, keepdims=True)
    gt = aff > thr
    eq = aff == thr
    need = cap - jnp.where(gt, 1.0, 0.0).sum(axis=1, keepdims=True)

    upper = jnp.where(lax.broadcasted_iota(jnp.int32, (LANE, LANE), 0)
                      <= lax.broadcasted_iota(jnp.int32, (LANE, LANE), 1), 1.0, 0.0).astype(BF16)

    def excl_cumsum(mask_f32):
        carry = jnp.zeros((n_exp, 1), F32)
        parts = []
        for jt in range(n_tiles):
            tile = mask_f32[:, jt * LANE:(jt + 1) * LANE]
            inc = _dot(tile.astype(BF16), upper)
            parts.append(inc - tile + carry)
            carry = carry + inc[:, LANE - 1:LANE]
        return jnp.concatenate(parts, axis=1)

    eq_f = jnp.where(eq, 1.0, 0.0)
    sel = gt | (eq & (excl_cumsum(eq_f) < need))
    sel_f = jnp.where(sel, 1.0, 0.0)
    pos_ref[...] = jnp.where(sel, excl_cumsum(sel_f), -1.0)

    hi_iota = lax.broadcasted_iota(jnp.int32, (RANK_HI, t), 0).astype(F32)
    lo_iota = lax.broadcasted_iota(jnp.int32, (RANK_RADIX, t), 0).astype(F32)
    t_idx = lax.broadcasted_iota(jnp.int32, (1, t), 1)
    t_hi = (t_idx // TOKEN_SPLIT).astype(F32)
    t_lo = (t_idx % TOKEN_SPLIT).astype(F32)
    for ex in range(n_exp):
        pos = pos_ref[ex:ex + 1, :]
        hi = jnp.floor(pos * (1.0 / RANK_RADIX))
        lo = pos - hi * RANK_RADIX
        hit = hi_iota == hi
        parts = [t_hi, t_lo] + [a.astype(F32) for a in _split3(aff[ex:ex + 1, :])]
        lhs = jnp.concatenate([jnp.where(hit, v, 0.0) for v in parts], axis=0).astype(BF16)
        rhs = jnp.where(lo_iota == lo, 1.0, 0.0).astype(BF16)
        r = _dot_nt(lhs, rhs)
        idx_ref[0, ex] = (r[:RANK_HI] * TOKEN_SPLIT + r[RANK_HI:2 * RANK_HI]).astype(jnp.int32)
        gate_ref[0, ex] = r[2 * RANK_HI:3 * RANK_HI] + r[3 * RANK_HI:4 * RANK_HI] + r[4 * RANK_HI:]


def _route(logits, n_exp, cap):
    batch, t, _ = logits.shape
    assert cap % RANK_RADIX == 0 and cap <= RANK_HI * RANK_RADIX and t <= LANE * TOKEN_SPLIT
    kern = functools.partial(_route_kernel, n_exp=n_exp, cap=cap)
    out_spec = pl.BlockSpec((1, n_exp, RANK_HI, RANK_RADIX), lambda b: (b, 0, 0, 0))
    idx, gate = pl.pallas_call(
        kern,
        grid=(batch,),
        in_specs=[pl.BlockSpec((1, t, LANE), lambda b: (b, 0, 0))],
        out_specs=[out_spec, out_spec],
        out_shape=[jax.ShapeDtypeStruct((batch, n_exp, RANK_HI, RANK_RADIX), jnp.int32),
                   jax.ShapeDtypeStruct((batch, n_exp, RANK_HI, RANK_RADIX), F32)],
        scratch_shapes=[pltpu.VMEM((n_exp, t), F32)],
        compiler_params=_cp(("arbitrary",)),
        name="route",
    )(logits)
    flat = lambda a: a.reshape(batch, n_exp, RANK_HI * RANK_RADIX)[:, :, :cap]
    return flat(idx), flat(gate)


def _gather_kernel(idx_ref, src_ref, xe_ref, tile_ref, *, slots, p, stride):
    unroll = 8

    def body(c, _):
        for u in range(unroll):
            r = c * unroll + u
            t = idx_ref[0, 0, 0, r]
            tile_ref[pl.ds(r, p, stride=stride), :] = src_ref[0, pl.ds(pl.multiple_of(t * p, p), p), :]
        return 0

    lax.fori_loop(0, slots // unroll, body, 0)
    xe_ref[0, 0] = jnp.concatenate([tile_ref[pl.ds(jj * stride, slots), :] for jj in range(p)], axis=1).astype(BF16)


def _gather(idx, h2s, tt, n_exp):
    batch, _, _, slots = idx.shape
    p = h2s.shape[1] // tt
    d = p * LANE
    stride = slots + SUBLANE
    kern = functools.partial(_gather_kernel, slots=slots, p=p, stride=stride)
    return pl.pallas_call(
        kern,
        grid=(batch, n_exp),
        in_specs=[pl.BlockSpec((1, 1, 1, slots), lambda b, e: (b, e, 0, 0), memory_space=pltpu.SMEM),
                  pl.BlockSpec((1, tt * p, LANE), lambda b, e: (b, 0, 0), pipeline_mode=pl.Buffered(1))],
        out_specs=pl.BlockSpec((1, 1, slots, d), lambda b, e: (e, b, 0, 0)),
        out_shape=jax.ShapeDtypeStruct((n_exp, batch, slots, d), BF16),
        scratch_shapes=[pltpu.VMEM((p * stride, LANE), F32)],
        compiler_params=_cp(("arbitrary", "arbitrary")),
        name="moe_gather",
    )(idx, h2s)


def _ffn_kernel(xe_ref, gate_ref, wg_hbm, wu_hbm, wd_hbm, ye_ref, wg_st, wu_st, wd_st, wgb_ref, wub_ref, wdb_ref,
                sem, *, layer, n_exp, f_chunk, cast_rows):
    e, b = pl.program_id(0), pl.program_id(1)
    stage = ((wg_hbm, wg_st, wgb_ref), (wu_hbm, wu_st, wub_ref), (wd_hbm, wd_st, wdb_ref))

    def weight_copies(ex):
        return [pltpu.make_async_copy(hbm.at[layer, ex], st, sem.at[i]) for i, (hbm, st, _) in enumerate(stage)]

    @pl.when((e == 0) & (b == 0))
    def _():
        for cp in weight_copies(0):
            cp.start()

    @pl.when(b == 0)
    def _():
        for cp in weight_copies(e):
            cp.wait()
        for _, st, dst in stage:
            def cast(r, _, st=st, dst=dst):
                rows = pl.ds(pl.multiple_of(r * cast_rows, cast_rows), cast_rows)
                dst[rows, :] = st[rows, :].astype(BF16)
                return 0
            lax.fori_loop(0, st.shape[0] // cast_rows, cast, 0)

        @pl.when(e + 1 < n_exp)
        def _():
            for cp in weight_copies(e + 1):
                cp.start()

    x = xe_ref[0, 0]
    slots, d = x.shape
    f = wgb_ref.shape[1]
    y = jnp.zeros((slots, d), F32)
    for c in range(f // f_chunk):
        cs = slice(c * f_chunk, (c + 1) * f_chunk)
        a = _dot(x, wgb_ref[:, cs])
        u = _dot(x, wub_ref[:, cs])
        y = y + _dot((_silu(a) * u).astype(BF16), wdb_ref[cs, :])
    y = y * gate_ref[0, 0]
    p = d // LANE
    for jj in range(p):
        ye_ref[0, 0, pl.ds(jj, slots, stride=p), :] = y[:, jj * LANE:(jj + 1) * LANE]


def _ffn(xe, gates, wg, wu, wd, layer):
    n_exp, batch, slots, d = xe.shape
    f = wg.shape[3]
    p = d // LANE
    kern = functools.partial(_ffn_kernel, layer=layer, n_exp=n_exp, f_chunk=min(f, 512), cast_rows=min(d, 256))
    hbm = pl.BlockSpec(memory_space=pl.ANY)
    return pl.pallas_call(
        kern,
        grid=(n_exp, batch),
        in_specs=[pl.BlockSpec((1, 1, slots, d), lambda e, b: (e, b, 0, 0)),
                  pl.BlockSpec((1, 1, slots, 1), lambda e, b: (e, b, 0, 0)),
                  hbm, hbm, hbm],
        out_specs=pl.BlockSpec((1, 1, slots * p, LANE), lambda e, b: (e, b, 0, 0)),
        out_shape=jax.ShapeDtypeStruct((n_exp, batch, slots * p, LANE), F32),
        scratch_shapes=[pltpu.VMEM((d, f), F32), pltpu.VMEM((d, f), F32), pltpu.VMEM((f, d), F32),
                        pltpu.VMEM((d, f), BF16), pltpu.VMEM((d, f), BF16), pltpu.VMEM((f, d), BF16),
                        pltpu.SemaphoreType.DMA((3,))],
        compiler_params=_cp(("arbitrary", "arbitrary")),
        name="moe_ffn",
    )(xe, gates, wg, wu, wd)


def _combine_kernel(idx_ref, ye_ref, acc_ref, *, slots, p):
    @pl.when(pl.program_id(1) == 0)
    def _():
        acc_ref[...] = jnp.zeros_like(acc_ref)

    unroll = 4

    def body(c, _):
        rows, vals = [], []
        for u in range(unroll):
            r = c * unroll + u
            row = pl.multiple_of(idx_ref[0, 0, 0, r] * p, p)
            rows.append(row)
            vals.append(acc_ref[0, pl.ds(row, p), :] + ye_ref[0, 0, pl.ds(pl.multiple_of(r * p, p), p), :])
        for row, val in zip(rows, vals):
            acc_ref[0, pl.ds(row, p), :] = val
        return 0

    lax.fori_loop(0, slots // unroll, body, 0)


def _combine(idx, ye, tt):
    n_exp, batch, sp, _ = ye.shape
    slots = idx.shape[3]
    p = sp // slots
    kern = functools.partial(_combine_kernel, slots=slots, p=p)
    return pl.pallas_call(
        kern,
        grid=(batch, n_exp),
        in_specs=[pl.BlockSpec((1, 1, 1, slots), lambda b, e: (b, e, 0, 0), memory_space=pltpu.SMEM),
                  pl.BlockSpec((1, 1, sp, LANE), lambda b, e: (e, b, 0, 0))],
        out_specs=pl.BlockSpec((1, tt * p, LANE), lambda b, e: (b, 0, 0)),
        out_shape=jax.ShapeDtypeStruct((batch, tt * p, LANE), F32),
        compiler_params=_cp(("arbitrary", "arbitrary")),
        name="moe_combine",
    )(idx, ye)


def _final_kernel(moe_ref, x1_ref, mod_ref, g_ref, o_ref):
    p = x1_ref.shape[2] // LANE
    moe = jnp.concatenate([moe_ref[0, pl.ds(jj, TM, stride=p), :] for jj in range(p)], axis=1)
    o_ref[0] = x1_ref[0] + mod_ref[0][5:6, :] * _rms(moe, g_ref[3:4, :])


def _final(moe_s, x1, mod, g, n_ctx_tiles, skip_tiles):
    batch, tt, d = x1.shape
    p = d // LANE
    n_tiles = tt // TM - skip_tiles
    return pl.pallas_call(
        _final_kernel,
        grid=(batch, n_tiles),
        in_specs=[pl.BlockSpec((1, TM * p, LANE), lambda b, i: (b, i + skip_tiles, 0)),
                  pl.BlockSpec((1, TM, d), lambda b, i: (b, i + skip_tiles, 0)),
                  pl.BlockSpec((1, 6, d), lambda b, i: (jnp.where(i + skip_tiles < n_ctx_tiles, batch, b), 0, 0)),
                  pl.BlockSpec((4, d), lambda b, i: (0, 0))],
        out_specs=pl.BlockSpec((1, TM, d), lambda b, i: (b, i, 0)),
        out_shape=jax.ShapeDtypeStruct((batch, n_tiles * TM, d), F32),
        compiler_params=_cp(("arbitrary", "arbitrary")),
        name="post_ffn",
    )(moe_s, x1, mod, g)


def _rope_tables(ctx, seq):
    half = HEAD_DIM // 2
    inv = ROPE_BASE ** (-np.arange(0, half, 2, dtype=np.float32) / half)
    t = np.arange(seq)
    ang_r = (t // GRID_W).astype(np.float32)[:, None] * inv[None, :]
    ang_c = (t % GRID_W).astype(np.float32)[:, None] * inv[None, :]
    ang = jnp.asarray(np.concatenate([ang_r, ang_r, ang_c, ang_c], axis=1))
    sign = np.tile(np.concatenate([-np.ones(16), np.ones(16)]), 2).astype(np.float32)
    cos = jnp.concatenate([jnp.ones((ctx, HEAD_DIM), F32), jnp.cos(ang)], axis=0)
    sin = jnp.concatenate([jnp.zeros((ctx, HEAD_DIM), F32), jnp.sin(ang) * sign[None, :]], axis=0)
    return jnp.tile(cos, (1, LANE // HEAD_DIM)), jnp.tile(sin, (1, LANE // HEAD_DIM)), cos.T, sin.T


def _moe(h2s, logits, x1, mod, g, layer, w_gate, w_up, w_down, ctx, n_ctx_tiles, skip_tiles):
    batch, tt, d = x1.shape
    n_exp = w_gate.shape[1]
    seq = tt - ctx
    idx_ctx, gate_ctx = _route(logits[:, :ctx], n_exp, EC_CAPACITY * ctx // n_exp)
    idx_lat, gate_lat = _route(logits[:, ctx:], n_exp, EC_CAPACITY * seq // n_exp)
    idx = jnp.concatenate([idx_ctx, idx_lat + ctx], axis=2)[:, :, None, :]
    gates = jnp.transpose(jnp.concatenate([gate_ctx, gate_lat], axis=2), (1, 0, 2))[..., None]
    xe = _gather(idx, h2s, tt, n_exp)
    ye = _ffn(xe, gates, w_gate, w_up, w_down, layer)
    moe_s = _combine(idx, ye, tt)
    return _final(moe_s, x1, mod, g, n_ctx_tiles, skip_tiles)


def kernel(x, c, ctx, c_ctx, ada_w, ada_b, norm_g, na_w_qkv, na_w_out, na_rel_bias, ml_w_in, ml_b_gate,
           ml_norm_w, ml_w_out, moe_w_router, moe_w_gate, moe_w_up, moe_w_down):
    batch, seq, d = x.shape
    n_ctx = ctx.shape[1]
    depth = ada_w.shape[0]
    rows = seq // GRID_W
    n_exp = moe_w_router.shape[-1]
    ml_heads = ml_norm_w.shape[-1] // ML_V_DIM
    assert n_ctx % TM == 0 and seq % TM == 0 and n_ctx % ML_CHUNK == 0 and seq % ML_CHUNK == 0
    assert rows >= K_ROWS and d % (2 * HEAD_DIM) == 0 and n_exp <= LANE and batch < 16
    n_ct = n_ctx // TM

    xa = jnp.concatenate([ctx, x], axis=1)
    c_all = jnp.zeros((16, d), F32).at[:batch].set(c).at[batch].set(c_ctx)
    mod = _ada(c_all, ada_w, ada_b).reshape(depth, 16, 6, d)
    rope = _rope_tables(n_ctx, seq)

    for l in range(depth):
        last = l == depth - 1
        jx = l // 2
        g = norm_g[l]
        wr = jnp.zeros((d, LANE), F32).at[:, :n_exp].set(moe_w_router[l])
        wr_hi = wr.astype(BF16)
        wr2 = jnp.stack([wr_hi, (wr - wr_hi.astype(F32)).astype(BF16)])
        if l % 2 == 0:
            scale = HEAD_DIM ** -0.5
            w_qkv = na_w_qkv[jx]
            wk = w_qkv[:, d:2 * d].astype(BF16)
            wt = jnp.concatenate([w_qkv[:, :d] * (scale * LOG2E), w_qkv[:, 2 * d:]], axis=1).T.astype(BF16)
            k, qt, vt = _proj_na(xa, mod[l], g, wk, wt, n_ct)
            o = _na(k, qt, vt, _na_bias_table(na_rel_bias[jx]), n_ctx, rows)
            x1, h2s, logits = _post((o,), na_w_out[jx].astype(BF16), xa, mod[l], g, wr2, n_ct)
        else:
            qk_w = ml_heads * HEAD_DIM
            v_w = ml_heads * ML_V_DIM
            main_w = 2 * qk_w + 2 * v_w
            w_in = ml_w_in[jx]
            ng = 2 * ml_heads
            pad = jnp.zeros((d, LANE - ng), F32)
            w_gates = [w_in[:, main_w:main_w + ng], pad, w_in[:, main_w + ng:], pad]
            wk = jnp.concatenate([w_in[:, qk_w:2 * qk_w] * (HEAD_DIM ** -0.5)] + w_gates, axis=1).astype(BF16)
            wt = jnp.concatenate([w_in[:, :qk_w], w_in[:, 2 * qk_w:main_w]] + w_gates, axis=1).T.astype(BF16)
            zpad = jnp.zeros((LANE - ng,), F32)
            bg = jnp.concatenate([ml_b_gate[jx][:ng], zpad, ml_b_gate[jx][ng:], zpad])[None, :]
            bgt = jnp.broadcast_to(bg.T, (2 * LANE, TM))
            k, gates, qt, vt, ot, gates_t = _proj_ml(xa, mod[l], g, wk, wt, bg, bgt, rope, n_ct, qk_w, v_w)
            hf = _scan(qt, k, vt, gates, gates_t, ml_heads, n_ctx, reverse=False)
            hb = _scan(qt, k, vt, gates, gates_t, ml_heads, n_ctx, reverse=True)
            nw = jnp.broadcast_to(ml_norm_w[jx][:, None], (v_w, LANE))
            x1, h2s, logits = _post((hf, hb, ot, nw), ml_w_out[jx].astype(BF16), xa, mod[l], g,
                                    wr2, n_ct, ml_heads=ml_heads)
        xa = _moe(h2s, logits, x1, mod[l], g, l, moe_w_gate, moe_w_up, moe_w_down, n_ctx, n_ct,
                  n_ct if last else 0)
    return xa
```

```python
import functools

import numpy as np
import jax
import jax.numpy as jnp
from jax import lax
from jax.experimental import pallas as pl
from jax.experimental.pallas import tpu as pltpu

F32 = jnp.float32
BF16 = jnp.bfloat16

LANE = 128
SUBLANE = 8
VMEM_LIMIT = 56 * 1024 * 1024

EPS = 1e-6
NEG = -1e30
LOG2E = 1.4426950408889634
GRID_W = 64
HEAD_DIM = 64
ML_V_DIM = 128
WIN_ROWS = 8
WIN_COLS = 16
ROPE_BASE = 10000.0
EC_CAPACITY = 2

TM = 256
Q_ROWS = TM // GRID_W
K_ROWS = Q_ROWS + WIN_ROWS
ML_CHUNK = 256
NA_PAIRS = 4
RANK_RADIX = 32
RANK_HI = 16
TOKEN_SPLIT = 64


def _cp(sem):
    return pltpu.CompilerParams(dimension_semantics=sem, vmem_limit_bytes=VMEM_LIMIT)


def _dot(a, b):
    return jnp.dot(a, b, preferred_element_type=F32)


def _dot_nt(a, b):
    return lax.dot_general(a, b, (((1,), (1,)), ((), ())), preferred_element_type=F32)


def _dot_tn(a, b):
    return lax.dot_general(a, b, (((0,), (0,)), ((), ())), preferred_element_type=F32)


def _split3(x):
    x1 = x.astype(BF16)
    r1 = x - x1.astype(F32)
    x2 = r1.astype(BF16)
    x3 = (r1 - x2.astype(F32)).astype(BF16)
    return x1, x2, x3


def _rms(x, g):
    return x * lax.rsqrt(jnp.mean(x * x, axis=-1, keepdims=True) + EPS) * g


def _silu(x):
    return x * jax.nn.sigmoid(x)


def _ada_kernel(c_ref, w_ref, b_ref, o_ref):
    s = _silu(c_ref[...]).astype(BF16)
    o_ref[0] = _dot(s, w_ref[0].astype(BF16)) + b_ref[0]


def _ada(c_all, ada_w, ada_b):
    depth, d, n = ada_w.shape
    rows = c_all.shape[0]
    tn = 1024 if n % 1024 == 0 else n
    return pl.pallas_call(
        _ada_kernel,
        grid=(depth, n // tn),
        in_specs=[pl.BlockSpec((rows, d), lambda l, j: (0, 0)),
                  pl.BlockSpec((1, d, tn), lambda l, j: (l, 0, j)),
                  pl.BlockSpec((1, 1, tn), lambda l, j: (l, 0, j))],
        out_specs=pl.BlockSpec((1, rows, tn), lambda l, j: (l, 0, j)),
        out_shape=jax.ShapeDtypeStruct((depth, rows, n), F32),
        compiler_params=_cp(("arbitrary", "arbitrary")),
        name="ada_mod",
    )(c_all, ada_w, ada_b.reshape(depth, 1, n))


def _mod_spec(batch, n_ctx_tiles, d):
    return pl.BlockSpec((1, 6, d), lambda b, i: (jnp.where(i < n_ctx_tiles, batch, b), 0, 0))


def _stream_specs(xs, n_ct):
    if isinstance(xs, tuple):
        d = xs[0].shape[-1]
        return list(xs), [pl.BlockSpec((1, TM, d), lambda b, i: (b, jnp.minimum(i, n_ct - 1), 0)),
                          pl.BlockSpec((1, TM, d), lambda b, i: (b, jnp.maximum(i - n_ct, 0), 0))]
    return [xs], [pl.BlockSpec((1, TM, xs.shape[-1]), lambda b, i: (b, i, 0))]


def _stream_tile(x_refs, n_ct):
    if len(x_refs) == 1:
        return x_refs[0][0]
    return jnp.where(pl.program_id(1) < n_ct, x_refs[0][0], x_refs[1][0])


def _proj_na_kernel(*refs, n_x, n_ct):
    mod_ref, g_ref, wk_ref, wt_ref, k_ref, qt_ref, vt_ref = refs[n_x:]
    m = mod_ref[0]
    h = (_rms(_stream_tile(refs[:n_x], n_ct), g_ref[0:1, :]) * (1.0 + m[1:2, :]) + m[0:1, :]).astype(BF16)
    d = h.shape[1]
    k_ref[0] = _dot(h, wk_ref[...]).astype(BF16)
    rt = _dot_nt(wt_ref[...], h)
    qt_ref[0, 0] = rt[:d].astype(BF16)
    vt_ref[0, 0] = rt[d:].astype(BF16)


def _proj_na(xs, mod, g, wk, wt, n_ctx_tiles):
    x_args, x_specs = _stream_specs(xs, n_ctx_tiles)
    batch, _, d = x_args[0].shape
    tt = sum(a.shape[1] for a in x_args)
    n_tiles = tt // TM
    const = lambda a: pl.BlockSpec(a.shape, lambda b, i: (0,) * a.ndim)
    feat = pl.BlockSpec((1, 1, d, TM), lambda b, i: (b, i, 0, 0))
    return pl.pallas_call(
        functools.partial(_proj_na_kernel, n_x=len(x_args), n_ct=n_ctx_tiles),
        grid=(batch, n_tiles),
        in_specs=x_specs + [_mod_spec(batch, n_ctx_tiles, d), const(g), const(wk), const(wt)],
        out_specs=[pl.BlockSpec((1, TM, d), lambda b, i: (b, i, 0)), feat, feat],
        out_shape=[jax.ShapeDtypeStruct((batch, tt, d), BF16),
                   jax.ShapeDtypeStruct((batch, n_tiles, d, TM), BF16),
                   jax.ShapeDtypeStruct((batch, n_tiles, d, TM), BF16)],
        compiler_params=_cp(("arbitrary", "arbitrary")),
        name="proj_na",
    )(*x_args, mod, g, wk, wt)


def _log_sigmoid(f):
    return jnp.minimum(f, 0.0) - jnp.log1p(jnp.exp(-jnp.abs(f)))


def _proj_ml_kernel(x_ref, mod_ref, g_ref, wk_ref, wt_ref, bg_ref, bgt_ref, cos_ref, sin_ref, cost_ref, sint_ref,
                    k_ref, gate_ref, qt_ref, vt_ref, ot_ref, gatet_ref, *, qk_w, v_w):
    m = mod_ref[0]
    h = (_rms(x_ref[0], g_ref[0:1, :]) * (1.0 + m[1:2, :]) + m[0:1, :]).astype(BF16)
    r = _dot(h, wk_ref[...])
    k = r[:, :qk_w]
    reps = qk_w // LANE
    lane = lax.broadcasted_iota(jnp.int32, k.shape, 1)
    partner = jnp.where(lane % 32 < 16, pltpu.roll(k, qk_w - 16, 1), pltpu.roll(k, 16, 1))
    k = k * jnp.concatenate([cos_ref[...]] * reps, axis=1) + partner * jnp.concatenate([sin_ref[...]] * reps, axis=1)
    k_ref[0] = k.astype(BF16)
    gr = r[:, qk_w:] + bg_ref[...]
    gate_ref[0, :, :LANE] = gr[:, :LANE]
    gate_ref[0, :, LANE:] = _log_sigmoid(gr[:, LANE:])

    rt = _dot_nt(wt_ref[...], h)
    q = rt[:qk_w]
    reps = qk_w // HEAD_DIM
    row = lax.broadcasted_iota(jnp.int32, q.shape, 0)
    partner = jnp.where(row % 32 < 16, pltpu.roll(q, qk_w - 16, 0), pltpu.roll(q, 16, 0))
    q = q * jnp.concatenate([cost_ref[...]] * reps, axis=0) + partner * jnp.concatenate([sint_ref[...]] * reps, axis=0)
    qt_ref[0] = q.astype(BF16)
    vt_ref[0] = rt[qk_w:qk_w + v_w].astype(BF16)
    ot_ref[0] = rt[qk_w + v_w:qk_w + 2 * v_w].astype(BF16)
    gt = rt[qk_w + 2 * v_w:] + bgt_ref[...]
    gatet_ref[0, :LANE, :] = gt[:LANE]
    gatet_ref[0, LANE:, :] = _log_sigmoid(gt[LANE:])


def _proj_ml(xa, mod, g, wk, wt, bg, bgt, tables, n_ctx_tiles, qk_w, v_w):
    batch, tt, d = xa.shape
    cos_t, sin_t, cos_tt, sin_tt = tables
    kern = functools.partial(_proj_ml_kernel, qk_w=qk_w, v_w=v_w)
    const = lambda a: pl.BlockSpec(a.shape, lambda b, i: (0,) * a.ndim)
    tok = lambda w: pl.BlockSpec((1, TM, w), lambda b, i: (b, i, 0))
    feat = lambda w: pl.BlockSpec((1, w, TM), lambda b, i: (b, 0, i))
    return pl.pallas_call(
        kern,
        grid=(batch, tt // TM),
        in_specs=[tok(d), _mod_spec(batch, n_ctx_tiles, d), const(g), const(wk), const(wt), const(bg), const(bgt),
                  pl.BlockSpec((TM, LANE), lambda b, i: (i, 0)),
                  pl.BlockSpec((TM, LANE), lambda b, i: (i, 0)),
                  pl.BlockSpec((HEAD_DIM, TM), lambda b, i: (0, i)),
                  pl.BlockSpec((HEAD_DIM, TM), lambda b, i: (0, i))],
        out_specs=[tok(qk_w), tok(2 * LANE), feat(qk_w), feat(v_w), feat(v_w), feat(2 * LANE)],
        out_shape=[jax.ShapeDtypeStruct((batch, tt, qk_w), BF16),
                   jax.ShapeDtypeStruct((batch, tt, 2 * LANE), F32),
                   jax.ShapeDtypeStruct((batch, qk_w, tt), BF16),
                   jax.ShapeDtypeStruct((batch, v_w, tt), BF16),
                   jax.ShapeDtypeStruct((batch, v_w, tt), BF16),
                   jax.ShapeDtypeStruct((batch, 2 * LANE, tt), F32)],
        compiler_params=_cp(("arbitrary", "arbitrary")),
        name="proj_ml",
    )(xa, mod, g, wk, wt, bg, bgt, cos_t, sin_t, cos_tt, sin_tt)


def _na_kernel(qt_ref, k_ref, vt_ref, tab_ref, sel_ref, o_ref, *, ctx, rows, pairs):
    step = pl.program_id(2)
    n_ct = ctx // TM
    win_tiles = K_ROWS * GRID_W // TM
    ones_rows = jnp.where(lax.broadcasted_iota(jnp.int32, (16, TM), 0) == 0, 1.0, 0.0).astype(BF16)

    def attend(chunks):
        m, acc = None, None
        for s, v_tile in chunks:
            m_new = s.max(axis=0, keepdims=True)
            if m is not None:
                m_new = jnp.maximum(m, m_new)
            p = jnp.exp2(s - m_new).astype(BF16)
            part = _dot(jnp.concatenate([v_tile, ones_rows], axis=0), p)
            acc = part if acc is None else acc * jnp.exp2(m - m_new) + part
            m = m_new
        return acc[:HEAD_DIM] / acc[HEAD_DIM:HEAD_DIM + 1]

    def ctx_chunks(hh, q_t):
        sl = slice(hh * HEAD_DIM, (hh + 1) * HEAD_DIM)
        return [(_dot(k_ref[0, t * TM:(t + 1) * TM, sl], q_t), vt_ref[0, t, sl, :]) for t in range(n_ct)]

    @pl.when(step == 0)
    def _():
        for hd in range(2 * pairs):
            sl = slice(hd * HEAD_DIM, (hd + 1) * HEAD_DIM)
            o_ref[0, 0, sl, :] = attend(ctx_chunks(hd, qt_ref[0, 0, sl, :])).astype(BF16)

    @pl.when(step > 0)
    def _():
        rb = step - 1
        start = jnp.clip(Q_ROWS * rb - WIN_ROWS // 2, 0, rows - K_ROWS)
        t0 = n_ct + start // Q_ROWS
        ks = pl.multiple_of(t0 * TM, TM)
        kr = start + lax.broadcasted_iota(jnp.int32, (16, TM), 0)
        r = Q_ROWS * rb + lax.broadcasted_iota(jnp.int32, (16, TM), 1) // GRID_W
        r0 = jnp.clip(r - WIN_ROWS // 2, 0, rows - WIN_ROWS)
        pen = jnp.where((kr >= r0) & (kr < r0 + WIN_ROWS), 0.0, NEG).astype(BF16)
        pen = jnp.concatenate([pen, jnp.zeros((HEAD_DIM - 16, TM), BF16)], axis=0)
        first_half = lax.broadcasted_iota(jnp.int32, (win_tiles * TM, LANE), 1) < HEAD_DIM

        for hd in range(2 * pairs):
            hh = hd % 2
            sl = slice(hd * HEAD_DIM, (hd + 1) * HEAD_DIM)
            q_t = qt_ref[0, 0, sl, :]
            k_both = k_ref[0, pl.ds(ks, win_tiles * TM), (hd // 2) * LANE:(hd // 2 + 1) * LANE]
            if hh == 0:
                k_aug = jnp.where(first_half, k_both, sel_ref[...])
                q_aug = jnp.concatenate([q_t, pen], axis=0)
            else:
                k_aug = jnp.where(first_half, sel_ref[...], k_both)
                q_aug = jnp.concatenate([pen, q_t], axis=0)
            chunks = []
            for i in range(win_tiles):
                blocks = []
                for krl in range(i * Q_ROWS, (i + 1) * Q_ROWS):
                    tiles = []
                    for u in range(Q_ROWS // 2):
                        dr_e = start + krl - (Q_ROWS * rb + 2 * u) + WIN_ROWS - 1
                        tiles.append(tab_ref[hd, jnp.clip(dr_e, -1, 2 * WIN_ROWS - 1) + 1])
                    blocks.append(jnp.concatenate(tiles, axis=1))
                s_i = _dot(k_aug[i * TM:(i + 1) * TM], q_aug) + jnp.concatenate(blocks, axis=0)
                chunks.append((s_i, vt_ref[0, t0 + i, sl, :]))
            o_ref[0, 0, sl, :] = attend(ctx_chunks(hd, q_t) + chunks).astype(BF16)


def _na_bias_table(rel_bias):
    n_dr, n_dc = 2 * WIN_ROWS - 1, 2 * WIN_COLS - 1
    col = np.arange(GRID_W)
    c0 = np.clip(col - WIN_COLS // 2, 0, GRID_W - WIN_COLS)
    col_ok = (col[:, None] >= c0[None, :]) & (col[:, None] < c0[None, :] + WIN_COLS)
    dc = np.clip(col[:, None] - col[None, :] + WIN_COLS - 1, 0, n_dc - 1)
    oh = ((np.arange(n_dc)[:, None, None] == dc[None]) & col_ok[None]).astype(np.float32)
    cb = jnp.einsum('hrc,ckq->hrkq', rel_bias.astype(F32) * LOG2E, oh, precision=lax.Precision.HIGHEST)
    cb = jnp.where(col_ok[None, None], cb, NEG)
    neg = jnp.full((cb.shape[0], 2, GRID_W, GRID_W), NEG, F32)
    ext = jnp.concatenate([neg, cb, neg[:, :1]], axis=1)
    n_tiles = 2 * WIN_ROWS + 1
    return jnp.concatenate([ext[:, 1:1 + n_tiles], ext[:, 0:n_tiles]], axis=-1)


def _na(k, qt, vt, bias_tab, ctx, rows):
    batch, n_tiles, d, _ = qt.shape
    tt = k.shape[1]
    pairs = min(NA_PAIRS, d // LANE)
    gw = pairs * LANE
    hp = d // gw
    n_rb = rows // Q_ROWS
    n_ct = ctx // TM
    assert n_ct == 1
    q_tile = lambda h, b, s: (b, jnp.where(s == 0, 0, s - 1 + n_ct), h, 0)
    key_row = np.arange(K_ROWS * GRID_W)[:, None] // GRID_W
    key_row_sel = jnp.asarray(key_row == (np.arange(LANE)[None, :] % HEAD_DIM), BF16)
    kern = functools.partial(_na_kernel, ctx=ctx, rows=rows, pairs=pairs)
    return pl.pallas_call(
        kern,
        grid=(hp, batch, n_rb + 1),
        in_specs=[pl.BlockSpec((1, 1, gw, TM), q_tile),
                  pl.BlockSpec((1, tt, gw), lambda h, b, s: (b, 0, h)),
                  pl.BlockSpec((1, n_tiles, gw, TM), lambda h, b, s: (b, 0, h, 0)),
                  pl.BlockSpec((2 * pairs,) + bias_tab.shape[1:], lambda h, b, s: (h, 0, 0, 0)),
                  pl.BlockSpec(key_row_sel.shape, lambda h, b, s: (0, 0))],
        out_specs=pl.BlockSpec((1, 1, gw, TM), q_tile),
        out_shape=jax.ShapeDtypeStruct((batch, n_tiles, d, TM), BF16),
        compiler_params=_cp(("arbitrary", "arbitrary", "arbitrary")),
        name="na_attention",
    )(qt, k, vt, bias_tab, key_row_sel)


def _scan_kernel(qt_ref, k_ref, vt_ref, gate_ref, gatet_ref, o_ref, c_ref, m_ref, *, heads, reverse):
    L = ML_CHUNK
    j = pl.program_id(1)

    @pl.when(j == 0)
    def _():
        c_ref[...] = jnp.zeros_like(c_ref)
        m_ref[...] = jnp.full_like(m_ref, NEG)

    r_i = lax.broadcasted_iota(jnp.int32, (L, L), 0)
    c_i = lax.broadcasted_iota(jnp.int32, (L, L), 1)
    seen = (r_i >= c_i) if reverse else (r_i <= c_i)
    tri_row = jnp.where(seen, 1.0, 0.0).astype(BF16)
    tri_col = jnp.where((r_i <= c_i) if reverse else (r_i >= c_i), 1.0, 0.0).astype(BF16)

    f1, f2, f3 = _split3(gate_ref[0, :, LANE:])
    b_col = _dot(tri_col, f1) + _dot(tri_col, f2) + _dot(tri_col, f3)
    u_col = gate_ref[0, :, :LANE] - b_col
    f1, f2, f3 = _split3(gatet_ref[0, LANE:, :])
    b_row = _dot(f1, tri_row) + _dot(f2, tri_row) + _dot(f3, tri_row)
    i_row = gatet_ref[0, :LANE, :]
    last = 0 if reverse else L - 1
    ones_rows = jnp.where(lax.broadcasted_iota(jnp.int32, (ML_V_DIM, L), 0) == 0, 1.0, 0.0).astype(BF16)

    for h in range(heads):
        c = (heads if reverse else 0) + h
        br = b_row[c:c + 1, :]
        ir = i_row[c:c + 1, :]
        mh = m_ref[h]
        mh_row = jnp.concatenate([mh] * (L // LANE), axis=1)
        a_t = jnp.where(seen, jnp.broadcast_to(u_col[:, c:c + 1], (L, L)), NEG)
        mu = jnp.maximum(a_t.max(axis=0, keepdims=True), mh_row)
        p_t = jnp.exp(a_t - mu)
        w_inter = jnp.exp(mh_row - mu)
        kh = k_ref[0, :, h * HEAD_DIM:(h + 1) * HEAD_DIM]
        state = c_ref[h]
        q_t = qt_ref[0, h * HEAD_DIM:(h + 1) * HEAD_DIM, :]
        r1 = _dot(jnp.concatenate([kh, state.astype(BF16)], axis=0), q_t)
        sp = (r1[:L] * p_t).astype(BF16)
        v_aug = jnp.concatenate([vt_ref[0, h * ML_V_DIM:(h + 1) * ML_V_DIM, :], ones_rows], axis=0)
        rt = _dot(v_aug, sp) + w_inter * r1[L:]
        den = rt[ML_V_DIM:ML_V_DIM + 1]
        o_ref[0, h * ML_V_DIM:(h + 1) * ML_V_DIM, :] = rt[:ML_V_DIM] / jnp.maximum(jnp.abs(den), jnp.exp(-(br + mu)))
        be = br[:, last:last + 1]
        m1 = mh[:, 0:1]
        g_row = be - br + ir
        m_new = jnp.maximum(be + m1, g_row.max(axis=1, keepdims=True))
        w_k = jnp.exp(g_row - m_new)
        decay = jnp.exp(be + m1 - m_new)
        c_ref[h] = decay * state + _dot((v_aug.astype(F32) * w_k).astype(BF16), kh)
        m_ref[h] = jnp.broadcast_to(m_new, (1, LANE))


def _scan(qt, k, vt, gates, gates_t, heads, ctx, reverse):
    batch, tt, qk_w = k.shape
    L = ML_CHUNK
    nc = tt // L
    ncc = ctx // L
    v_w = heads * ML_V_DIM

    def chunk(j):
        if not reverse:
            return j
        return jnp.where(j < ncc, ncc - 1 - j, nc - 1 - (j - ncc))

    kern = functools.partial(_scan_kernel, heads=heads, reverse=reverse)
    feat = lambda w: pl.BlockSpec((1, w, L), lambda b, j: (b, 0, chunk(j)))
    tok = lambda w: pl.BlockSpec((1, L, w), lambda b, j: (b, chunk(j), 0))
    return pl.pallas_call(
        kern,
        grid=(batch, nc),
        in_specs=[feat(qk_w), tok(qk_w), feat(v_w), tok(2 * LANE), feat(2 * LANE)],
        out_specs=feat(v_w),
        out_shape=jax.ShapeDtypeStruct((batch, v_w, tt), F32),
        scratch_shapes=[pltpu.VMEM((heads, 2 * ML_V_DIM, HEAD_DIM), F32),
                        pltpu.VMEM((heads, 1, LANE), F32)],
        compiler_params=_cp(("arbitrary", "arbitrary")),
        name="mlstm_scan_bwd" if reverse else "mlstm_scan_fwd",
    )(qt, k, vt, gates, gates_t)


def _post_common(y, xa, mod_ref, g_ref, wr_ref, x1_ref, h2s_ref, lg_ref):
    m = mod_ref[0]
    x1 = xa + m[2:3, :] * _rms(y, g_ref[1:2, :])
    x1_ref[0] = x1
    h2 = _rms(x1, g_ref[2:3, :]) * (1.0 + m[4:5, :]) + m[3:4, :]
    d = h2.shape[1]
    p = d // LANE
    for jj in range(p):
        h2s_ref[0, pl.ds(jj, TM, stride=p), :] = h2[:, jj * LANE:(jj + 1) * LANE]
    h_hi = h2.astype(BF16)
    h_lo = (h2 - h_hi.astype(F32)).astype(BF16)
    lg_ref[0] = _dot(h_hi, wr_ref[0]) + _dot(h_hi, wr_ref[1]) + _dot(h_lo, wr_ref[0])


def _post_na_kernel(ot_ref, w_ref, *refs, n_x, n_ct):
    y = _dot_tn(ot_ref[0, 0], w_ref[...])
    _post_common(y, _stream_tile(refs[:n_x], n_ct), *refs[n_x:])


def _post_ml_kernel(hf_ref, hb_ref, og_ref, nw_ref, w_ref, xa_ref, mod_ref, g_ref, wr_ref,
                    x1_ref, h2s_ref, lg_ref, *, heads):
    hs = hf_ref[0] + hb_ref[0]
    parts = []
    for h in range(heads):
        t = hs[h * ML_V_DIM:(h + 1) * ML_V_DIM, :]
        parts.append(t * lax.rsqrt(jnp.mean(t * t, axis=0, keepdims=True) + EPS))
    nw = jnp.concatenate([nw_ref[...]] * (TM // LANE), axis=1)
    hn = jnp.concatenate(parts, axis=0) * nw
    y_in = (hn * jax.nn.sigmoid(og_ref[0].astype(F32))).astype(BF16)
    y = _dot_tn(y_in, w_ref[...])
    _post_common(y, xa_ref[0], mod_ref, g_ref, wr_ref, x1_ref, h2s_ref, lg_ref)


def _post(mixer_inputs, w_out, xs, mod, g, wr, n_ctx_tiles, ml_heads=None):
    x_args, x_specs = _stream_specs(xs, n_ctx_tiles)
    batch, _, d = x_args[0].shape
    tt = sum(a.shape[1] for a in x_args)
    p = d // LANE
    tile = lambda w: pl.BlockSpec((1, TM, w), lambda b, i: (b, i, 0))
    full2 = lambda a: pl.BlockSpec(a.shape, lambda b, i: (0,) * a.ndim)
    if ml_heads is None:
        (o,) = mixer_inputs
        kern = functools.partial(_post_na_kernel, n_x=len(x_args), n_ct=n_ctx_tiles)
        head_specs, head_args = [pl.BlockSpec((1, 1, d, TM), lambda b, i: (b, i, 0, 0))], [o]
    else:
        assert len(x_args) == 1
        hf, hb, og, nw = mixer_inputs
        v_w = ml_heads * ML_V_DIM
        kern = functools.partial(_post_ml_kernel, heads=ml_heads)
        feat = pl.BlockSpec((1, v_w, TM), lambda b, i: (b, 0, i))
        head_specs = [feat, feat, feat, full2(nw)]
        head_args = [hf, hb, og, nw]
    return pl.pallas_call(
        kern,
        grid=(batch, tt // TM),
        in_specs=head_specs + [full2(w_out)] + x_specs + [_mod_spec(batch, n_ctx_tiles, d), full2(g), full2(wr)],
        out_specs=[tile(d),
                   pl.BlockSpec((1, TM * p, LANE), lambda b, i: (b, i, 0)),
                   tile(LANE)],
        out_shape=[jax.ShapeDtypeStruct((batch, tt, d), F32),
                   jax.ShapeDtypeStruct((batch, tt * p, LANE), F32),
                   jax.ShapeDtypeStruct((batch, tt, LANE), F32)],
        compiler_params=_cp(("arbitrary", "arbitrary")),
        name="post_mixer",
    )(*head_args, w_out, *x_args, mod, g, wr)


def _route_kernel(lg_ref, *refs, n_exp, sets):
    for si, (off, t, cap) in enumerate(sets):
        _route_set(lg_ref[0, off:off + t, :], refs[2 * si], refs[2 * si + 1], refs[-1], n_exp, cap)


def _route_set(lg, idx_ref, gate_ref, pos_ref, n_exp, cap):
    t = lg.shape[0]
    n_tiles = t // LANE
    lt = lg.T[:n_exp, :]
    e = jnp.exp(lt - lt.max(axis=0, keepdims=True))
    aff = e / e.sum(axis=0, keepdims=True)
    def count_ge(v):
        return jnp.where(aff >= v, 1.0, 0.0).sum(axis=1, keepdims=True)

    def bisect(i, cur):
        cand = cur | jnp.left_shift(jnp.int32(1), 30 - i)
        return jnp.where(count_ge(pltpu.bitcast(cand, F32)) >= cap, cand, cur)

    v_bits = lax.fori_loop(0, 31, bisect, jnp.zeros((n_exp, 1), jnp.int32))

    def refine(_, hi):
        pivot = jnp.where(aff < hi, aff, -1.0).max(axis=1, keepdims=True)
        return jnp.where(count_ge(pivot) >= cap, hi, pivot)

    min_normal_bits = 0x00800000
    hi = lax.fori_loop(0, 3, refine, pltpu.bitcast(jnp.maximum(v_bits + 1, min_normal_bits), F32))
    thr = jnp.where(aff < hi, aff, -1.0).max(axis=1, keepdims=True)
    gt = aff > thr
    eq = aff == thr
    need = cap - jnp.where(gt, 1.0, 0.0).sum(axis=1, keepdims=True)

    upper = jnp.where(lax.broadcasted_iota(jnp.int32, (LANE, LANE), 0)
                      <= lax.broadcasted_iota(jnp.int32, (LANE, LANE), 1), 1.0, 0.0).astype(BF16)

    def excl_cumsum(mask_f32):
        carry = jnp.zeros((n_exp, 1), F32)
        parts = []
        for jt in range(n_tiles):
            tile = mask_f32[:, jt * LANE:(jt + 1) * LANE]
            inc = _dot(tile.astype(BF16), upper)
            parts.append(inc - tile + carry)
            carry = carry + inc[:, LANE - 1:LANE]
        return jnp.concatenate(parts, axis=1)

    eq_f = jnp.where(eq, 1.0, 0.0)
    sel = gt | (eq & (excl_cumsum(eq_f) < need))
    sel_f = jnp.where(sel, 1.0, 0.0)
    pos_ref[:, :t] = jnp.where(sel, excl_cumsum(sel_f), -1.0)

    hi_iota = lax.broadcasted_iota(jnp.int32, (RANK_HI, t), 0).astype(F32)
    lo_iota = lax.broadcasted_iota(jnp.int32, (RANK_RADIX, t), 0).astype(F32)
    t_idx = lax.broadcasted_iota(jnp.int32, (1, t), 1)
    t_hi = (t_idx // TOKEN_SPLIT).astype(F32)
    t_lo = (t_idx % TOKEN_SPLIT).astype(F32)
    for ex in range(n_exp):
        pos = pos_ref[ex:ex + 1, :t]
        hi = jnp.floor(pos * (1.0 / RANK_RADIX))
        lo = pos - hi * RANK_RADIX
        hit = hi_iota == hi
        parts = [t_hi, t_lo] + [a.astype(F32) for a in _split3(aff[ex:ex + 1, :])]
        lhs = jnp.concatenate([jnp.where(hit, v, 0.0) for v in parts], axis=0).astype(BF16)
        rhs = jnp.where(lo_iota == lo, 1.0, 0.0).astype(BF16)
        r = _dot_nt(lhs, rhs)
        idx_ref[0, ex] = (r[:RANK_HI] * TOKEN_SPLIT + r[RANK_HI:2 * RANK_HI]).astype(jnp.int32)
        gate_ref[0, ex] = r[2 * RANK_HI:3 * RANK_HI] + r[3 * RANK_HI:4 * RANK_HI] + r[4 * RANK_HI:]


def _route(logits, n_exp, sets):
    batch, tt, _ = logits.shape
    for _, t, cap in sets:
        assert cap % RANK_RADIX == 0 and cap <= RANK_HI * RANK_RADIX and t <= LANE * TOKEN_SPLIT
    kern = functools.partial(_route_kernel, n_exp=n_exp, sets=tuple(sets))
    out_spec = pl.BlockSpec((1, n_exp, RANK_HI, RANK_RADIX), lambda b: (b, 0, 0, 0))
    outs = pl.pallas_call(
        kern,
        grid=(batch,),
        in_specs=[pl.BlockSpec((1, tt, LANE), lambda b: (b, 0, 0))],
        out_specs=[out_spec, out_spec] * len(sets),
        out_shape=[jax.ShapeDtypeStruct((batch, n_exp, RANK_HI, RANK_RADIX), jnp.int32),
                   jax.ShapeDtypeStruct((batch, n_exp, RANK_HI, RANK_RADIX), F32)] * len(sets),
        scratch_shapes=[pltpu.VMEM((n_exp, max(t for _, t, _ in sets)), F32)],
        compiler_params=_cp(("arbitrary",)),
        name="route",
    )(logits)
    flat = lambda a, cap: a.reshape(batch, n_exp, RANK_HI * RANK_RADIX)[:, :, :cap]
    return [(flat(outs[2 * i], cap), flat(outs[2 * i + 1], cap)) for i, (_, _, cap) in enumerate(sets)]


def _gather_kernel(idx_ref, src_ref, xe_ref, tile_ref, *, slots, p, stride):
    unroll = 8

    def body(c, _):
        for u in range(unroll):
            r = c * unroll + u
            t = idx_ref[0, 0, 0, r]
            tile_ref[pl.ds(r, p, stride=stride), :] = src_ref[0, pl.ds(pl.multiple_of(t * p, p), p), :]
        return 0

    lax.fori_loop(0, slots // unroll, body, 0)
    xe_ref[0, 0] = jnp.concatenate([tile_ref[pl.ds(jj * stride, slots), :] for jj in range(p)], axis=1).astype(BF16)


def _gather(idx, h2s, tt, n_exp):
    batch, _, _, slots = idx.shape
    p = h2s.shape[1] // tt
    d = p * LANE
    stride = slots + SUBLANE
    kern = functools.partial(_gather_kernel, slots=slots, p=p, stride=stride)
    return pl.pallas_call(
        kern,
        grid=(batch, n_exp),
        in_specs=[pl.BlockSpec((1, 1, 1, slots), lambda b, e: (b, e, 0, 0), memory_space=pltpu.SMEM),
                  pl.BlockSpec((1, tt * p, LANE), lambda b, e: (b, 0, 0), pipeline_mode=pl.Buffered(1))],
        out_specs=pl.BlockSpec((1, 1, slots, d), lambda b, e: (e, b, 0, 0)),
        out_shape=jax.ShapeDtypeStruct((n_exp, batch, slots, d), BF16),
        scratch_shapes=[pltpu.VMEM((p * stride, LANE), F32)],
        compiler_params=_cp(("arbitrary", "arbitrary")),
        name="moe_gather",
    )(idx, h2s)


def _ffn_kernel(xe_ref, gate_ref, wg_hbm, wu_hbm, wd_hbm, ye_ref, wg_st, wu_st, wd_st, wgb_ref, wub_ref, wdb_ref,
                sem, *, layer, n_exp, f_chunk, cast_rows):
    e, b = pl.program_id(0), pl.program_id(1)
    stage = ((wg_hbm, wg_st, wgb_ref), (wu_hbm, wu_st, wub_ref), (wd_hbm, wd_st, wdb_ref))

    def weight_copies(ex):
        return [pltpu.make_async_copy(hbm.at[layer, ex], st, sem.at[i]) for i, (hbm, st, _) in enumerate(stage)]

    @pl.when((e == 0) & (b == 0))
    def _():
        for cp in weight_copies(0):
            cp.start()

    @pl.when(b == 0)
    def _():
        for cp in weight_copies(e):
            cp.wait()
        for _, st, dst in stage:
            def cast(r, _, st=st, dst=dst):
                rows = pl.ds(pl.multiple_of(r * cast_rows, cast_rows), cast_rows)
                dst[rows, :] = st[rows, :].astype(BF16)
                return 0
            lax.fori_loop(0, st.shape[0] // cast_rows, cast, 0)

        @pl.when(e + 1 < n_exp)
        def _():
            for cp in weight_copies(e + 1):
                cp.start()

    x = xe_ref[0, 0]
    slots, d = x.shape
    f = wgb_ref.shape[1]
    y = jnp.zeros((slots, d), F32)
    for c in range(f // f_chunk):
        cs = slice(c * f_chunk, (c + 1) * f_chunk)
        a = _dot(x, wgb_ref[:, cs])
        u = _dot(x, wub_ref[:, cs])
        y = y + _dot((_silu(a) * u).astype(BF16), wdb_ref[cs, :])
    y = y * gate_ref[0, 0]
    p = d // LANE
    for jj in range(p):
        ye_ref[0, 0, pl.ds(jj, slots, stride=p), :] = y[:, jj * LANE:(jj + 1) * LANE]


def _ffn(xe, gates, wg, wu, wd, layer):
    n_exp, batch, slots, d = xe.shape
    f = wg.shape[3]
    p = d // LANE
    kern = functools.partial(_ffn_kernel, layer=layer, n_exp=n_exp, f_chunk=min(f, 512), cast_rows=min(d, 256))
    hbm = pl.BlockSpec(memory_space=pl.ANY)
    return pl.pallas_call(
        kern,
        grid=(n_exp, batch),
        in_specs=[pl.BlockSpec((1, 1, slots, d), lambda e, b: (e, b, 0, 0)),
                  pl.BlockSpec((1, 1, slots, 1), lambda e, b: (e, b, 0, 0)),
                  hbm, hbm, hbm],
        out_specs=pl.BlockSpec((1, 1, slots * p, LANE), lambda e, b: (e, b, 0, 0)),
        out_shape=jax.ShapeDtypeStruct((n_exp, batch, slots * p, LANE), F32),
        scratch_shapes=[pltpu.VMEM((d, f), F32), pltpu.VMEM((d, f), F32), pltpu.VMEM((f, d), F32),
                        pltpu.VMEM((d, f), BF16), pltpu.VMEM((d, f), BF16), pltpu.VMEM((f, d), BF16),
                        pltpu.SemaphoreType.DMA((3,))],
        compiler_params=_cp(("arbitrary", "arbitrary")),
        name="moe_ffn",
    )(xe, gates, wg, wu, wd)


def _combine_kernel(idx_ref, ye_ref, acc_ref, *, slots, p):
    @pl.when(pl.program_id(1) == 0)
    def _():
        acc_ref[...] = jnp.zeros_like(acc_ref)

    unroll = 4

    def body(c, _):
        rows, vals = [], []
        for u in range(unroll):
            r = c * unroll + u
            row = pl.multiple_of(idx_ref[0, 0, 0, r] * p, p)
            rows.append(row)
            vals.append(acc_ref[0, pl.ds(row, p), :] + ye_ref[0, 0, pl.ds(pl.multiple_of(r * p, p), p), :])
        for row, val in zip(rows, vals):
            acc_ref[0, pl.ds(row, p), :] = val
        return 0

    lax.fori_loop(0, slots // unroll, body, 0)


def _combine(idx, ye, tt):
    n_exp, batch, sp, _ = ye.shape
    slots = idx.shape[3]
    p = sp // slots
    kern = functools.partial(_combine_kernel, slots=slots, p=p)
    return pl.pallas_call(
        kern,
        grid=(batch, n_exp),
        in_specs=[pl.BlockSpec((1, 1, 1, slots), lambda b, e: (b, e, 0, 0), memory_space=pltpu.SMEM),
                  pl.BlockSpec((1, 1, sp, LANE), lambda b, e: (e, b, 0, 0))],
        out_specs=pl.BlockSpec((1, tt * p, LANE), lambda b, e: (b, 0, 0)),
        out_shape=jax.ShapeDtypeStruct((batch, tt * p, LANE), F32),
        compiler_params=_cp(("arbitrary", "arbitrary")),
        name="moe_combine",
    )(idx, ye)


def _final_kernel(moe_ref, x1_ref, mod_ref, g_ref, o_ref):
    p = x1_ref.shape[2] // LANE
    moe = jnp.concatenate([moe_ref[0, pl.ds(jj, TM, stride=p), :] for jj in range(p)], axis=1)
    o_ref[0] = x1_ref[0] + mod_ref[0][5:6, :] * _rms(moe, g_ref[3:4, :])


def _final(moe_s, x1, mod, g, n_ctx_tiles, skip_tiles):
    batch, tt, d = x1.shape
    p = d // LANE
    n_tiles = tt // TM - skip_tiles
    return pl.pallas_call(
        _final_kernel,
        grid=(batch, n_tiles),
        in_specs=[pl.BlockSpec((1, TM * p, LANE), lambda b, i: (b, i + skip_tiles, 0)),
                  pl.BlockSpec((1, TM, d), lambda b, i: (b, i + skip_tiles, 0)),
                  pl.BlockSpec((1, 6, d), lambda b, i: (jnp.where(i + skip_tiles < n_ctx_tiles, batch, b), 0, 0)),
                  pl.BlockSpec((4, d), lambda b, i: (0, 0))],
        out_specs=pl.BlockSpec((1, TM, d), lambda b, i: (b, i, 0)),
        out_shape=jax.ShapeDtypeStruct((batch, n_tiles * TM, d), F32),
        compiler_params=_cp(("arbitrary", "arbitrary")),
        name="post_ffn",
    )(moe_s, x1, mod, g)


def _rope_tables(ctx, seq):
    half = HEAD_DIM // 2
    inv = ROPE_BASE ** (-np.arange(0, half, 2, dtype=np.float32) / half)
    t = np.arange(seq)
    ang_r = (t // GRID_W).astype(np.float32)[:, None] * inv[None, :]
    ang_c = (t % GRID_W).astype(np.float32)[:, None] * inv[None, :]
    ang = jnp.asarray(np.concatenate([ang_r, ang_r, ang_c, ang_c], axis=1))
    sign = np.tile(np.concatenate([-np.ones(16), np.ones(16)]), 2).astype(np.float32)
    cos = jnp.concatenate([jnp.ones((ctx, HEAD_DIM), F32), jnp.cos(ang)], axis=0)
    sin = jnp.concatenate([jnp.zeros((ctx, HEAD_DIM), F32), jnp.sin(ang) * sign[None, :]], axis=0)
    return jnp.tile(cos, (1, LANE // HEAD_DIM)), jnp.tile(sin, (1, LANE // HEAD_DIM)), cos.T, sin.T


def _moe(h2s, logits, x1, mod, g, layer, w_gate, w_up, w_down, ctx, n_ctx_tiles, skip_tiles):
    batch, tt, d = x1.shape
    n_exp = w_gate.shape[1]
    seq = tt - ctx
    (idx_ctx, gate_ctx), (idx_lat, gate_lat) = _route(
        logits, n_exp, [(0, ctx, EC_CAPACITY * ctx // n_exp), (ctx, seq, EC_CAPACITY * seq // n_exp)])
    idx = jnp.concatenate([idx_ctx, idx_lat + ctx], axis=2)[:, :, None, :]
    gates = jnp.transpose(jnp.concatenate([gate_ctx, gate_lat], axis=2), (1, 0, 2))[..., None]
    xe = _gather(idx, h2s, tt, n_exp)
    ye = _ffn(xe, gates, w_gate, w_up, w_down, layer)
    moe_s = _combine(idx, ye, tt)
    return _final(moe_s, x1, mod, g, n_ctx_tiles, skip_tiles)


def kernel(x, c, ctx, c_ctx, ada_w, ada_b, norm_g, na_w_qkv, na_w_out, na_rel_bias, ml_w_in, ml_b_gate,
           ml_norm_w, ml_w_out, moe_w_router, moe_w_gate, moe_w_up, moe_w_down):
    batch, seq, d = x.shape
    n_ctx = ctx.shape[1]
    depth = ada_w.shape[0]
    rows = seq // GRID_W
    n_exp = moe_w_router.shape[-1]
    ml_heads = ml_norm_w.shape[-1] // ML_V_DIM
    assert n_ctx % TM == 0 and seq % TM == 0 and n_ctx % ML_CHUNK == 0 and seq % ML_CHUNK == 0
    assert rows >= K_ROWS and d % (2 * HEAD_DIM) == 0 and n_exp <= LANE and batch < 16
    n_ct = n_ctx // TM

    xa = (ctx, x)
    c_all = jnp.zeros((16, d), F32).at[:batch].set(c).at[batch].set(c_ctx)
    mod = _ada(c_all, ada_w, ada_b).reshape(depth, 16, 6, d)
    rope = _rope_tables(n_ctx, seq)

    for l in range(depth):
        last = l == depth - 1
        jx = l // 2
        g = norm_g[l]
        wr = jnp.zeros((d, LANE), F32).at[:, :n_exp].set(moe_w_router[l])
        wr_hi = wr.astype(BF16)
        wr2 = jnp.stack([wr_hi, (wr - wr_hi.astype(F32)).astype(BF16)])
        if l % 2 == 0:
            scale = HEAD_DIM ** -0.5
            w_qkv = na_w_qkv[jx]
            wk = w_qkv[:, d:2 * d].astype(BF16)
            wt = jnp.concatenate([w_qkv[:, :d] * (scale * LOG2E), w_qkv[:, 2 * d:]], axis=1).T.astype(BF16)
            k, qt, vt = _proj_na(xa, mod[l], g, wk, wt, n_ct)
            o = _na(k, qt, vt, _na_bias_table(na_rel_bias[jx]), n_ctx, rows)
            x1, h2s, logits = _post((o,), na_w_out[jx].astype(BF16), xa, mod[l], g, wr2, n_ct)
        else:
            qk_w = ml_heads * HEAD_DIM
            v_w = ml_heads * ML_V_DIM
            main_w = 2 * qk_w + 2 * v_w
            w_in = ml_w_in[jx]
            ng = 2 * ml_heads
            pad = jnp.zeros((d, LANE - ng), F32)
            w_gates = [w_in[:, main_w:main_w + ng], pad, w_in[:, main_w + ng:], pad]
            wk = jnp.concatenate([w_in[:, qk_w:2 * qk_w] * (HEAD_DIM ** -0.5)] + w_gates, axis=1).astype(BF16)
            wt = jnp.concatenate([w_in[:, :qk_w], w_in[:, 2 * qk_w:main_w]] + w_gates, axis=1).T.astype(BF16)
            zpad = jnp.zeros((LANE - ng,), F32)
            bg = jnp.concatenate([ml_b_gate[jx][:ng], zpad, ml_b_gate[jx][ng:], zpad])[None, :]
            bgt = jnp.broadcast_to(bg.T, (2 * LANE, TM))
            k, gates, qt, vt, ot, gates_t = _proj_ml(xa, mod[l], g, wk, wt, bg, bgt, rope, n_ct, qk_w, v_w)
            hf = _scan(qt, k, vt, gates, gates_t, ml_heads, n_ctx, reverse=False)
            hb = _scan(qt, k, vt, gates, gates_t, ml_heads, n_ctx, reverse=True)
            nw = jnp.broadcast_to(ml_norm_w[jx][:, None], (v_w, LANE))
            x1, h2s, logits = _post((hf, hb, ot, nw), ml_w_out[jx].astype(BF16), xa, mod[l], g,
                                    wr2, n_ct, ml_heads=ml_heads)
        xa = _moe(h2s, logits, x1, mod[l], g, l, moe_w_gate, moe_w_up, moe_w_down, n_ctx, n_ct,
                  n_ct if last else 0)
    return xa
```

```python
import functools

import numpy as np
import jax
import jax.numpy as jnp
from jax import lax
from jax.experimental import pallas as pl
from jax.experimental.pallas import tpu as pltpu

F32 = jnp.float32
BF16 = jnp.bfloat16

LANE = 128
SUBLANE = 8
VMEM_LIMIT = 56 * 1024 * 1024

EPS = 1e-6
NEG = -1e30
LOG2E = 1.4426950408889634
GRID_W = 64
HEAD_DIM = 64
ML_V_DIM = 128
WIN_ROWS = 8
WIN_COLS = 16
ROPE_BASE = 10000.0
EC_CAPACITY = 2

TM = 256
Q_ROWS = TM // GRID_W
K_ROWS = Q_ROWS + WIN_ROWS
ML_CHUNK = 256
NA_PAIRS = 4
RANK_RADIX = 32
RANK_HI = 16
TOKEN_SPLIT = 64


def _cp(sem):
    return pltpu.CompilerParams(dimension_semantics=sem, vmem_limit_bytes=VMEM_LIMIT)


def _dot(a, b):
    return jnp.dot(a, b, preferred_element_type=F32)


def _dot_nt(a, b):
    return lax.dot_general(a, b, (((1,), (1,)), ((), ())), preferred_element_type=F32)


def _dot_tn(a, b):
    return lax.dot_general(a, b, (((0,), (0,)), ((), ())), preferred_element_type=F32)


def _split3(x):
    x1 = x.astype(BF16)
    r1 = x - x1.astype(F32)
    x2 = r1.astype(BF16)
    x3 = (r1 - x2.astype(F32)).astype(BF16)
    return x1, x2, x3


def _rms(x, g):
    return x * lax.rsqrt(jnp.mean(x * x, axis=-1, keepdims=True) + EPS) * g


def _silu(x):
    return x * jax.nn.sigmoid(x)


def _ada_kernel(c_ref, w_ref, b_ref, o_ref):
    s = _silu(c_ref[...]).astype(BF16)
    o_ref[0] = _dot(s, w_ref[0].astype(BF16)) + b_ref[0]


def _ada(c_all, ada_w, ada_b):
    depth, d, n = ada_w.shape
    rows = c_all.shape[0]
    tn = 1024 if n % 1024 == 0 else n
    return pl.pallas_call(
        _ada_kernel,
        grid=(depth, n // tn),
        in_specs=[pl.BlockSpec((rows, d), lambda l, j: (0, 0)),
                  pl.BlockSpec((1, d, tn), lambda l, j: (l, 0, j)),
                  pl.BlockSpec((1, 1, tn), lambda l, j: (l, 0, j))],
        out_specs=pl.BlockSpec((1, rows, tn), lambda l, j: (l, 0, j)),
        out_shape=jax.ShapeDtypeStruct((depth, rows, n), F32),
        compiler_params=_cp(("arbitrary", "arbitrary")),
        name="ada_mod",
    )(c_all, ada_w, ada_b.reshape(depth, 1, n))


def _mod_spec(batch, n_ctx_tiles, d):
    return pl.BlockSpec((1, 6, d), lambda b, i: (jnp.where(i < n_ctx_tiles, batch, b), 0, 0))


def _stream_specs(xs, n_ct):
    if isinstance(xs, tuple):
        d = xs[0].shape[-1]
        return list(xs), [pl.BlockSpec((1, TM, d), lambda b, i: (b, jnp.minimum(i, n_ct - 1), 0)),
                          pl.BlockSpec((1, TM, d), lambda b, i: (b, jnp.maximum(i - n_ct, 0), 0))]
    return [xs], [pl.BlockSpec((1, TM, xs.shape[-1]), lambda b, i: (b, i, 0))]


def _stream_tile(x_refs, n_ct):
    if len(x_refs) == 1:
        return x_refs[0][0]
    return jnp.where(pl.program_id(1) < n_ct, x_refs[0][0], x_refs[1][0])


def _proj_na_kernel(*refs, n_x, n_ct):
    mod_ref, g_ref, wk_ref, wt_ref, k_ref, qt_ref, vt_ref = refs[n_x:]
    m = mod_ref[0]
    h = (_rms(_stream_tile(refs[:n_x], n_ct), g_ref[0:1, :]) * (1.0 + m[1:2, :]) + m[0:1, :]).astype(BF16)
    d = h.shape[1]
    k_ref[0] = _dot(h, wk_ref[...]).astype(BF16)
    rt = _dot_nt(wt_ref[...], h)
    qt_ref[0, 0] = rt[:d].astype(BF16)
    vt_ref[0, 0] = rt[d:].astype(BF16)


def _proj_na(xs, mod, g, wk, wt, n_ctx_tiles):
    x_args, x_specs = _stream_specs(xs, n_ctx_tiles)
    batch, _, d = x_args[0].shape
    tt = sum(a.shape[1] for a in x_args)
    n_tiles = tt // TM
    const = lambda a: pl.BlockSpec(a.shape, lambda b, i: (0,) * a.ndim)
    feat = pl.BlockSpec((1, 1, d, TM), lambda b, i: (b, i, 0, 0))
    return pl.pallas_call(
        functools.partial(_proj_na_kernel, n_x=len(x_args), n_ct=n_ctx_tiles),
        grid=(batch, n_tiles),
        in_specs=x_specs + [_mod_spec(batch, n_ctx_tiles, d), const(g), const(wk), const(wt)],
        out_specs=[pl.BlockSpec((1, TM, d), lambda b, i: (b, i, 0)), feat, feat],
        out_shape=[jax.ShapeDtypeStruct((batch, tt, d), BF16),
                   jax.ShapeDtypeStruct((batch, n_tiles, d, TM), BF16),
                   jax.ShapeDtypeStruct((batch, n_tiles, d, TM), BF16)],
        compiler_params=_cp(("arbitrary", "arbitrary")),
        name="proj_na",
    )(*x_args, mod, g, wk, wt)


def _log_sigmoid(f):
    return jnp.minimum(f, 0.0) - jnp.log1p(jnp.exp(-jnp.abs(f)))


def _proj_ml_kernel(x_ref, mod_ref, g_ref, wk_ref, wt_ref, bg_ref, bgt_ref, cos_ref, sin_ref, cost_ref, sint_ref,
                    k_ref, gate_ref, qt_ref, vt_ref, ot_ref, gatet_ref, *, qk_w, v_w):
    m = mod_ref[0]
    h = (_rms(x_ref[0], g_ref[0:1, :]) * (1.0 + m[1:2, :]) + m[0:1, :]).astype(BF16)
    r = _dot(h, wk_ref[...])
    k = r[:, :qk_w]
    reps = qk_w // LANE
    lane = lax.broadcasted_iota(jnp.int32, k.shape, 1)
    partner = jnp.where(lane % 32 < 16, pltpu.roll(k, qk_w - 16, 1), pltpu.roll(k, 16, 1))
    k = k * jnp.concatenate([cos_ref[...]] * reps, axis=1) + partner * jnp.concatenate([sin_ref[...]] * reps, axis=1)
    k_ref[0] = k.astype(BF16)
    gr = r[:, qk_w:] + bg_ref[...]
    gate_ref[0, :, :LANE] = gr[:, :LANE]
    gate_ref[0, :, LANE:] = _log_sigmoid(gr[:, LANE:])

    rt = _dot_nt(wt_ref[...], h)
    q = rt[:qk_w]
    reps = qk_w // HEAD_DIM
    row = lax.broadcasted_iota(jnp.int32, q.shape, 0)
    partner = jnp.where(row % 32 < 16, pltpu.roll(q, qk_w - 16, 0), pltpu.roll(q, 16, 0))
    q = q * jnp.concatenate([cost_ref[...]] * reps, axis=0) + partner * jnp.concatenate([sint_ref[...]] * reps, axis=0)
    qt_ref[0] = q.astype(BF16)
    vt_ref[0] = rt[qk_w:qk_w + v_w].astype(BF16)
    ot_ref[0] = rt[qk_w + v_w:qk_w + 2 * v_w].astype(BF16)
    gt = rt[qk_w + 2 * v_w:] + bgt_ref[...]
    gatet_ref[0, :LANE, :] = gt[:LANE]
    gatet_ref[0, LANE:, :] = _log_sigmoid(gt[LANE:])


def _proj_ml(xa, mod, g, wk, wt, bg, bgt, tables, n_ctx_tiles, qk_w, v_w):
    batch, tt, d = xa.shape
    cos_t, sin_t, cos_tt, sin_tt = tables
    kern = functools.partial(_proj_ml_kernel, qk_w=qk_w, v_w=v_w)
    const = lambda a: pl.BlockSpec(a.shape, lambda b, i: (0,) * a.ndim)
    tok = lambda w: pl.BlockSpec((1, TM, w), lambda b, i: (b, i, 0))
    feat = lambda w: pl.BlockSpec((1, w, TM), lambda b, i: (b, 0, i))
    return pl.pallas_call(
        kern,
        grid=(batch, tt // TM),
        in_specs=[tok(d), _mod_spec(batch, n_ctx_tiles, d), const(g), const(wk), const(wt), const(bg), const(bgt),
                  pl.BlockSpec((TM, LANE), lambda b, i: (i, 0)),
                  pl.BlockSpec((TM, LANE), lambda b, i: (i, 0)),
                  pl.BlockSpec((HEAD_DIM, TM), lambda b, i: (0, i)),
                  pl.BlockSpec((HEAD_DIM, TM), lambda b, i: (0, i))],
        out_specs=[tok(qk_w), tok(2 * LANE), feat(qk_w), feat(v_w), feat(v_w), feat(2 * LANE)],
        out_shape=[jax.ShapeDtypeStruct((batch, tt, qk_w), BF16),
                   jax.ShapeDtypeStruct((batch, tt, 2 * LANE), F32),
                   jax.ShapeDtypeStruct((batch, qk_w, tt), BF16),
                   jax.ShapeDtypeStruct((batch, v_w, tt), BF16),
                   jax.ShapeDtypeStruct((batch, v_w, tt), BF16),
                   jax.ShapeDtypeStruct((batch, 2 * LANE, tt), F32)],
        compiler_params=_cp(("arbitrary", "arbitrary")),
        name="proj_ml",
    )(xa, mod, g, wk, wt, bg, bgt, cos_t, sin_t, cos_tt, sin_tt)


def _na_kernel(qt_ref, k_ref, vt_ref, tab_ref, sel_ref, o_ref, *, ctx, rows, pairs):
    step = pl.program_id(2)
    n_ct = ctx // TM
    win_tiles = K_ROWS * GRID_W // TM
    ones_rows = jnp.where(lax.broadcasted_iota(jnp.int32, (16, TM), 0) == 0, 1.0, 0.0).astype(BF16)

    def attend(chunks):
        m, acc = None, None
        for s, v_tile in chunks:
            m_new = s.max(axis=0, keepdims=True)
            if m is not None:
                m_new = jnp.maximum(m, m_new)
            p = jnp.exp2(s - m_new).astype(BF16)
            part = _dot(jnp.concatenate([v_tile, ones_rows], axis=0), p)
            acc = part if acc is None else acc * jnp.exp2(m - m_new) + part
            m = m_new
        return acc[:HEAD_DIM] / acc[HEAD_DIM:HEAD_DIM + 1]

    def ctx_chunks(hh, q_t):
        sl = slice(hh * HEAD_DIM, (hh + 1) * HEAD_DIM)
        return [(_dot(k_ref[0, t * TM:(t + 1) * TM, sl], q_t), vt_ref[0, t, sl, :]) for t in range(n_ct)]

    @pl.when(step == 0)
    def _():
        for hd in range(2 * pairs):
            sl = slice(hd * HEAD_DIM, (hd + 1) * HEAD_DIM)
            o_ref[0, 0, sl, :] = attend(ctx_chunks(hd, qt_ref[0, 0, sl, :])).astype(BF16)

    @pl.when(step > 0)
    def _():
        rb = step - 1
        start = jnp.clip(Q_ROWS * rb - WIN_ROWS // 2, 0, rows - K_ROWS)
        t0 = n_ct + start // Q_ROWS
        ks = pl.multiple_of(t0 * TM, TM)
        kr = start + lax.broadcasted_iota(jnp.int32, (16, TM), 0)
        r = Q_ROWS * rb + lax.broadcasted_iota(jnp.int32, (16, TM), 1) // GRID_W
        r0 = jnp.clip(r - WIN_ROWS // 2, 0, rows - WIN_ROWS)
        pen = jnp.where((kr >= r0) & (kr < r0 + WIN_ROWS), 0.0, NEG).astype(BF16)
        pen = jnp.concatenate([pen, jnp.zeros((HEAD_DIM - 16, TM), BF16)], axis=0)
        first_half = lax.broadcasted_iota(jnp.int32, (win_tiles * TM, LANE), 1) < HEAD_DIM

        for hd in range(2 * pairs):
            hh = hd % 2
            sl = slice(hd * HEAD_DIM, (hd + 1) * HEAD_DIM)
            q_t = qt_ref[0, 0, sl, :]
            k_both = k_ref[0, pl.ds(ks, win_tiles * TM), (hd // 2) * LANE:(hd // 2 + 1) * LANE]
            if hh == 0:
                k_aug = jnp.where(first_half, k_both, sel_ref[...])
                q_aug = jnp.concatenate([q_t, pen], axis=0)
            else:
                k_aug = jnp.where(first_half, sel_ref[...], k_both)
                q_aug = jnp.concatenate([pen, q_t], axis=0)
            chunks = []
            for i in range(win_tiles):
                blocks = []
                for krl in range(i * Q_ROWS, (i + 1) * Q_ROWS):
                    tiles = []
                    for u in range(Q_ROWS // 2):
                        dr_e = start + krl - (Q_ROWS * rb + 2 * u) + WIN_ROWS - 1
                        tiles.append(tab_ref[hd, jnp.clip(dr_e, -1, 2 * WIN_ROWS - 1) + 1])
                    blocks.append(jnp.concatenate(tiles, axis=1))
                s_i = _dot(k_aug[i * TM:(i + 1) * TM], q_aug) + jnp.concatenate(blocks, axis=0)
                chunks.append((s_i, vt_ref[0, t0 + i, sl, :]))
            o_ref[0, 0, sl, :] = attend(ctx_chunks(hd, q_t) + chunks).astype(BF16)


def _na_bias_table(rel_bias):
    n_dr, n_dc = 2 * WIN_ROWS - 1, 2 * WIN_COLS - 1
    col = np.arange(GRID_W)
    c0 = np.clip(col - WIN_COLS // 2, 0, GRID_W - WIN_COLS)
    col_ok = (col[:, None] >= c0[None, :]) & (col[:, None] < c0[None, :] + WIN_COLS)
    dc = np.clip(col[:, None] - col[None, :] + WIN_COLS - 1, 0, n_dc - 1)
    oh = ((np.arange(n_dc)[:, None, None] == dc[None]) & col_ok[None]).astype(np.float32)
    cb = jnp.einsum('hrc,ckq->hrkq', rel_bias.astype(F32) * LOG2E, oh, precision=lax.Precision.HIGHEST)
    cb = jnp.where(col_ok[None, None], cb, NEG)
    neg = jnp.full((cb.shape[0], 2, GRID_W, GRID_W), NEG, F32)
    ext = jnp.concatenate([neg, cb, neg[:, :1]], axis=1)
    n_tiles = 2 * WIN_ROWS + 1
    return jnp.concatenate([ext[:, 1:1 + n_tiles], ext[:, 0:n_tiles]], axis=-1)


def _na(k, qt, vt, bias_tab, ctx, rows):
    batch, n_tiles, d, _ = qt.shape
    tt = k.shape[1]
    pairs = min(NA_PAIRS, d // LANE)
    gw = pairs * LANE
    hp = d // gw
    n_rb = rows // Q_ROWS
    n_ct = ctx // TM
    assert n_ct == 1
    q_tile = lambda h, b, s: (b, jnp.where(s == 0, 0, s - 1 + n_ct), h, 0)
    key_row = np.arange(K_ROWS * GRID_W)[:, None] // GRID_W
    key_row_sel = jnp.asarray(key_row == (np.arange(LANE)[None, :] % HEAD_DIM), BF16)
    kern = functools.partial(_na_kernel, ctx=ctx, rows=rows, pairs=pairs)
    return pl.pallas_call(
        kern,
        grid=(hp, batch, n_rb + 1),
        in_specs=[pl.BlockSpec((1, 1, gw, TM), q_tile),
                  pl.BlockSpec((1, tt, gw), lambda h, b, s: (b, 0, h)),
                  pl.BlockSpec((1, n_tiles, gw, TM), lambda h, b, s: (b, 0, h, 0)),
                  pl.BlockSpec((2 * pairs,) + bias_tab.shape[1:], lambda h, b, s: (h, 0, 0, 0)),
                  pl.BlockSpec(key_row_sel.shape, lambda h, b, s: (0, 0))],
        out_specs=pl.BlockSpec((1, 1, gw, TM), q_tile),
        out_shape=jax.ShapeDtypeStruct((batch, n_tiles, d, TM), BF16),
        compiler_params=_cp(("arbitrary", "arbitrary", "arbitrary")),
        name="na_attention",
    )(qt, k, vt, bias_tab, key_row_sel)


def _scan_kernel(qt_ref, k_ref, vt_ref, gate_ref, gatet_ref, o_ref, c_ref, m_ref, *, heads, reverse):
    L = ML_CHUNK
    j = pl.program_id(1)

    @pl.when(j == 0)
    def _():
        c_ref[...] = jnp.zeros_like(c_ref)
        m_ref[...] = jnp.full_like(m_ref, NEG)

    r_i = lax.broadcasted_iota(jnp.int32, (L, L), 0)
    c_i = lax.broadcasted_iota(jnp.int32, (L, L), 1)
    seen = (r_i >= c_i) if reverse else (r_i <= c_i)
    tri_row = jnp.where(seen, 1.0, 0.0).astype(BF16)
    tri_col = jnp.where((r_i <= c_i) if reverse else (r_i >= c_i), 1.0, 0.0).astype(BF16)

    f1, f2, f3 = _split3(gate_ref[0, :, LANE:])
    b_col = _dot(tri_col, f1) + _dot(tri_col, f2) + _dot(tri_col, f3)
    u_col = gate_ref[0, :, :LANE] - b_col
    f1, f2, f3 = _split3(gatet_ref[0, LANE:, :])
    b_row = _dot(f1, tri_row) + _dot(f2, tri_row) + _dot(f3, tri_row)
    i_row = gatet_ref[0, :LANE, :]
    last = 0 if reverse else L - 1
    ones_rows = jnp.where(lax.broadcasted_iota(jnp.int32, (ML_V_DIM, L), 0) == 0, 1.0, 0.0).astype(BF16)

    for h in range(heads):
        c = (heads if reverse else 0) + h
        br = b_row[c:c + 1, :]
        ir = i_row[c:c + 1, :]
        mh = m_ref[h]
        mh_row = jnp.concatenate([mh] * (L // LANE), axis=1)
        a_t = jnp.where(seen, jnp.broadcast_to(u_col[:, c:c + 1], (L, L)), NEG)
        mu = jnp.maximum(a_t.max(axis=0, keepdims=True), mh_row)
        p_t = jnp.exp(a_t - mu)
        w_inter = jnp.exp(mh_row - mu)
        kh = k_ref[0, :, h * HEAD_DIM:(h + 1) * HEAD_DIM]
        state = c_ref[h]
        q_t = qt_ref[0, h * HEAD_DIM:(h + 1) * HEAD_DIM, :]
        r1 = _dot(jnp.concatenate([kh, state.astype(BF16)], axis=0), q_t)
        sp = (r1[:L] * p_t).astype(BF16)
        v_aug = jnp.concatenate([vt_ref[0, h * ML_V_DIM:(h + 1) * ML_V_DIM, :], ones_rows], axis=0)
        rt = _dot(v_aug, sp) + w_inter * r1[L:]
        den = rt[ML_V_DIM:ML_V_DIM + 1]
        h_out = rt[:ML_V_DIM] / jnp.maximum(jnp.abs(den), jnp.exp(-(br + mu)))
        o_ref[0, h * ML_V_DIM:(h + 1) * ML_V_DIM, :] = h_out.astype(BF16)
        be = br[:, last:last + 1]
        m1 = mh[:, 0:1]
        g_row = be - br + ir
        m_new = jnp.maximum(be + m1, g_row.max(axis=1, keepdims=True))
        w_k = jnp.exp(g_row - m_new)
        decay = jnp.exp(be + m1 - m_new)
        c_ref[h] = decay * state + _dot((v_aug.astype(F32) * w_k).astype(BF16), kh)
        m_ref[h] = jnp.broadcast_to(m_new, (1, LANE))


def _scan(qt, k, vt, gates, gates_t, heads, ctx, reverse):
    batch, tt, qk_w = k.shape
    L = ML_CHUNK
    nc = tt // L
    ncc = ctx // L
    v_w = heads * ML_V_DIM

    def chunk(j):
        if not reverse:
            return j
        return jnp.where(j < ncc, ncc - 1 - j, nc - 1 - (j - ncc))

    kern = functools.partial(_scan_kernel, heads=heads, reverse=reverse)
    feat = lambda w: pl.BlockSpec((1, w, L), lambda b, j: (b, 0, chunk(j)))
    tok = lambda w: pl.BlockSpec((1, L, w), lambda b, j: (b, chunk(j), 0))
    return pl.pallas_call(
        kern,
        grid=(batch, nc),
        in_specs=[feat(qk_w), tok(qk_w), feat(v_w), tok(2 * LANE), feat(2 * LANE)],
        out_specs=feat(v_w),
        out_shape=jax.ShapeDtypeStruct((batch, v_w, tt), BF16),
        scratch_shapes=[pltpu.VMEM((heads, 2 * ML_V_DIM, HEAD_DIM), F32),
                        pltpu.VMEM((heads, 1, LANE), F32)],
        compiler_params=_cp(("arbitrary", "arbitrary")),
        name="mlstm_scan_bwd" if reverse else "mlstm_scan_fwd",
    )(qt, k, vt, gates, gates_t)


def _post_common(y, xa, mod_ref, g_ref, wr_ref, x1_ref, h2s_ref, lg_ref):
    m = mod_ref[0]
    x1 = xa + m[2:3, :] * _rms(y, g_ref[1:2, :])
    x1_ref[0] = x1
    h2 = _rms(x1, g_ref[2:3, :]) * (1.0 + m[4:5, :]) + m[3:4, :]
    d = h2.shape[1]
    p = d // LANE
    for jj in range(p):
        h2s_ref[0, pl.ds(jj, TM, stride=p), :] = h2[:, jj * LANE:(jj + 1) * LANE]
    h_hi = h2.astype(BF16)
    h_lo = (h2 - h_hi.astype(F32)).astype(BF16)
    lg_ref[0] = _dot(h_hi, wr_ref[0]) + _dot(h_hi, wr_ref[1]) + _dot(h_lo, wr_ref[0])


def _post_na_kernel(ot_ref, w_ref, *refs, n_x, n_ct):
    y = _dot_tn(ot_ref[0, 0], w_ref[...])
    _post_common(y, _stream_tile(refs[:n_x], n_ct), *refs[n_x:])


def _post_ml_kernel(hf_ref, hb_ref, og_ref, nw_ref, w_ref, xa_ref, mod_ref, g_ref, wr_ref,
                    x1_ref, h2s_ref, lg_ref, *, heads):
    hs = hf_ref[0].astype(F32) + hb_ref[0].astype(F32)
    parts = []
    for h in range(heads):
        t = hs[h * ML_V_DIM:(h + 1) * ML_V_DIM, :]
        parts.append(t * lax.rsqrt(jnp.mean(t * t, axis=0, keepdims=True) + EPS))
    nw = jnp.concatenate([nw_ref[...]] * (TM // LANE), axis=1)
    hn = jnp.concatenate(parts, axis=0) * nw
    y_in = (hn * jax.nn.sigmoid(og_ref[0].astype(F32))).astype(BF16)
    y = _dot_tn(y_in, w_ref[...])
    _post_common(y, xa_ref[0], mod_ref, g_ref, wr_ref, x1_ref, h2s_ref, lg_ref)


def _post(mixer_inputs, w_out, xs, mod, g, wr, n_ctx_tiles, ml_heads=None):
    x_args, x_specs = _stream_specs(xs, n_ctx_tiles)
    batch, _, d = x_args[0].shape
    tt = sum(a.shape[1] for a in x_args)
    p = d // LANE
    tile = lambda w: pl.BlockSpec((1, TM, w), lambda b, i: (b, i, 0))
    full2 = lambda a: pl.BlockSpec(a.shape, lambda b, i: (0,) * a.ndim)
    if ml_heads is None:
        (o,) = mixer_inputs
        kern = functools.partial(_post_na_kernel, n_x=len(x_args), n_ct=n_ctx_tiles)
        head_specs, head_args = [pl.BlockSpec((1, 1, d, TM), lambda b, i: (b, i, 0, 0))], [o]
    else:
        assert len(x_args) == 1
        hf, hb, og, nw = mixer_inputs
        v_w = ml_heads * ML_V_DIM
        kern = functools.partial(_post_ml_kernel, heads=ml_heads)
        feat = pl.BlockSpec((1, v_w, TM), lambda b, i: (b, 0, i))
        head_specs = [feat, feat, feat, full2(nw)]
        head_args = [hf, hb, og, nw]
    return pl.pallas_call(
        kern,
        grid=(batch, tt // TM),
        in_specs=head_specs + [full2(w_out)] + x_specs + [_mod_spec(batch, n_ctx_tiles, d), full2(g), full2(wr)],
        out_specs=[tile(d),
                   pl.BlockSpec((1, TM * p, LANE), lambda b, i: (b, i, 0)),
                   tile(LANE)],
        out_shape=[jax.ShapeDtypeStruct((batch, tt, d), F32),
                   jax.ShapeDtypeStruct((batch, tt * p, LANE), F32),
                   jax.ShapeDtypeStruct((batch, tt, LANE), F32)],
        compiler_params=_cp(("arbitrary", "arbitrary")),
        name="post_mixer",
    )(*head_args, w_out, *x_args, mod, g, wr)


def _route_kernel(lg_ref, *refs, n_exp, sets):
    for si, (off, t, cap) in enumerate(sets):
        _route_set(lg_ref[0, off:off + t, :], refs[2 * si], refs[2 * si + 1], refs[-1], n_exp, cap)


def _route_set(lg, idx_ref, gate_ref, pos_ref, n_exp, cap):
    t = lg.shape[0]
    n_tiles = t // LANE
    lt = lg.T[:n_exp, :]
    e = jnp.exp(lt - lt.max(axis=0, keepdims=True))
    aff = e / e.sum(axis=0, keepdims=True)
    def count_ge(v):
        return jnp.where(aff >= v, 1.0, 0.0).sum(axis=1, keepdims=True)

    def bisect(i, cur):
        cand = cur | jnp.left_shift(jnp.int32(1), 30 - i)
        return jnp.where(count_ge(pltpu.bitcast(cand, F32)) >= cap, cand, cur)

    v_bits = lax.fori_loop(0, 31, bisect, jnp.zeros((n_exp, 1), jnp.int32))

    def refine(_, hi):
        pivot = jnp.where(aff < hi, aff, -1.0).max(axis=1, keepdims=True)
        return jnp.where(count_ge(pivot) >= cap, hi, pivot)

    min_normal_bits = 0x00800000
    hi = lax.fori_loop(0, 3, refine, pltpu.bitcast(jnp.maximum(v_bits + 1, min_normal_bits), F32))
    thr = jnp.where(aff < hi, aff, -1.0).max(axis=1, keepdims=True)
    gt = aff > thr
    eq = aff == thr
    need = cap - jnp.where(gt, 1.0, 0.0).sum(axis=1, keepdims=True)

    upper = jnp.where(lax.broadcasted_iota(jnp.int32, (LANE, LANE), 0)
                      <= lax.broadcasted_iota(jnp.int32, (LANE, LANE), 1), 1.0, 0.0).astype(BF16)

    def excl_cumsum(mask_f32):
        carry = jnp.zeros((n_exp, 1), F32)
        parts = []
        for jt in range(n_tiles):
            tile = mask_f32[:, jt * LANE:(jt + 1) * LANE]
            inc = _dot(tile.astype(BF16), upper)
            parts.append(inc - tile + carry)
            carry = carry + inc[:, LANE - 1:LANE]
        return jnp.concatenate(parts, axis=1)

    eq_f = jnp.where(eq, 1.0, 0.0)
    sel = gt | (eq & (excl_cumsum(eq_f) < need))
    sel_f = jnp.where(sel, 1.0, 0.0)
    pos_ref[:, :t] = jnp.where(sel, excl_cumsum(sel_f), -1.0)

    hi_iota = lax.broadcasted_iota(jnp.int32, (RANK_HI, t), 0).astype(F32)
    lo_iota = lax.broadcasted_iota(jnp.int32, (RANK_RADIX, t), 0).astype(F32)
    t_idx = lax.broadcasted_iota(jnp.int32, (1, t), 1)
    t_hi = (t_idx // TOKEN_SPLIT).astype(F32)
    t_lo = (t_idx % TOKEN_SPLIT).astype(F32)
    for ex in range(n_exp):
        pos = pos_ref[ex:ex + 1, :t]
        hi = jnp.floor(pos * (1.0 / RANK_RADIX))
        lo = pos - hi * RANK_RADIX
        hit = hi_iota == hi
        parts = [t_hi, t_lo] + [a.astype(F32) for a in _split3(aff[ex:ex + 1, :])]
        lhs = jnp.concatenate([jnp.where(hit, v, 0.0) for v in parts], axis=0).astype(BF16)
        rhs = jnp.where(lo_iota == lo, 1.0, 0.0).astype(BF16)
        r = _dot_nt(lhs, rhs)
        idx_ref[0, ex] = (r[:RANK_HI] * TOKEN_SPLIT + r[RANK_HI:2 * RANK_HI]).astype(jnp.int32)
        gate_ref[0, ex] = r[2 * RANK_HI:3 * RANK_HI] + r[3 * RANK_HI:4 * RANK_HI] + r[4 * RANK_HI:]


def _route(logits, n_exp, sets):
    batch, tt, _ = logits.shape
    for _, t, cap in sets:
        assert cap % RANK_RADIX == 0 and cap <= RANK_HI * RANK_RADIX and t <= LANE * TOKEN_SPLIT
    kern = functools.partial(_route_kernel, n_exp=n_exp, sets=tuple(sets))
    out_spec = pl.BlockSpec((1, n_exp, RANK_HI, RANK_RADIX), lambda b: (b, 0, 0, 0))
    outs = pl.pallas_call(
        kern,
        grid=(batch,),
        in_specs=[pl.BlockSpec((1, tt, LANE), lambda b: (b, 0, 0))],
        out_specs=[out_spec, out_spec] * len(sets),
        out_shape=[jax.ShapeDtypeStruct((batch, n_exp, RANK_HI, RANK_RADIX), jnp.int32),
                   jax.ShapeDtypeStruct((batch, n_exp, RANK_HI, RANK_RADIX), F32)] * len(sets),
        scratch_shapes=[pltpu.VMEM((n_exp, max(t for _, t, _ in sets)), F32)],
        compiler_params=_cp(("arbitrary",)),
        name="route",
    )(logits)
    flat = lambda a, cap: a.reshape(batch, n_exp, RANK_HI * RANK_RADIX)[:, :, :cap]
    return [(flat(outs[2 * i], cap), flat(outs[2 * i + 1], cap)) for i, (_, _, cap) in enumerate(sets)]


def _gather_kernel(idx_ref, src_ref, xe_ref, tile_ref, *, slots, p, stride):
    unroll = 8

    def body(c, _):
        for u in range(unroll):
            r = c * unroll + u
            t = idx_ref[0, 0, 0, r]
            tile_ref[pl.ds(r, p, stride=stride), :] = src_ref[0, pl.ds(pl.multiple_of(t * p, p), p), :]
        return 0

    lax.fori_loop(0, slots // unroll, body, 0)
    xe_ref[0, 0] = jnp.concatenate([tile_ref[pl.ds(jj * stride, slots), :] for jj in range(p)], axis=1).astype(BF16)


def _gather(idx, h2s, tt, n_exp):
    batch, _, _, slots = idx.shape
    p = h2s.shape[1] // tt
    d = p * LANE
    stride = slots + SUBLANE
    kern = functools.partial(_gather_kernel, slots=slots, p=p, stride=stride)
    return pl.pallas_call(
        kern,
        grid=(batch, n_exp),
        in_specs=[pl.BlockSpec((1, 1, 1, slots), lambda b, e: (b, e, 0, 0), memory_space=pltpu.SMEM),
                  pl.BlockSpec((1, tt * p, LANE), lambda b, e: (b, 0, 0), pipeline_mode=pl.Buffered(1))],
        out_specs=pl.BlockSpec((1, 1, slots, d), lambda b, e: (e, b, 0, 0)),
        out_shape=jax.ShapeDtypeStruct((n_exp, batch, slots, d), BF16),
        scratch_shapes=[pltpu.VMEM((p * stride, LANE), F32)],
        compiler_params=_cp(("arbitrary", "arbitrary")),
        name="moe_gather",
    )(idx, h2s)


def _ffn_kernel(xe_ref, gate_ref, wg_hbm, wu_hbm, wd_hbm, ye_ref, wg_st, wu_st, wd_st, wgb_ref, wub_ref, wdb_ref,
                sem, *, layer, n_exp, f_chunk, cast_rows):
    e, b = pl.program_id(0), pl.program_id(1)
    stage = ((wg_hbm, wg_st, wgb_ref), (wu_hbm, wu_st, wub_ref), (wd_hbm, wd_st, wdb_ref))

    def weight_copies(ex):
        return [pltpu.make_async_copy(hbm.at[layer, ex], st, sem.at[i]) for i, (hbm, st, _) in enumerate(stage)]

    @pl.when((e == 0) & (b == 0))
    def _():
        for cp in weight_copies(0):
            cp.start()

    @pl.when(b == 0)
    def _():
        for cp in weight_copies(e):
            cp.wait()
        for _, st, dst in stage:
            def cast(r, _, st=st, dst=dst):
                rows = pl.ds(pl.multiple_of(r * cast_rows, cast_rows), cast_rows)
                dst[rows, :] = st[rows, :].astype(BF16)
                return 0
            lax.fori_loop(0, st.shape[0] // cast_rows, cast, 0)

        @pl.when(e + 1 < n_exp)
        def _():
            for cp in weight_copies(e + 1):
                cp.start()

    x = xe_ref[0, 0]
    slots, d = x.shape
    f = wgb_ref.shape[1]
    y = jnp.zeros((slots, d), F32)
    for c in range(f // f_chunk):
        cs = slice(c * f_chunk, (c + 1) * f_chunk)
        a = _dot(x, wgb_ref[:, cs])
        u = _dot(x, wub_ref[:, cs])
        y = y + _dot((_silu(a) * u).astype(BF16), wdb_ref[cs, :])
    y = y * gate_ref[0, 0]
    p = d // LANE
    for jj in range(p):
        ye_ref[0, 0, pl.ds(jj, slots, stride=p), :] = y[:, jj * LANE:(jj + 1) * LANE]


def _ffn(xe, gates, wg, wu, wd, layer):
    n_exp, batch, slots, d = xe.shape
    f = wg.shape[3]
    p = d // LANE
    kern = functools.partial(_ffn_kernel, layer=layer, n_exp=n_exp, f_chunk=min(f, 512), cast_rows=min(d, 256))
    hbm = pl.BlockSpec(memory_space=pl.ANY)
    return pl.pallas_call(
        kern,
        grid=(n_exp, batch),
        in_specs=[pl.BlockSpec((1, 1, slots, d), lambda e, b: (e, b, 0, 0)),
                  pl.BlockSpec((1, 1, slots, 1), lambda e, b: (e, b, 0, 0)),
                  hbm, hbm, hbm],
        out_specs=pl.BlockSpec((1, 1, slots * p, LANE), lambda e, b: (e, b, 0, 0)),
        out_shape=jax.ShapeDtypeStruct((n_exp, batch, slots * p, LANE), F32),
        scratch_shapes=[pltpu.VMEM((d, f), F32), pltpu.VMEM((d, f), F32), pltpu.VMEM((f, d), F32),
                        pltpu.VMEM((d, f), BF16), pltpu.VMEM((d, f), BF16), pltpu.VMEM((f, d), BF16),
                        pltpu.SemaphoreType.DMA((3,))],
        compiler_params=_cp(("arbitrary", "arbitrary")),
        name="moe_ffn",
    )(xe, gates, wg, wu, wd)


def _combine_kernel(idx_ref, ye_ref, acc_ref, *, slots, p):
    @pl.when(pl.program_id(1) == 0)
    def _():
        acc_ref[...] = jnp.zeros_like(acc_ref)

    unroll = 4

    def body(c, _):
        rows, vals = [], []
        for u in range(unroll):
            r = c * unroll + u
            row = pl.multiple_of(idx_ref[0, 0, 0, r] * p, p)
            rows.append(row)
            vals.append(acc_ref[0, pl.ds(row, p), :] + ye_ref[0, 0, pl.ds(pl.multiple_of(r * p, p), p), :])
        for row, val in zip(rows, vals):
            acc_ref[0, pl.ds(row, p), :] = val
        return 0

    lax.fori_loop(0, slots // unroll, body, 0)


def _combine(idx, ye, tt):
    n_exp, batch, sp, _ = ye.shape
    slots = idx.shape[3]
    p = sp // slots
    kern = functools.partial(_combine_kernel, slots=slots, p=p)
    return pl.pallas_call(
        kern,
        grid=(batch, n_exp),
        in_specs=[pl.BlockSpec((1, 1, 1, slots), lambda b, e: (b, e, 0, 0), memory_space=pltpu.SMEM),
                  pl.BlockSpec((1, 1, sp, LANE), lambda b, e: (e, b, 0, 0))],
        out_specs=pl.BlockSpec((1, tt * p, LANE), lambda b, e: (b, 0, 0)),
        out_shape=jax.ShapeDtypeStruct((batch, tt * p, LANE), F32),
        compiler_params=_cp(("arbitrary", "arbitrary")),
        name="moe_combine",
    )(idx, ye)


def _final_kernel(moe_ref, x1_ref, mod_ref, g_ref, o_ref):
    p = x1_ref.shape[2] // LANE
    moe = jnp.concatenate([moe_ref[0, pl.ds(jj, TM, stride=p), :] for jj in range(p)], axis=1)
    o_ref[0] = x1_ref[0] + mod_ref[0][5:6, :] * _rms(moe, g_ref[3:4, :])


def _final(moe_s, x1, mod, g, n_ctx_tiles, skip_tiles):
    batch, tt, d = x1.shape
    p = d // LANE
    n_tiles = tt // TM - skip_tiles
    return pl.pallas_call(
        _final_kernel,
        grid=(batch, n_tiles),
        in_specs=[pl.BlockSpec((1, TM * p, LANE), lambda b, i: (b, i + skip_tiles, 0)),
                  pl.BlockSpec((1, TM, d), lambda b, i: (b, i + skip_tiles, 0)),
                  pl.BlockSpec((1, 6, d), lambda b, i: (jnp.where(i + skip_tiles < n_ctx_tiles, batch, b), 0, 0)),
                  pl.BlockSpec((4, d), lambda b, i: (0, 0))],
        out_specs=pl.BlockSpec((1, TM, d), lambda b, i: (b, i, 0)),
        out_shape=jax.ShapeDtypeStruct((batch, n_tiles * TM, d), F32),
        compiler_params=_cp(("arbitrary", "arbitrary")),
        name="post_ffn",
    )(moe_s, x1, mod, g)


def _rope_tables(ctx, seq):
    half = HEAD_DIM // 2
    inv = ROPE_BASE ** (-np.arange(0, half, 2, dtype=np.float32) / half)
    t = np.arange(seq)
    ang_r = (t // GRID_W).astype(np.float32)[:, None] * inv[None, :]
    ang_c = (t % GRID_W).astype(np.float32)[:, None] * inv[None, :]
    ang = jnp.asarray(np.concatenate([ang_r, ang_r, ang_c, ang_c], axis=1))
    sign = np.tile(np.concatenate([-np.ones(16), np.ones(16)]), 2).astype(np.float32)
    cos = jnp.concatenate([jnp.ones((ctx, HEAD_DIM), F32), jnp.cos(ang)], axis=0)
    sin = jnp.concatenate([jnp.zeros((ctx, HEAD_DIM), F32), jnp.sin(ang) * sign[None, :]], axis=0)
    return jnp.tile(cos, (1, LANE // HEAD_DIM)), jnp.tile(sin, (1, LANE // HEAD_DIM)), cos.T, sin.T


def _moe(h2s, logits, x1, mod, g, layer, w_gate, w_up, w_down, ctx, n_ctx_tiles, skip_tiles):
    batch, tt, d = x1.shape
    n_exp = w_gate.shape[1]
    seq = tt - ctx
    (idx_ctx, gate_ctx), (idx_lat, gate_lat) = _route(
        logits, n_exp, [(0, ctx, EC_CAPACITY * ctx // n_exp), (ctx, seq, EC_CAPACITY * seq // n_exp)])
    idx = jnp.concatenate([idx_ctx, idx_lat + ctx], axis=2)[:, :, None, :]
    gates = jnp.transpose(jnp.concatenate([gate_ctx, gate_lat], axis=2), (1, 0, 2))[..., None]
    xe = _gather(idx, h2s, tt, n_exp)
    ye = _ffn(xe, gates, w_gate, w_up, w_down, layer)
    moe_s = _combine(idx, ye, tt)
    return _final(moe_s, x1, mod, g, n_ctx_tiles, skip_tiles)


def kernel(x, c, ctx, c_ctx, ada_w, ada_b, norm_g, na_w_qkv, na_w_out, na_rel_bias, ml_w_in, ml_b_gate,
           ml_norm_w, ml_w_out, moe_w_router, moe_w_gate, moe_w_up, moe_w_down):
    batch, seq, d = x.shape
    n_ctx = ctx.shape[1]
    depth = ada_w.shape[0]
    rows = seq // GRID_W
    n_exp = moe_w_router.shape[-1]
    ml_heads = ml_norm_w.shape[-1] // ML_V_DIM
    assert n_ctx % TM == 0 and seq % TM == 0 and n_ctx % ML_CHUNK == 0 and seq % ML_CHUNK == 0
    assert rows >= K_ROWS and d % (2 * HEAD_DIM) == 0 and n_exp <= LANE and batch < 16
    n_ct = n_ctx // TM

    xa = (ctx, x)
    c_all = jnp.zeros((16, d), F32).at[:batch].set(c).at[batch].set(c_ctx)
    mod = _ada(c_all, ada_w, ada_b).reshape(depth, 16, 6, d)
    rope = _rope_tables(n_ctx, seq)

    for l in range(depth):
        last = l == depth - 1
        jx = l // 2
        g = norm_g[l]
        wr = jnp.zeros((d, LANE), F32).at[:, :n_exp].set(moe_w_router[l])
        wr_hi = wr.astype(BF16)
        wr2 = jnp.stack([wr_hi, (wr - wr_hi.astype(F32)).astype(BF16)])
        if l % 2 == 0:
            scale = HEAD_DIM ** -0.5
            w_qkv = na_w_qkv[jx]
            wk = w_qkv[:, d:2 * d].astype(BF16)
            wt = jnp.concatenate([w_qkv[:, :d] * (scale * LOG2E), w_qkv[:, 2 * d:]], axis=1).T.astype(BF16)
            k, qt, vt = _proj_na(xa, mod[l], g, wk, wt, n_ct)
            o = _na(k, qt, vt, _na_bias_table(na_rel_bias[jx]), n_ctx, rows)
            x1, h2s, logits = _post((o,), na_w_out[jx].astype(BF16), xa, mod[l], g, wr2, n_ct)
        else:
            qk_w = ml_heads * HEAD_DIM
            v_w = ml_heads * ML_V_DIM
            main_w = 2 * qk_w + 2 * v_w
            w_in = ml_w_in[jx]
            ng = 2 * ml_heads
            pad = jnp.zeros((d, LANE - ng), F32)
            w_gates = [w_in[:, main_w:main_w + ng], pad, w_in[:, main_w + ng:], pad]
            wk = jnp.concatenate([w_in[:, qk_w:2 * qk_w] * (HEAD_DIM ** -0.5)] + w_gates, axis=1).astype(BF16)
            wt = jnp.concatenate([w_in[:, :qk_w], w_in[:, 2 * qk_w:main_w]] + w_gates, axis=1).T.astype(BF16)
            zpad = jnp.zeros((LANE - ng,), F32)
            bg = jnp.concatenate([ml_b_gate[jx][:ng], zpad, ml_b_gate[jx][ng:], zpad])[None, :]
            bgt = jnp.broadcast_to(bg.T, (2 * LANE, TM))
            k, gates, qt, vt, ot, gates_t = _proj_ml(xa, mod[l], g, wk, wt, bg, bgt, rope, n_ct, qk_w, v_w)
            hf = _scan(qt, k, vt, gates, gates_t, ml_heads, n_ctx, reverse=False)
            hb = _scan(qt, k, vt, gates, gates_t, ml_heads, n_ctx, reverse=True)
            nw = jnp.broadcast_to(ml_norm_w[jx][:, None], (v_w, LANE))
            x1, h2s, logits = _post((hf, hb, ot, nw), ml_w_out[jx].astype(BF16), xa, mod[l], g,
                                    wr2, n_ct, ml_heads=ml_heads)
        xa = _moe(h2s, logits, x1, mod[l], g, l, moe_w_gate, moe_w_up, moe_w_down, n_ctx, n_ct,
                  n_ct if last else 0)
    return xa
```

```python
import functools

import numpy as np
import jax
import jax.numpy as jnp
from jax import lax
from jax.experimental import pallas as pl
from jax.experimental.pallas import tpu as pltpu

F32 = jnp.float32
BF16 = jnp.bfloat16

LANE = 128
SUBLANE = 8
VMEM_LIMIT = 56 * 1024 * 1024

EPS = 1e-6
NEG = -1e30
LOG2E = 1.4426950408889634
GRID_W = 64
HEAD_DIM = 64
ML_V_DIM = 128
WIN_ROWS = 8
WIN_COLS = 16
ROPE_BASE = 10000.0
EC_CAPACITY = 2

TM = 256
Q_ROWS = TM // GRID_W
K_ROWS = Q_ROWS + WIN_ROWS
ML_CHUNK = 256
NA_PAIRS = 4
RANK_RADIX = 32
RANK_HI = 16
TOKEN_SPLIT = 64


def _cp(sem):
    return pltpu.CompilerParams(dimension_semantics=sem, vmem_limit_bytes=VMEM_LIMIT)


def _dot(a, b):
    return jnp.dot(a, b, preferred_element_type=F32)


def _dot_nt(a, b):
    return lax.dot_general(a, b, (((1,), (1,)), ((), ())), preferred_element_type=F32)


def _dot_tn(a, b):
    return lax.dot_general(a, b, (((0,), (0,)), ((), ())), preferred_element_type=F32)


def _split3(x):
    x1 = x.astype(BF16)
    r1 = x - x1.astype(F32)
    x2 = r1.astype(BF16)
    x3 = (r1 - x2.astype(F32)).astype(BF16)
    return x1, x2, x3


def _rms(x, g):
    return x * lax.rsqrt(jnp.mean(x * x, axis=-1, keepdims=True) + EPS) * g


def _silu(x):
    return x * jax.nn.sigmoid(x)


def _ada_kernel(c_ref, w_ref, b_ref, o_ref):
    s = _silu(c_ref[...]).astype(BF16)
    o_ref[0] = _dot(s, w_ref[0].astype(BF16)) + b_ref[0]


def _ada(c_all, ada_w, ada_b):
    depth, d, n = ada_w.shape
    rows = c_all.shape[0]
    tn = 1024 if n % 1024 == 0 else n
    return pl.pallas_call(
        _ada_kernel,
        grid=(depth, n // tn),
        in_specs=[pl.BlockSpec((rows, d), lambda l, j: (0, 0)),
                  pl.BlockSpec((1, d, tn), lambda l, j: (l, 0, j)),
                  pl.BlockSpec((1, 1, tn), lambda l, j: (l, 0, j))],
        out_specs=pl.BlockSpec((1, rows, tn), lambda l, j: (l, 0, j)),
        out_shape=jax.ShapeDtypeStruct((depth, rows, n), F32),
        compiler_params=_cp(("arbitrary", "arbitrary")),
        name="ada_mod",
    )(c_all, ada_w, ada_b.reshape(depth, 1, n))


def _mod_spec(batch, n_ctx_tiles, d):
    return pl.BlockSpec((1, 6, d), lambda b, i: (jnp.where(i < n_ctx_tiles, batch, b), 0, 0))


def _stream_specs(xs, n_ct):
    if isinstance(xs, tuple):
        d = xs[0].shape[-1]
        return list(xs), [pl.BlockSpec((1, TM, d), lambda b, i: (b, jnp.minimum(i, n_ct - 1), 0)),
                          pl.BlockSpec((1, TM, d), lambda b, i: (b, jnp.maximum(i - n_ct, 0), 0))]
    return [xs], [pl.BlockSpec((1, TM, xs.shape[-1]), lambda b, i: (b, i, 0))]


def _stream_tile(x_refs, n_ct):
    if len(x_refs) == 1:
        return x_refs[0][0]
    return jnp.where(pl.program_id(1) < n_ct, x_refs[0][0], x_refs[1][0])


def _proj_na_kernel(*refs, n_x, n_ct):
    mod_ref, g_ref, wk_ref, wt_ref, k_ref, qt_ref, vt_ref = refs[n_x:]
    m = mod_ref[0]
    h = (_rms(_stream_tile(refs[:n_x], n_ct), g_ref[0:1, :]) * (1.0 + m[1:2, :]) + m[0:1, :]).astype(BF16)
    d = h.shape[1]
    k_ref[0] = _dot(h, wk_ref[...]).astype(BF16)
    rt = _dot_nt(wt_ref[...], h)
    qt_ref[0, 0] = rt[:d].astype(BF16)
    vt_ref[0, 0] = rt[d:].astype(BF16)


def _proj_na(xs, mod, g, wk, wt, n_ctx_tiles):
    x_args, x_specs = _stream_specs(xs, n_ctx_tiles)
    batch, _, d = x_args[0].shape
    tt = sum(a.shape[1] for a in x_args)
    n_tiles = tt // TM
    const = lambda a: pl.BlockSpec(a.shape, lambda b, i: (0,) * a.ndim)
    feat = pl.BlockSpec((1, 1, d, TM), lambda b, i: (b, i, 0, 0))
    return pl.pallas_call(
        functools.partial(_proj_na_kernel, n_x=len(x_args), n_ct=n_ctx_tiles),
        grid=(batch, n_tiles),
        in_specs=x_specs + [_mod_spec(batch, n_ctx_tiles, d), const(g), const(wk), const(wt)],
        out_specs=[pl.BlockSpec((1, TM, d), lambda b, i: (b, i, 0)), feat, feat],
        out_shape=[jax.ShapeDtypeStruct((batch, tt, d), BF16),
                   jax.ShapeDtypeStruct((batch, n_tiles, d, TM), BF16),
                   jax.ShapeDtypeStruct((batch, n_tiles, d, TM), BF16)],
        compiler_params=_cp(("arbitrary", "arbitrary")),
        name="proj_na",
    )(*x_args, mod, g, wk, wt)


def _log_sigmoid(f):
    return jnp.minimum(f, 0.0) - jnp.log1p(jnp.exp(-jnp.abs(f)))


def _proj_ml_kernel(x_ref, mod_ref, g_ref, wk_ref, wt_ref, bg_ref, bgt_ref, cos_ref, sin_ref, cost_ref, sint_ref,
                    k_ref, gate_ref, qt_ref, vt_ref, ot_ref, gatet_ref, *, qk_w, v_w):
    m = mod_ref[0]
    h = (_rms(x_ref[0], g_ref[0:1, :]) * (1.0 + m[1:2, :]) + m[0:1, :]).astype(BF16)
    r = _dot(h, wk_ref[...])
    k = r[:, :qk_w]
    reps = qk_w // LANE
    lane = lax.broadcasted_iota(jnp.int32, k.shape, 1)
    partner = jnp.where(lane % 32 < 16, pltpu.roll(k, qk_w - 16, 1), pltpu.roll(k, 16, 1))
    k = k * jnp.concatenate([cos_ref[...]] * reps, axis=1) + partner * jnp.concatenate([sin_ref[...]] * reps, axis=1)
    k_ref[0] = k.astype(BF16)
    gr = r[:, qk_w:] + bg_ref[...]
    gate_ref[0, :, :LANE] = gr[:, :LANE]
    gate_ref[0, :, LANE:] = _log_sigmoid(gr[:, LANE:])

    rt = _dot_nt(wt_ref[...], h)
    q = rt[:qk_w]
    reps = qk_w // HEAD_DIM
    row = lax.broadcasted_iota(jnp.int32, q.shape, 0)
    partner = jnp.where(row % 32 < 16, pltpu.roll(q, qk_w - 16, 0), pltpu.roll(q, 16, 0))
    q = q * jnp.concatenate([cost_ref[...]] * reps, axis=0) + partner * jnp.concatenate([sint_ref[...]] * reps, axis=0)
    qt_ref[0] = q.astype(BF16)
    vt_ref[0] = rt[qk_w:qk_w + v_w].astype(BF16)
    ot_ref[0] = rt[qk_w + v_w:qk_w + 2 * v_w].astype(BF16)
    gt = rt[qk_w + 2 * v_w:] + bgt_ref[...]
    gatet_ref[0, :LANE, :] = gt[:LANE]
    gatet_ref[0, LANE:, :] = _log_sigmoid(gt[LANE:])


def _proj_ml(xa, mod, g, wk, wt, bg, bgt, tables, n_ctx_tiles, qk_w, v_w):
    batch, tt, d = xa.shape
    cos_t, sin_t, cos_tt, sin_tt = tables
    kern = functools.partial(_proj_ml_kernel, qk_w=qk_w, v_w=v_w)
    const = lambda a: pl.BlockSpec(a.shape, lambda b, i: (0,) * a.ndim)
    tok = lambda w: pl.BlockSpec((1, TM, w), lambda b, i: (b, i, 0))
    feat = lambda w: pl.BlockSpec((1, w, TM), lambda b, i: (b, 0, i))
    return pl.pallas_call(
        kern,
        grid=(batch, tt // TM),
        in_specs=[tok(d), _mod_spec(batch, n_ctx_tiles, d), const(g), const(wk), const(wt), const(bg), const(bgt),
                  pl.BlockSpec((TM, LANE), lambda b, i: (i, 0)),
                  pl.BlockSpec((TM, LANE), lambda b, i: (i, 0)),
                  pl.BlockSpec((HEAD_DIM, TM), lambda b, i: (0, i)),
                  pl.BlockSpec((HEAD_DIM, TM), lambda b, i: (0, i))],
        out_specs=[tok(qk_w), tok(2 * LANE), feat(qk_w), feat(v_w), feat(v_w), feat(2 * LANE)],
        out_shape=[jax.ShapeDtypeStruct((batch, tt, qk_w), BF16),
                   jax.ShapeDtypeStruct((batch, tt, 2 * LANE), F32),
                   jax.ShapeDtypeStruct((batch, qk_w, tt), BF16),
                   jax.ShapeDtypeStruct((batch, v_w, tt), BF16),
                   jax.ShapeDtypeStruct((batch, v_w, tt), BF16),
                   jax.ShapeDtypeStruct((batch, 2 * LANE, tt), F32)],
        compiler_params=_cp(("arbitrary", "arbitrary")),
        name="proj_ml",
    )(xa, mod, g, wk, wt, bg, bgt, cos_t, sin_t, cos_tt, sin_tt)


def _na_kernel(qt_ref, k_ref, vt_ref, tab_ref, sel_ref, o_ref, *, ctx, rows, pairs):
    step = pl.program_id(2)
    n_ct = ctx // TM
    win_tiles = K_ROWS * GRID_W // TM
    ones_rows = jnp.where(lax.broadcasted_iota(jnp.int32, (16, TM), 0) == 0, 1.0, 0.0).astype(BF16)

    def attend(chunks):
        m, acc = None, None
        for s, v_tile in chunks:
            m_new = s.max(axis=0, keepdims=True)
            if m is not None:
                m_new = jnp.maximum(m, m_new)
            p = jnp.exp2(s - m_new).astype(BF16)
            part = _dot(jnp.concatenate([v_tile, ones_rows], axis=0), p)
            acc = part if acc is None else acc * jnp.exp2(m - m_new) + part
            m = m_new
        return acc[:HEAD_DIM] / acc[HEAD_DIM:HEAD_DIM + 1]

    def ctx_chunks(hh, q_t):
        sl = slice(hh * HEAD_DIM, (hh + 1) * HEAD_DIM)
        return [(_dot(k_ref[0, t * TM:(t + 1) * TM, sl], q_t), vt_ref[0, t, sl, :]) for t in range(n_ct)]

    @pl.when(step == 0)
    def _():
        for hd in range(2 * pairs):
            sl = slice(hd * HEAD_DIM, (hd + 1) * HEAD_DIM)
            o_ref[0, 0, sl, :] = attend(ctx_chunks(hd, qt_ref[0, 0, sl, :])).astype(BF16)

    @pl.when(step > 0)
    def _():
        rb = step - 1
        start = jnp.clip(Q_ROWS * rb - WIN_ROWS // 2, 0, rows - K_ROWS)
        t0 = n_ct + start // Q_ROWS
        ks = pl.multiple_of(t0 * TM, TM)
        kr = start + lax.broadcasted_iota(jnp.int32, (16, TM), 0)
        r = Q_ROWS * rb + lax.broadcasted_iota(jnp.int32, (16, TM), 1) // GRID_W
        r0 = jnp.clip(r - WIN_ROWS // 2, 0, rows - WIN_ROWS)
        pen = jnp.where((kr >= r0) & (kr < r0 + WIN_ROWS), 0.0, NEG).astype(BF16)
        pen = jnp.concatenate([pen, jnp.zeros((HEAD_DIM - 16, TM), BF16)], axis=0)
        first_half = lax.broadcasted_iota(jnp.int32, (win_tiles * TM, LANE), 1) < HEAD_DIM

        for hd in range(2 * pairs):
            hh = hd % 2
            sl = slice(hd * HEAD_DIM, (hd + 1) * HEAD_DIM)
            q_t = qt_ref[0, 0, sl, :]
            k_both = k_ref[0, pl.ds(ks, win_tiles * TM), (hd // 2) * LANE:(hd // 2 + 1) * LANE]
            if hh == 0:
                k_aug = jnp.where(first_half, k_both, sel_ref[...])
                q_aug = jnp.concatenate([q_t, pen], axis=0)
            else:
                k_aug = jnp.where(first_half, sel_ref[...], k_both)
                q_aug = jnp.concatenate([pen, q_t], axis=0)
            chunks = []
            for i in range(win_tiles):
                blocks = []
                for krl in range(i * Q_ROWS, (i + 1) * Q_ROWS):
                    tiles = []
                    for u in range(Q_ROWS // 2):
                        dr_e = start + krl - (Q_ROWS * rb + 2 * u) + WIN_ROWS - 1
                        tiles.append(tab_ref[hd, jnp.clip(dr_e, -1, 2 * WIN_ROWS - 1) + 1])
                    blocks.append(jnp.concatenate(tiles, axis=1))
                s_i = _dot(k_aug[i * TM:(i + 1) * TM], q_aug) + jnp.concatenate(blocks, axis=0)
                chunks.append((s_i, vt_ref[0, t0 + i, sl, :]))
            o_ref[0, 0, sl, :] = attend(ctx_chunks(hd, q_t) + chunks).astype(BF16)


def _na_bias_table(rel_bias):
    n_dr, n_dc = 2 * WIN_ROWS - 1, 2 * WIN_COLS - 1
    col = np.arange(GRID_W)
    c0 = np.clip(col - WIN_COLS // 2, 0, GRID_W - WIN_COLS)
    col_ok = (col[:, None] >= c0[None, :]) & (col[:, None] < c0[None, :] + WIN_COLS)
    dc = np.clip(col[:, None] - col[None, :] + WIN_COLS - 1, 0, n_dc - 1)
    oh = ((np.arange(n_dc)[:, None, None] == dc[None]) & col_ok[None]).astype(np.float32)
    cb = jnp.einsum('hrc,ckq->hrkq', rel_bias.astype(F32) * LOG2E, oh, precision=lax.Precision.HIGHEST)
    cb = jnp.where(col_ok[None, None], cb, NEG)
    neg = jnp.full((cb.shape[0], 2, GRID_W, GRID_W), NEG, F32)
    ext = jnp.concatenate([neg, cb, neg[:, :1]], axis=1)
    n_tiles = 2 * WIN_ROWS + 1
    return jnp.concatenate([ext[:, 1:1 + n_tiles], ext[:, 0:n_tiles]], axis=-1)


def _na(k, qt, vt, bias_tab, ctx, rows):
    batch, n_tiles, d, _ = qt.shape
    tt = k.shape[1]
    pairs = min(NA_PAIRS, d // LANE)
    gw = pairs * LANE
    hp = d // gw
    n_rb = rows // Q_ROWS
    n_ct = ctx // TM
    assert n_ct == 1
    q_tile = lambda h, b, s: (b, jnp.where(s == 0, 0, s - 1 + n_ct), h, 0)
    key_row = np.arange(K_ROWS * GRID_W)[:, None] // GRID_W
    key_row_sel = jnp.asarray(key_row == (np.arange(LANE)[None, :] % HEAD_DIM), BF16)
    kern = functools.partial(_na_kernel, ctx=ctx, rows=rows, pairs=pairs)
    return pl.pallas_call(
        kern,
        grid=(hp, batch, n_rb + 1),
        in_specs=[pl.BlockSpec((1, 1, gw, TM), q_tile),
                  pl.BlockSpec((1, tt, gw), lambda h, b, s: (b, 0, h)),
                  pl.BlockSpec((1, n_tiles, gw, TM), lambda h, b, s: (b, 0, h, 0)),
                  pl.BlockSpec((2 * pairs,) + bias_tab.shape[1:], lambda h, b, s: (h, 0, 0, 0)),
                  pl.BlockSpec(key_row_sel.shape, lambda h, b, s: (0, 0))],
        out_specs=pl.BlockSpec((1, 1, gw, TM), q_tile),
        out_shape=jax.ShapeDtypeStruct((batch, n_tiles, d, TM), BF16),
        compiler_params=_cp(("arbitrary", "arbitrary", "arbitrary")),
        name="na_attention",
    )(qt, k, vt, bias_tab, key_row_sel)


def _scan_kernel(qt_ref, k_ref, vt_ref, gate_ref, gatet_ref, o_ref, c_ref, m_ref, *, heads, reverse):
    L = ML_CHUNK
    j = pl.program_id(1)

    @pl.when(j == 0)
    def _():
        c_ref[...] = jnp.zeros_like(c_ref)
        m_ref[...] = jnp.full_like(m_ref, NEG)

    r_i = lax.broadcasted_iota(jnp.int32, (L, L), 0)
    c_i = lax.broadcasted_iota(jnp.int32, (L, L), 1)
    seen = (r_i >= c_i) if reverse else (r_i <= c_i)
    tri_row = jnp.where(seen, 1.0, 0.0).astype(BF16)
    tri_col = jnp.where((r_i <= c_i) if reverse else (r_i >= c_i), 1.0, 0.0).astype(BF16)

    f1, f2, f3 = _split3(gate_ref[0, :, LANE:])
    b_col = _dot(tri_col, f1) + _dot(tri_col, f2) + _dot(tri_col, f3)
    u_col = gate_ref[0, :, :LANE] - b_col
    f1, f2, f3 = _split3(gatet_ref[0, LANE:, :])
    b_row = _dot(f1, tri_row) + _dot(f2, tri_row) + _dot(f3, tri_row)
    i_row = gatet_ref[0, :LANE, :]
    last = 0 if reverse else L - 1
    ones_rows = jnp.where(lax.broadcasted_iota(jnp.int32, (ML_V_DIM, L), 0) == 0, 1.0, 0.0).astype(BF16)

    for h in range(heads):
        c = (heads if reverse else 0) + h
        br = b_row[c:c + 1, :]
        ir = i_row[c:c + 1, :]
        mh = m_ref[h]
        mh_row = jnp.concatenate([mh] * (L // LANE), axis=1)
        a_t = jnp.where(seen, jnp.broadcast_to(u_col[:, c:c + 1], (L, L)), NEG)
        mu = jnp.maximum(a_t.max(axis=0, keepdims=True), mh_row)
        p_t = jnp.exp(a_t - mu)
        w_inter = jnp.exp(mh_row - mu)
        kh = k_ref[0, :, h * HEAD_DIM:(h + 1) * HEAD_DIM]
        state = c_ref[h]
        q_t = qt_ref[0, h * HEAD_DIM:(h + 1) * HEAD_DIM, :]
        r1 = _dot(jnp.concatenate([kh, state.astype(BF16)], axis=0), q_t)
        sp = (r1[:L] * p_t).astype(BF16)
        v_aug = jnp.concatenate([vt_ref[0, h * ML_V_DIM:(h + 1) * ML_V_DIM, :], ones_rows], axis=0)
        rt = _dot(v_aug, sp) + w_inter * r1[L:]
        den = rt[ML_V_DIM:ML_V_DIM + 1]
        h_out = rt[:ML_V_DIM] / jnp.maximum(jnp.abs(den), jnp.exp(-(br + mu)))
        o_ref[0, h * ML_V_DIM:(h + 1) * ML_V_DIM, :] = h_out.astype(BF16)
        be = br[:, last:last + 1]
        m1 = mh[:, 0:1]
        g_row = be - br + ir
        m_new = jnp.maximum(be + m1, g_row.max(axis=1, keepdims=True))
        w_k = jnp.exp(g_row - m_new)
        decay = jnp.exp(be + m1 - m_new)
        c_ref[h] = decay * state + _dot((v_aug.astype(F32) * w_k).astype(BF16), kh)
        m_ref[h] = jnp.broadcast_to(m_new, (1, LANE))


def _scan(qt, k, vt, gates, gates_t, heads, ctx, reverse):
    batch, tt, qk_w = k.shape
    L = ML_CHUNK
    nc = tt // L
    ncc = ctx // L
    v_w = heads * ML_V_DIM

    def chunk(j):
        if not reverse:
            return j
        return jnp.where(j < ncc, ncc - 1 - j, nc - 1 - (j - ncc))

    kern = functools.partial(_scan_kernel, heads=heads, reverse=reverse)
    feat = lambda w: pl.BlockSpec((1, w, L), lambda b, j: (b, 0, chunk(j)))
    tok = lambda w: pl.BlockSpec((1, L, w), lambda b, j: (b, chunk(j), 0))
    return pl.pallas_call(
        kern,
        grid=(batch, nc),
        in_specs=[feat(qk_w), tok(qk_w), feat(v_w), tok(2 * LANE), feat(2 * LANE)],
        out_specs=feat(v_w),
        out_shape=jax.ShapeDtypeStruct((batch, v_w, tt), BF16),
        scratch_shapes=[pltpu.VMEM((heads, 2 * ML_V_DIM, HEAD_DIM), F32),
                        pltpu.VMEM((heads, 1, LANE), F32)],
        compiler_params=_cp(("arbitrary", "arbitrary")),
        name="mlstm_scan_bwd" if reverse else "mlstm_scan_fwd",
    )(qt, k, vt, gates, gates_t)


def _post_common(y, xa, mod_ref, g_ref, wr_ref, x1_ref, h2s_ref, lg_ref):
    m = mod_ref[0]
    x1 = xa + m[2:3, :] * _rms(y, g_ref[1:2, :])
    x1_ref[0] = x1
    h2 = _rms(x1, g_ref[2:3, :]) * (1.0 + m[4:5, :]) + m[3:4, :]
    d = h2.shape[1]
    p = d // LANE
    for jj in range(p):
        h2s_ref[0, pl.ds(jj, TM, stride=p), :] = h2[:, jj * LANE:(jj + 1) * LANE]
    h_hi = h2.astype(BF16)
    h_lo = (h2 - h_hi.astype(F32)).astype(BF16)
    lg_ref[0] = _dot(h_hi, wr_ref[0]) + _dot(h_hi, wr_ref[1]) + _dot(h_lo, wr_ref[0])


def _post_na_kernel(ot_ref, w_ref, *refs, n_x, n_ct):
    y = _dot_tn(ot_ref[0, 0], w_ref[...])
    _post_common(y, _stream_tile(refs[:n_x], n_ct), *refs[n_x:])


def _post_ml_kernel(hf_ref, hb_ref, og_ref, nw_ref, w_ref, xa_ref, mod_ref, g_ref, wr_ref,
                    x1_ref, h2s_ref, lg_ref, *, heads):
    hs = hf_ref[0].astype(F32) + hb_ref[0].astype(F32)
    parts = []
    for h in range(heads):
        t = hs[h * ML_V_DIM:(h + 1) * ML_V_DIM, :]
        parts.append(t * lax.rsqrt(jnp.mean(t * t, axis=0, keepdims=True) + EPS))
    nw = jnp.concatenate([nw_ref[...]] * (TM // LANE), axis=1)
    hn = jnp.concatenate(parts, axis=0) * nw
    y_in = (hn * jax.nn.sigmoid(og_ref[0].astype(F32))).astype(BF16)
    y = _dot_tn(y_in, w_ref[...])
    _post_common(y, xa_ref[0], mod_ref, g_ref, wr_ref, x1_ref, h2s_ref, lg_ref)


def _post(mixer_inputs, w_out, xs, mod, g, wr, n_ctx_tiles, ml_heads=None):
    x_args, x_specs = _stream_specs(xs, n_ctx_tiles)
    batch, _, d = x_args[0].shape
    tt = sum(a.shape[1] for a in x_args)
    p = d // LANE
    tile = lambda w: pl.BlockSpec((1, TM, w), lambda b, i: (b, i, 0))
    full2 = lambda a: pl.BlockSpec(a.shape, lambda b, i: (0,) * a.ndim)
    if ml_heads is None:
        (o,) = mixer_inputs
        kern = functools.partial(_post_na_kernel, n_x=len(x_args), n_ct=n_ctx_tiles)
        head_specs, head_args = [pl.BlockSpec((1, 1, d, TM), lambda b, i: (b, i, 0, 0))], [o]
    else:
        assert len(x_args) == 1
        hf, hb, og, nw = mixer_inputs
        v_w = ml_heads * ML_V_DIM
        kern = functools.partial(_post_ml_kernel, heads=ml_heads)
        feat = pl.BlockSpec((1, v_w, TM), lambda b, i: (b, 0, i))
        head_specs = [feat, feat, feat, full2(nw)]
        head_args = [hf, hb, og, nw]
    return pl.pallas_call(
        kern,
        grid=(batch, tt // TM),
        in_specs=head_specs + [full2(w_out)] + x_specs + [_mod_spec(batch, n_ctx_tiles, d), full2(g), full2(wr)],
        out_specs=[tile(d),
                   pl.BlockSpec((1, TM * p, LANE), lambda b, i: (b, i, 0)),
                   tile(LANE)],
        out_shape=[jax.ShapeDtypeStruct((batch, tt, d), F32),
                   jax.ShapeDtypeStruct((batch, tt * p, LANE), F32),
                   jax.ShapeDtypeStruct((batch, tt, LANE), F32)],
        compiler_params=_cp(("arbitrary", "arbitrary")),
        name="post_mixer",
    )(*head_args, w_out, *x_args, mod, g, wr)


def _route_kernel(lg_ref, *refs, n_exp, sets):
    for si, (off, t, cap) in enumerate(sets):
        _route_set(lg_ref[0, off:off + t, :], refs[2 * si], refs[2 * si + 1], refs[-1], n_exp, cap)


def _route_set(lg, idx_ref, gate_ref, pos_ref, n_exp, cap):
    t = lg.shape[0]
    n_tiles = t // LANE
    lt = lg.T[:n_exp, :]
    e = jnp.exp(lt - lt.max(axis=0, keepdims=True))
    aff = e / e.sum(axis=0, keepdims=True)
    def count_ge(v):
        return jnp.where(aff >= v, 1.0, 0.0).sum(axis=1, keepdims=True)

    def bisect(i, cur):
        cand = cur | jnp.left_shift(jnp.int32(1), 30 - i)
        return jnp.where(count_ge(pltpu.bitcast(cand, F32)) >= cap, cand, cur)

    v_bits = lax.fori_loop(0, 31, bisect, jnp.zeros((n_exp, 1), jnp.int32))

    def refine(_, hi):
        pivot = jnp.where(aff < hi, aff, -1.0).max(axis=1, keepdims=True)
        return jnp.where(count_ge(pivot) >= cap, hi, pivot)

    min_normal_bits = 0x00800000
    hi = lax.fori_loop(0, 3, refine, pltpu.bitcast(jnp.maximum(v_bits + 1, min_normal_bits), F32))
    thr = jnp.where(aff < hi, aff, -1.0).max(axis=1, keepdims=True)
    gt = aff > thr
    eq = aff == thr
    need = cap - jnp.where(gt, 1.0, 0.0).sum(axis=1, keepdims=True)

    upper = jnp.where(lax.broadcasted_iota(jnp.int32, (LANE, LANE), 0)
                      <= lax.broadcasted_iota(jnp.int32, (LANE, LANE), 1), 1.0, 0.0).astype(BF16)

    def excl_cumsum(mask_f32):
        carry = jnp.zeros((n_exp, 1), F32)
        parts = []
        for jt in range(n_tiles):
            tile = mask_f32[:, jt * LANE:(jt + 1) * LANE]
            inc = _dot(tile.astype(BF16), upper)
            parts.append(inc - tile + carry)
            carry = carry + inc[:, LANE - 1:LANE]
        return jnp.concatenate(parts, axis=1)

    eq_f = jnp.where(eq, 1.0, 0.0)
    sel = gt | (eq & (excl_cumsum(eq_f) < need))
    sel_f = jnp.where(sel, 1.0, 0.0)
    pos_ref[:, :t] = jnp.where(sel, excl_cumsum(sel_f), -1.0)

    hi_iota = lax.broadcasted_iota(jnp.int32, (RANK_HI, t), 0).astype(F32)
    lo_iota = lax.broadcasted_iota(jnp.int32, (RANK_RADIX, t), 0).astype(F32)
    t_idx = lax.broadcasted_iota(jnp.int32, (1, t), 1)
    t_hi = (t_idx // TOKEN_SPLIT).astype(F32)
    t_lo = (t_idx % TOKEN_SPLIT).astype(F32)
    for ex in range(n_exp):
        pos = pos_ref[ex:ex + 1, :t]
        hi = jnp.floor(pos * (1.0 / RANK_RADIX))
        lo = pos - hi * RANK_RADIX
        hit = hi_iota == hi
        parts = [t_hi, t_lo] + [a.astype(F32) for a in _split3(aff[ex:ex + 1, :])]
        lhs = jnp.concatenate([jnp.where(hit, v, 0.0) for v in parts], axis=0).astype(BF16)
        rhs = jnp.where(lo_iota == lo, 1.0, 0.0).astype(BF16)
        r = _dot_nt(lhs, rhs)
        idx_ref[0, ex] = (r[:RANK_HI] * TOKEN_SPLIT + r[RANK_HI:2 * RANK_HI]).astype(jnp.int32)
        gate_ref[0, ex] = r[2 * RANK_HI:3 * RANK_HI] + r[3 * RANK_HI:4 * RANK_HI] + r[4 * RANK_HI:]


def _route(logits, n_exp, sets):
    batch, tt, _ = logits.shape
    for _, t, cap in sets:
        assert cap % RANK_RADIX == 0 and cap <= RANK_HI * RANK_RADIX and t <= LANE * TOKEN_SPLIT
    kern = functools.partial(_route_kernel, n_exp=n_exp, sets=tuple(sets))
    out_spec = pl.BlockSpec((1, n_exp, RANK_HI, RANK_RADIX), lambda b: (b, 0, 0, 0))
    outs = pl.pallas_call(
        kern,
        grid=(batch,),
        in_specs=[pl.BlockSpec((1, tt, LANE), lambda b: (b, 0, 0))],
        out_specs=[out_spec, out_spec] * len(sets),
        out_shape=[jax.ShapeDtypeStruct((batch, n_exp, RANK_HI, RANK_RADIX), jnp.int32),
                   jax.ShapeDtypeStruct((batch, n_exp, RANK_HI, RANK_RADIX), F32)] * len(sets),
        scratch_shapes=[pltpu.VMEM((n_exp, max(t for _, t, _ in sets)), F32)],
        compiler_params=_cp(("arbitrary",)),
        name="route",
    )(logits)
    flat = lambda a, cap: a.reshape(batch, n_exp, RANK_HI * RANK_RADIX)[:, :, :cap]
    return [(flat(outs[2 * i], cap), flat(outs[2 * i + 1], cap)) for i, (_, _, cap) in enumerate(sets)]


def _gather_kernel(idx_ref, src_ref, xe_ref, tile_ref, *, slots, p, stride):
    unroll = 8

    def body(c, _):
        for u in range(unroll):
            r = c * unroll + u
            t = idx_ref[0, 0, 0, r]
            tile_ref[pl.ds(r, p, stride=stride), :] = src_ref[0, pl.ds(pl.multiple_of(t * p, p), p), :]
        return 0

    lax.fori_loop(0, slots // unroll, body, 0)
    xe_ref[0, 0] = jnp.concatenate([tile_ref[pl.ds(jj * stride, slots), :] for jj in range(p)], axis=1).astype(BF16)


def _gather(idx, h2s, tt, n_exp):
    batch, _, _, slots = idx.shape
    p = h2s.shape[1] // tt
    d = p * LANE
    stride = slots + SUBLANE
    kern = functools.partial(_gather_kernel, slots=slots, p=p, stride=stride)
    return pl.pallas_call(
        kern,
        grid=(batch, n_exp),
        in_specs=[pl.BlockSpec((1, 1, 1, slots), lambda b, e: (b, e, 0, 0), memory_space=pltpu.SMEM),
                  pl.BlockSpec((1, tt * p, LANE), lambda b, e: (b, 0, 0))],
        out_specs=pl.BlockSpec((1, 1, slots, d), lambda b, e: (e, b, 0, 0)),
        out_shape=jax.ShapeDtypeStruct((n_exp, batch, slots, d), BF16),
        scratch_shapes=[pltpu.VMEM((p * stride, LANE), F32)],
        compiler_params=_cp(("arbitrary", "arbitrary")),
        name="moe_gather",
    )(idx, h2s)


def _ffn_kernel(xe_ref, gate_ref, wg_hbm, wu_hbm, wd_hbm, ye_ref, wg_st, wu_st, wd_st, wgb_ref, wub_ref, wdb_ref,
                sem, *, layer, n_exp, f_chunk, cast_rows):
    e, b = pl.program_id(0), pl.program_id(1)
    stage = ((wg_hbm, wg_st, wgb_ref), (wu_hbm, wu_st, wub_ref), (wd_hbm, wd_st, wdb_ref))

    def weight_copies(ex):
        return [pltpu.make_async_copy(hbm.at[layer, ex], st, sem.at[i]) for i, (hbm, st, _) in enumerate(stage)]

    @pl.when((e == 0) & (b == 0))
    def _():
        for cp in weight_copies(0):
            cp.start()

    @pl.when(b == 0)
    def _():
        for cp in weight_copies(e):
            cp.wait()
        for _, st, dst in stage:
            def cast(r, _, st=st, dst=dst):
                rows = pl.ds(pl.multiple_of(r * cast_rows, cast_rows), cast_rows)
                dst[rows, :] = st[rows, :].astype(BF16)
                return 0
            lax.fori_loop(0, st.shape[0] // cast_rows, cast, 0)

        @pl.when(e + 1 < n_exp)
        def _():
            for cp in weight_copies(e + 1):
                cp.start()

    x = xe_ref[0, 0]
    slots, d = x.shape
    f = wgb_ref.shape[1]
    y = jnp.zeros((slots, d), F32)
    for c in range(f // f_chunk):
        cs = slice(c * f_chunk, (c + 1) * f_chunk)
        a = _dot(x, wgb_ref[:, cs])
        u = _dot(x, wub_ref[:, cs])
        y = y + _dot((_silu(a) * u).astype(BF16), wdb_ref[cs, :])
    y = y * gate_ref[0, 0]
    p = d // LANE
    for jj in range(p):
        ye_ref[0, 0, pl.ds(jj, slots, stride=p), :] = y[:, jj * LANE:(jj + 1) * LANE]


def _ffn(xe, gates, wg, wu, wd, layer):
    n_exp, batch, slots, d = xe.shape
    f = wg.shape[3]
    p = d // LANE
    kern = functools.partial(_ffn_kernel, layer=layer, n_exp=n_exp, f_chunk=min(f, 512), cast_rows=min(d, 256))
    hbm = pl.BlockSpec(memory_space=pl.ANY)
    return pl.pallas_call(
        kern,
        grid=(n_exp, batch),
        in_specs=[pl.BlockSpec((1, 1, slots, d), lambda e, b: (e, b, 0, 0)),
                  pl.BlockSpec((1, 1, slots, 1), lambda e, b: (e, b, 0, 0)),
                  hbm, hbm, hbm],
        out_specs=pl.BlockSpec((1, 1, slots * p, LANE), lambda e, b: (e, b, 0, 0)),
        out_shape=jax.ShapeDtypeStruct((n_exp, batch, slots * p, LANE), F32),
        scratch_shapes=[pltpu.VMEM((d, f), F32), pltpu.VMEM((d, f), F32), pltpu.VMEM((f, d), F32),
                        pltpu.VMEM((d, f), BF16), pltpu.VMEM((d, f), BF16), pltpu.VMEM((f, d), BF16),
                        pltpu.SemaphoreType.DMA((3,))],
        compiler_params=_cp(("arbitrary", "arbitrary")),
        name="moe_ffn",
    )(xe, gates, wg, wu, wd)


def _combine_kernel(idx_ref, ye_ref, acc_ref, *, slots, p):
    @pl.when(pl.program_id(1) == 0)
    def _():
        acc_ref[...] = jnp.zeros_like(acc_ref)

    unroll = 4

    def body(c, _):
        rows, vals = [], []
        for u in range(unroll):
            r = c * unroll + u
            row = pl.multiple_of(idx_ref[0, 0, 0, r] * p, p)
            rows.append(row)
            vals.append(acc_ref[0, pl.ds(row, p), :] + ye_ref[0, 0, pl.ds(pl.multiple_of(r * p, p), p), :])
        for row, val in zip(rows, vals):
            acc_ref[0, pl.ds(row, p), :] = val
        return 0

    lax.fori_loop(0, slots // unroll, body, 0)


def _combine(idx, ye, tt):
    n_exp, batch, sp, _ = ye.shape
    slots = idx.shape[3]
    p = sp // slots
    kern = functools.partial(_combine_kernel, slots=slots, p=p)
    return pl.pallas_call(
        kern,
        grid=(batch, n_exp),
        in_specs=[pl.BlockSpec((1, 1, 1, slots), lambda b, e: (b, e, 0, 0), memory_space=pltpu.SMEM),
                  pl.BlockSpec((1, 1, sp, LANE), lambda b, e: (e, b, 0, 0))],
        out_specs=pl.BlockSpec((1, tt * p, LANE), lambda b, e: (b, 0, 0)),
        out_shape=jax.ShapeDtypeStruct((batch, tt * p, LANE), F32),
        compiler_params=_cp(("arbitrary", "arbitrary")),
        name="moe_combine",
    )(idx, ye)


def _final_kernel(moe_ref, x1_ref, mod_ref, g_ref, o_ref):
    p = x1_ref.shape[2] // LANE
    moe = jnp.concatenate([moe_ref[0, pl.ds(jj, TM, stride=p), :] for jj in range(p)], axis=1)
    o_ref[0] = x1_ref[0] + mod_ref[0][5:6, :] * _rms(moe, g_ref[3:4, :])


def _final(moe_s, x1, mod, g, n_ctx_tiles, skip_tiles):
    batch, tt, d = x1.shape
    p = d // LANE
    n_tiles = tt // TM - skip_tiles
    return pl.pallas_call(
        _final_kernel,
        grid=(batch, n_tiles),
        in_specs=[pl.BlockSpec((1, TM * p, LANE), lambda b, i: (b, i + skip_tiles, 0)),
                  pl.BlockSpec((1, TM, d), lambda b, i: (b, i + skip_tiles, 0)),
                  pl.BlockSpec((1, 6, d), lambda b, i: (jnp.where(i + skip_tiles < n_ctx_tiles, batch, b), 0, 0)),
                  pl.BlockSpec((4, d), lambda b, i: (0, 0))],
        out_specs=pl.BlockSpec((1, TM, d), lambda b, i: (b, i, 0)),
        out_shape=jax.ShapeDtypeStruct((batch, n_tiles * TM, d), F32),
        compiler_params=_cp(("arbitrary", "arbitrary")),
        name="post_ffn",
    )(moe_s, x1, mod, g)


def _rope_tables(ctx, seq):
    half = HEAD_DIM // 2
    inv = ROPE_BASE ** (-np.arange(0, half, 2, dtype=np.float32) / half)
    t = np.arange(seq)
    ang_r = (t // GRID_W).astype(np.float32)[:, None] * inv[None, :]
    ang_c = (t % GRID_W).astype(np.float32)[:, None] * inv[None, :]
    ang = jnp.asarray(np.concatenate([ang_r, ang_r, ang_c, ang_c], axis=1))
    sign = np.tile(np.concatenate([-np.ones(16), np.ones(16)]), 2).astype(np.float32)
    cos = jnp.concatenate([jnp.ones((ctx, HEAD_DIM), F32), jnp.cos(ang)], axis=0)
    sin = jnp.concatenate([jnp.zeros((ctx, HEAD_DIM), F32), jnp.sin(ang) * sign[None, :]], axis=0)
    return jnp.tile(cos, (1, LANE // HEAD_DIM)), jnp.tile(sin, (1, LANE // HEAD_DIM)), cos.T, sin.T


def _moe(h2s, logits, x1, mod, g, layer, w_gate, w_up, w_down, ctx, n_ctx_tiles, skip_tiles):
    batch, tt, d = x1.shape
    n_exp = w_gate.shape[1]
    seq = tt - ctx
    (idx_ctx, gate_ctx), (idx_lat, gate_lat) = _route(
        logits, n_exp, [(0, ctx, EC_CAPACITY * ctx // n_exp), (ctx, seq, EC_CAPACITY * seq // n_exp)])
    idx = jnp.concatenate([idx_ctx, idx_lat + ctx], axis=2)[:, :, None, :]
    gates = jnp.transpose(jnp.concatenate([gate_ctx, gate_lat], axis=2), (1, 0, 2))[..., None]
    xe = _gather(idx, h2s, tt, n_exp)
    ye = _ffn(xe, gates, w_gate, w_up, w_down, layer)
    moe_s = _combine(idx, ye, tt)
    return _final(moe_s, x1, mod, g, n_ctx_tiles, skip_tiles)


def kernel(x, c, ctx, c_ctx, ada_w, ada_b, norm_g, na_w_qkv, na_w_out, na_rel_bias, ml_w_in, ml_b_gate,
           ml_norm_w, ml_w_out, moe_w_router, moe_w_gate, moe_w_up, moe_w_down):
    batch, seq, d = x.shape
    n_ctx = ctx.shape[1]
    depth = ada_w.shape[0]
    rows = seq // GRID_W
    n_exp = moe_w_router.shape[-1]
    ml_heads = ml_norm_w.shape[-1] // ML_V_DIM
    assert n_ctx % TM == 0 and seq % TM == 0 and n_ctx % ML_CHUNK == 0 and seq % ML_CHUNK == 0
    assert rows >= K_ROWS and d % (2 * HEAD_DIM) == 0 and n_exp <= LANE and batch < 16
    n_ct = n_ctx // TM

    xa = (ctx, x)
    c_all = jnp.zeros((16, d), F32).at[:batch].set(c).at[batch].set(c_ctx)
    mod = _ada(c_all, ada_w, ada_b).reshape(depth, 16, 6, d)
    rope = _rope_tables(n_ctx, seq)

    for l in range(depth):
        last = l == depth - 1
        jx = l // 2
        g = norm_g[l]
        wr = jnp.zeros((d, LANE), F32).at[:, :n_exp].set(moe_w_router[l])
        wr_hi = wr.astype(BF16)
        wr2 = jnp.stack([wr_hi, (wr - wr_hi.astype(F32)).astype(BF16)])
        if l % 2 == 0:
            scale = HEAD_DIM ** -0.5
            w_qkv = na_w_qkv[jx]
            wk = w_qkv[:, d:2 * d].astype(BF16)
            wt = jnp.concatenate([w_qkv[:, :d] * (scale * LOG2E), w_qkv[:, 2 * d:]], axis=1).T.astype(BF16)
            k, qt, vt = _proj_na(xa, mod[l], g, wk, wt, n_ct)
            o = _na(k, qt, vt, _na_bias_table(na_rel_bias[jx]), n_ctx, rows)
            x1, h2s, logits = _post((o,), na_w_out[jx].astype(BF16), xa, mod[l], g, wr2, n_ct)
        else:
            qk_w = ml_heads * HEAD_DIM
            v_w = ml_heads * ML_V_DIM
            main_w = 2 * qk_w + 2 * v_w
            w_in = ml_w_in[jx]
            ng = 2 * ml_heads
            pad = jnp.zeros((d, LANE - ng), F32)
            w_gates = [w_in[:, main_w:main_w + ng], pad, w_in[:, main_w + ng:], pad]
            wk = jnp.concatenate([w_in[:, qk_w:2 * qk_w] * (HEAD_DIM ** -0.5)] + w_gates, axis=1).astype(BF16)
            wt = jnp.concatenate([w_in[:, :qk_w], w_in[:, 2 * qk_w:main_w]] + w_gates, axis=1).T.astype(BF16)
            zpad = jnp.zeros((LANE - ng,), F32)
            bg = jnp.concatenate([ml_b_gate[jx][:ng], zpad, ml_b_gate[jx][ng:], zpad])[None, :]
            bgt = jnp.broadcast_to(bg.T, (2 * LANE, TM))
            k, gates, qt, vt, ot, gates_t = _proj_ml(xa, mod[l], g, wk, wt, bg, bgt, rope, n_ct, qk_w, v_w)
            hf = _scan(qt, k, vt, gates, gates_t, ml_heads, n_ctx, reverse=False)
            hb = _scan(qt, k, vt, gates, gates_t, ml_heads, n_ctx, reverse=True)
            nw = jnp.broadcast_to(ml_norm_w[jx][:, None], (v_w, LANE))
            x1, h2s, logits = _post((hf, hb, ot, nw), ml_w_out[jx].astype(BF16), xa, mod[l], g,
                                    wr2, n_ct, ml_heads=ml_heads)
        xa = _moe(h2s, logits, x1, mod[l], g, l, moe_w_gate, moe_w_up, moe_w_down, n_ctx, n_ct,
                  n_ct if last else 0)
    return xa
```

```python
import functools

import numpy as np
import jax
import jax.numpy as jnp
from jax import lax
from jax.experimental import pallas as pl
from jax.experimental.pallas import tpu as pltpu

F32 = jnp.float32
BF16 = jnp.bfloat16

LANE = 128
SUBLANE = 8
VMEM_LIMIT = 56 * 1024 * 1024

EPS = 1e-6
NEG = -1e30
LOG2E = 1.4426950408889634
GRID_W = 64
HEAD_DIM = 64
ML_V_DIM = 128
WIN_ROWS = 8
WIN_COLS = 16
ROPE_BASE = 10000.0
EC_CAPACITY = 2

TM = 256
Q_ROWS = TM // GRID_W
K_ROWS = Q_ROWS + WIN_ROWS
ML_CHUNK = 256
NA_PAIRS = 4
RANK_RADIX = 32
RANK_HI = 16
TOKEN_SPLIT = 64


def _cp(sem):
    return pltpu.CompilerParams(dimension_semantics=sem, vmem_limit_bytes=VMEM_LIMIT)


def _dot(a, b):
    return jnp.dot(a, b, preferred_element_type=F32)


def _dot_nt(a, b):
    return lax.dot_general(a, b, (((1,), (1,)), ((), ())), preferred_element_type=F32)


def _dot_tn(a, b):
    return lax.dot_general(a, b, (((0,), (0,)), ((), ())), preferred_element_type=F32)


def _split3(x):
    x1 = x.astype(BF16)
    r1 = x - x1.astype(F32)
    x2 = r1.astype(BF16)
    x3 = (r1 - x2.astype(F32)).astype(BF16)
    return x1, x2, x3


def _rms(x, g):
    return x * lax.rsqrt(jnp.mean(x * x, axis=-1, keepdims=True) + EPS) * g


def _silu(x):
    return x * jax.nn.sigmoid(x)


def _ada_kernel(c_ref, w_ref, b_ref, o_ref):
    s = _silu(c_ref[...]).astype(BF16)
    o_ref[0] = _dot(s, w_ref[0].astype(BF16)) + b_ref[0]


def _ada(c_all, ada_w, ada_b):
    depth, d, n = ada_w.shape
    rows = c_all.shape[0]
    tn = 1024 if n % 1024 == 0 else n
    return pl.pallas_call(
        _ada_kernel,
        grid=(depth, n // tn),
        in_specs=[pl.BlockSpec((rows, d), lambda l, j: (0, 0)),
                  pl.BlockSpec((1, d, tn), lambda l, j: (l, 0, j)),
                  pl.BlockSpec((1, 1, tn), lambda l, j: (l, 0, j))],
        out_specs=pl.BlockSpec((1, rows, tn), lambda l, j: (l, 0, j)),
        out_shape=jax.ShapeDtypeStruct((depth, rows, n), F32),
        compiler_params=_cp(("arbitrary", "arbitrary")),
        name="ada_mod",
    )(c_all, ada_w, ada_b.reshape(depth, 1, n))


def _mod_spec(batch, n_ctx_tiles, d):
    return pl.BlockSpec((1, 6, d), lambda b, i: (jnp.where(i < n_ctx_tiles, batch, b), 0, 0))


def _stream_specs(xs, n_ct):
    if isinstance(xs, tuple):
        d = xs[0].shape[-1]
        return list(xs), [pl.BlockSpec((1, TM, d), lambda b, i: (b, jnp.minimum(i, n_ct - 1), 0)),
                          pl.BlockSpec((1, TM, d), lambda b, i: (b, jnp.maximum(i - n_ct, 0), 0))]
    return [xs], [pl.BlockSpec((1, TM, xs.shape[-1]), lambda b, i: (b, i, 0))]


def _stream_tile(x_refs, n_ct):
    if len(x_refs) == 1:
        return x_refs[0][0]
    return jnp.where(pl.program_id(1) < n_ct, x_refs[0][0], x_refs[1][0])


def _proj_na_kernel(*refs, n_x, n_ct):
    mod_ref, g_ref, wk_ref, wt_ref, k_ref, qt_ref, vt_ref = refs[n_x:]
    m = mod_ref[0]
    h = (_rms(_stream_tile(refs[:n_x], n_ct), g_ref[0:1, :]) * (1.0 + m[1:2, :]) + m[0:1, :]).astype(BF16)
    d = h.shape[1]
    k_ref[0] = _dot(h, wk_ref[...]).astype(BF16)
    rt = _dot_nt(wt_ref[...], h)
    qt_ref[0, 0] = rt[:d].astype(BF16)
    vt_ref[0, 0] = rt[d:].astype(BF16)


def _proj_na(xs, mod, g, wk, wt, n_ctx_tiles):
    x_args, x_specs = _stream_specs(xs, n_ctx_tiles)
    batch, _, d = x_args[0].shape
    tt = sum(a.shape[1] for a in x_args)
    n_tiles = tt // TM
    const = lambda a: pl.BlockSpec(a.shape, lambda b, i: (0,) * a.ndim)
    feat = pl.BlockSpec((1, 1, d, TM), lambda b, i: (b, i, 0, 0))
    return pl.pallas_call(
        functools.partial(_proj_na_kernel, n_x=len(x_args), n_ct=n_ctx_tiles),
        grid=(batch, n_tiles),
        in_specs=x_specs + [_mod_spec(batch, n_ctx_tiles, d), const(g), const(wk), const(wt)],
        out_specs=[pl.BlockSpec((1, TM, d), lambda b, i: (b, i, 0)), feat, feat],
        out_shape=[jax.ShapeDtypeStruct((batch, tt, d), BF16),
                   jax.ShapeDtypeStruct((batch, n_tiles, d, TM), BF16),
                   jax.ShapeDtypeStruct((batch, n_tiles, d, TM), BF16)],
        compiler_params=_cp(("arbitrary", "arbitrary")),
        name="proj_na",
    )(*x_args, mod, g, wk, wt)


def _log_sigmoid(f):
    return jnp.minimum(f, 0.0) - jnp.log1p(jnp.exp(-jnp.abs(f)))


def _proj_ml_kernel(x_ref, mod_ref, g_ref, wk_ref, wt_ref, bg_ref, bgt_ref, cos_ref, sin_ref, cost_ref, sint_ref,
                    k_ref, gate_ref, qt_ref, vt_ref, ot_ref, gatet_ref, *, qk_w, v_w):
    m = mod_ref[0]
    h = (_rms(x_ref[0], g_ref[0:1, :]) * (1.0 + m[1:2, :]) + m[0:1, :]).astype(BF16)
    r = _dot(h, wk_ref[...])
    k = r[:, :qk_w]
    reps = qk_w // LANE
    lane = lax.broadcasted_iota(jnp.int32, k.shape, 1)
    partner = jnp.where(lane % 32 < 16, pltpu.roll(k, qk_w - 16, 1), pltpu.roll(k, 16, 1))
    k = k * jnp.concatenate([cos_ref[...]] * reps, axis=1) + partner * jnp.concatenate([sin_ref[...]] * reps, axis=1)
    k_ref[0] = k.astype(BF16)
    gr = r[:, qk_w:] + bg_ref[...]
    gate_ref[0, :, :LANE] = gr[:, :LANE]
    gate_ref[0, :, LANE:] = _log_sigmoid(gr[:, LANE:])

    rt = _dot_nt(wt_ref[...], h)
    q = rt[:qk_w]
    reps = qk_w // HEAD_DIM
    row = lax.broadcasted_iota(jnp.int32, q.shape, 0)
    partner = jnp.where(row % 32 < 16, pltpu.roll(q, qk_w - 16, 0), pltpu.roll(q, 16, 0))
    q = q * jnp.concatenate([cost_ref[...]] * reps, axis=0) + partner * jnp.concatenate([sint_ref[...]] * reps, axis=0)
    qt_ref[0] = q.astype(BF16)
    vt_ref[0] = rt[qk_w:qk_w + v_w].astype(BF16)
    ot_ref[0] = rt[qk_w + v_w:qk_w + 2 * v_w].astype(BF16)
    gt = rt[qk_w + 2 * v_w:] + bgt_ref[...]
    gatet_ref[0, :LANE, :] = gt[:LANE]
    gatet_ref[0, LANE:, :] = _log_sigmoid(gt[LANE:])


def _proj_ml(xa, mod, g, wk, wt, bg, bgt, tables, n_ctx_tiles, qk_w, v_w):
    batch, tt, d = xa.shape
    cos_t, sin_t, cos_tt, sin_tt = tables
    kern = functools.partial(_proj_ml_kernel, qk_w=qk_w, v_w=v_w)
    const = lambda a: pl.BlockSpec(a.shape, lambda b, i: (0,) * a.ndim)
    tok = lambda w: pl.BlockSpec((1, TM, w), lambda b, i: (b, i, 0))
    feat = lambda w: pl.BlockSpec((1, w, TM), lambda b, i: (b, 0, i))
    return pl.pallas_call(
        kern,
        grid=(batch, tt // TM),
        in_specs=[tok(d), _mod_spec(batch, n_ctx_tiles, d), const(g), const(wk), const(wt), const(bg), const(bgt),
                  pl.BlockSpec((TM, LANE), lambda b, i: (i, 0)),
                  pl.BlockSpec((TM, LANE), lambda b, i: (i, 0)),
                  pl.BlockSpec((HEAD_DIM, TM), lambda b, i: (0, i)),
                  pl.BlockSpec((HEAD_DIM, TM), lambda b, i: (0, i))],
        out_specs=[tok(qk_w), tok(2 * LANE), feat(qk_w), feat(v_w), feat(v_w), feat(2 * LANE)],
        out_shape=[jax.ShapeDtypeStruct((batch, tt, qk_w), BF16),
                   jax.ShapeDtypeStruct((batch, tt, 2 * LANE), F32),
                   jax.ShapeDtypeStruct((batch, qk_w, tt), BF16),
                   jax.ShapeDtypeStruct((batch, v_w, tt), BF16),
                   jax.ShapeDtypeStruct((batch, v_w, tt), BF16),
                   jax.ShapeDtypeStruct((batch, 2 * LANE, tt), F32)],
        compiler_params=_cp(("arbitrary", "arbitrary")),
        name="proj_ml",
    )(xa, mod, g, wk, wt, bg, bgt, cos_t, sin_t, cos_tt, sin_tt)


def _na_kernel(qt_ref, k_ref, vt_ref, tab_ref, sel_ref, o_ref, *, ctx, rows, pairs):
    step = pl.program_id(2)
    n_ct = ctx // TM
    win_tiles = K_ROWS * GRID_W // TM
    ones_rows = jnp.where(lax.broadcasted_iota(jnp.int32, (16, TM), 0) == 0, 1.0, 0.0).astype(BF16)

    def attend(chunks):
        m, acc = None, None
        for s, v_tile in chunks:
            m_new = s.max(axis=0, keepdims=True)
            if m is not None:
                m_new = jnp.maximum(m, m_new)
            p = jnp.exp2(s - m_new).astype(BF16)
            part = _dot(jnp.concatenate([v_tile, ones_rows], axis=0), p)
            acc = part if acc is None else acc * jnp.exp2(m - m_new) + part
            m = m_new
        return acc[:HEAD_DIM] / acc[HEAD_DIM:HEAD_DIM + 1]

    def ctx_chunks(hh, q_t):
        sl = slice(hh * HEAD_DIM, (hh + 1) * HEAD_DIM)
        return [(_dot(k_ref[0, t * TM:(t + 1) * TM, sl], q_t), vt_ref[0, t, sl, :]) for t in range(n_ct)]

    @pl.when(step == 0)
    def _():
        for hd in range(2 * pairs):
            sl = slice(hd * HEAD_DIM, (hd + 1) * HEAD_DIM)
            o_ref[0, 0, sl, :] = attend(ctx_chunks(hd, qt_ref[0, 0, sl, :])).astype(BF16)

    @pl.when(step > 0)
    def _():
        rb = step - 1
        start = jnp.clip(Q_ROWS * rb - WIN_ROWS // 2, 0, rows - K_ROWS)
        t0 = n_ct + start // Q_ROWS
        ks = pl.multiple_of(t0 * TM, TM)
        kr = start + lax.broadcasted_iota(jnp.int32, (16, TM), 0)
        r = Q_ROWS * rb + lax.broadcasted_iota(jnp.int32, (16, TM), 1) // GRID_W
        r0 = jnp.clip(r - WIN_ROWS // 2, 0, rows - WIN_ROWS)
        pen = jnp.where((kr >= r0) & (kr < r0 + WIN_ROWS), 0.0, NEG).astype(BF16)
        pen = jnp.concatenate([pen, jnp.zeros((HEAD_DIM - 16, TM), BF16)], axis=0)
        first_half = lax.broadcasted_iota(jnp.int32, (win_tiles * TM, LANE), 1) < HEAD_DIM

        for hd in range(2 * pairs):
            hh = hd % 2
            sl = slice(hd * HEAD_DIM, (hd + 1) * HEAD_DIM)
            q_t = qt_ref[0, 0, sl, :]
            k_both = k_ref[0, pl.ds(ks, win_tiles * TM), (hd // 2) * LANE:(hd // 2 + 1) * LANE]
            if hh == 0:
                k_aug = jnp.where(first_half, k_both, sel_ref[...])
                q_aug = jnp.concatenate([q_t, pen], axis=0)
            else:
                k_aug = jnp.where(first_half, sel_ref[...], k_both)
                q_aug = jnp.concatenate([pen, q_t], axis=0)
            chunks = []
            for i in range(win_tiles):
                blocks = []
                for krl in range(i * Q_ROWS, (i + 1) * Q_ROWS):
                    tiles = []
                    for u in range(Q_ROWS // 2):
                        dr_e = start + krl - (Q_ROWS * rb + 2 * u) + WIN_ROWS - 1
                        tiles.append(tab_ref[hd, jnp.clip(dr_e, -1, 2 * WIN_ROWS - 1) + 1])
                    blocks.append(jnp.concatenate(tiles, axis=1))
                s_i = _dot(k_aug[i * TM:(i + 1) * TM], q_aug) + jnp.concatenate(blocks, axis=0)
                chunks.append((s_i, vt_ref[0, t0 + i, sl, :]))
            o_ref[0, 0, sl, :] = attend(ctx_chunks(hd, q_t) + chunks).astype(BF16)


def _na_bias_table(rel_bias):
    n_dr, n_dc = 2 * WIN_ROWS - 1, 2 * WIN_COLS - 1
    col = np.arange(GRID_W)
    c0 = np.clip(col - WIN_COLS // 2, 0, GRID_W - WIN_COLS)
    col_ok = (col[:, None] >= c0[None, :]) & (col[:, None] < c0[None, :] + WIN_COLS)
    dc = np.clip(col[:, None] - col[None, :] + WIN_COLS - 1, 0, n_dc - 1)
    oh = ((np.arange(n_dc)[:, None, None] == dc[None]) & col_ok[None]).astype(np.float32)
    cb = jnp.einsum('hrc,ckq->hrkq', rel_bias.astype(F32) * LOG2E, oh, precision=lax.Precision.HIGHEST)
    cb = jnp.where(col_ok[None, None], cb, NEG)
    neg = jnp.full((cb.shape[0], 2, GRID_W, GRID_W), NEG, F32)
    ext = jnp.concatenate([neg, cb, neg[:, :1]], axis=1)
    n_tiles = 2 * WIN_ROWS + 1
    return jnp.concatenate([ext[:, 1:1 + n_tiles], ext[:, 0:n_tiles]], axis=-1)


def _na(k, qt, vt, bias_tab, ctx, rows):
    batch, n_tiles, d, _ = qt.shape
    tt = k.shape[1]
    pairs = min(NA_PAIRS, d // LANE)
    gw = pairs * LANE
    hp = d // gw
    n_rb = rows // Q_ROWS
    n_ct = ctx // TM
    assert n_ct == 1
    q_tile = lambda h, b, s: (b, jnp.where(s == 0, 0, s - 1 + n_ct), h, 0)
    key_row = np.arange(K_ROWS * GRID_W)[:, None] // GRID_W
    key_row_sel = jnp.asarray(key_row == (np.arange(LANE)[None, :] % HEAD_DIM), BF16)
    kern = functools.partial(_na_kernel, ctx=ctx, rows=rows, pairs=pairs)
    return pl.pallas_call(
        kern,
        grid=(hp, batch, n_rb + 1),
        in_specs=[pl.BlockSpec((1, 1, gw, TM), q_tile),
                  pl.BlockSpec((1, tt, gw), lambda h, b, s: (b, 0, h)),
                  pl.BlockSpec((1, n_tiles, gw, TM), lambda h, b, s: (b, 0, h, 0)),
                  pl.BlockSpec((2 * pairs,) + bias_tab.shape[1:], lambda h, b, s: (h, 0, 0, 0)),
                  pl.BlockSpec(key_row_sel.shape, lambda h, b, s: (0, 0))],
        out_specs=pl.BlockSpec((1, 1, gw, TM), q_tile),
        out_shape=jax.ShapeDtypeStruct((batch, n_tiles, d, TM), BF16),
        compiler_params=_cp(("arbitrary", "arbitrary", "arbitrary")),
        name="na_attention",
    )(qt, k, vt, bias_tab, key_row_sel)


def _scan_kernel(qt_ref, k_ref, vt_ref, gate_ref, gatet_ref, o_ref, c_ref, m_ref, *, heads, reverse):
    L = ML_CHUNK
    j = pl.program_id(1)

    @pl.when(j == 0)
    def _():
        c_ref[...] = jnp.zeros_like(c_ref)
        m_ref[...] = jnp.full_like(m_ref, NEG)

    r_i = lax.broadcasted_iota(jnp.int32, (L, L), 0)
    c_i = lax.broadcasted_iota(jnp.int32, (L, L), 1)
    seen = (r_i >= c_i) if reverse else (r_i <= c_i)
    tri_row = jnp.where(seen, 1.0, 0.0).astype(BF16)
    tri_col = jnp.where((r_i <= c_i) if reverse else (r_i >= c_i), 1.0, 0.0).astype(BF16)

    f1, f2, f3 = _split3(gate_ref[0, :, LANE:])
    b_col = _dot(tri_col, f1) + _dot(tri_col, f2) + _dot(tri_col, f3)
    u_col = gate_ref[0, :, :LANE] - b_col
    f1, f2, f3 = _split3(gatet_ref[0, LANE:, :])
    b_row = _dot(f1, tri_row) + _dot(f2, tri_row) + _dot(f3, tri_row)
    i_row = gatet_ref[0, :LANE, :]
    last = 0 if reverse else L - 1
    ones_rows = jnp.where(lax.broadcasted_iota(jnp.int32, (ML_V_DIM, L), 0) == 0, 1.0, 0.0).astype(BF16)

    for h in range(heads):
        c = (heads if reverse else 0) + h
        br = b_row[c:c + 1, :]
        ir = i_row[c:c + 1, :]
        mh = m_ref[h]
        mh_row = jnp.concatenate([mh] * (L // LANE), axis=1)
        a_t = jnp.where(seen, jnp.broadcast_to(u_col[:, c:c + 1], (L, L)), NEG)
        mu = jnp.maximum(a_t.max(axis=0, keepdims=True), mh_row)
        p_t = jnp.exp(a_t - mu)
        w_inter = jnp.exp(mh_row - mu)
        kh = k_ref[0, :, h * HEAD_DIM:(h + 1) * HEAD_DIM]
        state = c_ref[h]
        q_t = qt_ref[0, h * HEAD_DIM:(h + 1) * HEAD_DIM, :]
        r1 = _dot(jnp.concatenate([kh, state.astype(BF16)], axis=0), q_t)
        sp = (r1[:L] * p_t).astype(BF16)
        v_aug = jnp.concatenate([vt_ref[0, h * ML_V_DIM:(h + 1) * ML_V_DIM, :], ones_rows], axis=0)
        rt = _dot(v_aug, sp) + w_inter * r1[L:]
        den = rt[ML_V_DIM:ML_V_DIM + 1]
        h_out = rt[:ML_V_DIM] / jnp.maximum(jnp.abs(den), jnp.exp(-(br + mu)))
        o_ref[0, h * ML_V_DIM:(h + 1) * ML_V_DIM, :] = h_out.astype(BF16)
        be = br[:, last:last + 1]
        m1 = mh[:, 0:1]
        g_row = be - br + ir
        m_new = jnp.maximum(be + m1, g_row.max(axis=1, keepdims=True))
        w_k = jnp.exp(g_row - m_new)
        decay = jnp.exp(be + m1 - m_new)
        c_ref[h] = decay * state + _dot((v_aug.astype(F32) * w_k).astype(BF16), kh)
        m_ref[h] = jnp.broadcast_to(m_new, (1, LANE))


def _scan(qt, k, vt, gates, gates_t, heads, ctx, reverse):
    batch, tt, qk_w = k.shape
    L = ML_CHUNK
    nc = tt // L
    ncc = ctx // L
    v_w = heads * ML_V_DIM

    def chunk(j):
        if not reverse:
            return j
        return jnp.where(j < ncc, ncc - 1 - j, nc - 1 - (j - ncc))

    kern = functools.partial(_scan_kernel, heads=heads, reverse=reverse)
    feat = lambda w: pl.BlockSpec((1, w, L), lambda b, j: (b, 0, chunk(j)))
    tok = lambda w: pl.BlockSpec((1, L, w), lambda b, j: (b, chunk(j), 0))
    return pl.pallas_call(
        kern,
        grid=(batch, nc),
        in_specs=[feat(qk_w), tok(qk_w), feat(v_w), tok(2 * LANE), feat(2 * LANE)],
        out_specs=feat(v_w),
        out_shape=jax.ShapeDtypeStruct((batch, v_w, tt), BF16),
        scratch_shapes=[pltpu.VMEM((heads, 2 * ML_V_DIM, HEAD_DIM), F32),
                        pltpu.VMEM((heads, 1, LANE), F32)],
        compiler_params=_cp(("arbitrary", "arbitrary")),
        name="mlstm_scan_bwd" if reverse else "mlstm_scan_fwd",
    )(qt, k, vt, gates, gates_t)


def _post_common(y, xa, mod_ref, g_ref, wr_ref, x1_ref, h2s_ref, lg_ref):
    m = mod_ref[0]
    x1 = xa + m[2:3, :] * _rms(y, g_ref[1:2, :])
    x1_ref[0] = x1
    h2 = _rms(x1, g_ref[2:3, :]) * (1.0 + m[4:5, :]) + m[3:4, :]
    d = h2.shape[1]
    p = d // LANE
    for jj in range(p):
        h2s_ref[0, pl.ds(jj, TM, stride=p), :] = h2[:, jj * LANE:(jj + 1) * LANE]
    h_hi = h2.astype(BF16)
    h_lo = (h2 - h_hi.astype(F32)).astype(BF16)
    r = _dot(jnp.concatenate([h_hi, h_lo], axis=0), wr_ref[...])
    lg_ref[0] = r[:TM, :LANE] + r[:TM, LANE:] + r[TM:, :LANE]


def _post_na_kernel(ot_ref, w_ref, *refs, n_x, n_ct):
    y = _dot_tn(ot_ref[0, 0], w_ref[...])
    _post_common(y, _stream_tile(refs[:n_x], n_ct), *refs[n_x:])


def _post_ml_kernel(hf_ref, hb_ref, og_ref, nw_ref, w_ref, xa_ref, mod_ref, g_ref, wr_ref,
                    x1_ref, h2s_ref, lg_ref, *, heads):
    hs = hf_ref[0].astype(F32) + hb_ref[0].astype(F32)
    parts = []
    for h in range(heads):
        t = hs[h * ML_V_DIM:(h + 1) * ML_V_DIM, :]
        parts.append(t * lax.rsqrt(jnp.mean(t * t, axis=0, keepdims=True) + EPS))
    nw = jnp.concatenate([nw_ref[...]] * (TM // LANE), axis=1)
    hn = jnp.concatenate(parts, axis=0) * nw
    y_in = (hn * jax.nn.sigmoid(og_ref[0].astype(F32))).astype(BF16)
    y = _dot_tn(y_in, w_ref[...])
    _post_common(y, xa_ref[0], mod_ref, g_ref, wr_ref, x1_ref, h2s_ref, lg_ref)


def _post(mixer_inputs, w_out, xs, mod, g, wr, n_ctx_tiles, ml_heads=None):
    x_args, x_specs = _stream_specs(xs, n_ctx_tiles)
    batch, _, d = x_args[0].shape
    tt = sum(a.shape[1] for a in x_args)
    p = d // LANE
    tile = lambda w: pl.BlockSpec((1, TM, w), lambda b, i: (b, i, 0))
    full2 = lambda a: pl.BlockSpec(a.shape, lambda b, i: (0,) * a.ndim)
    if ml_heads is None:
        (o,) = mixer_inputs
        kern = functools.partial(_post_na_kernel, n_x=len(x_args), n_ct=n_ctx_tiles)
        head_specs, head_args = [pl.BlockSpec((1, 1, d, TM), lambda b, i: (b, i, 0, 0))], [o]
    else:
        assert len(x_args) == 1
        hf, hb, og, nw = mixer_inputs
        v_w = ml_heads * ML_V_DIM
        kern = functools.partial(_post_ml_kernel, heads=ml_heads)
        feat = pl.BlockSpec((1, v_w, TM), lambda b, i: (b, 0, i))
        head_specs = [feat, feat, feat, full2(nw)]
        head_args = [hf, hb, og, nw]
    return pl.pallas_call(
        kern,
        grid=(batch, tt // TM),
        in_specs=head_specs + [full2(w_out)] + x_specs + [_mod_spec(batch, n_ctx_tiles, d), full2(g), full2(wr)],
        out_specs=[tile(d),
                   pl.BlockSpec((1, TM * p, LANE), lambda b, i: (b, i, 0)),
                   tile(LANE)],
        out_shape=[jax.ShapeDtypeStruct((batch, tt, d), F32),
                   jax.ShapeDtypeStruct((batch, tt * p, LANE), F32),
                   jax.ShapeDtypeStruct((batch, tt, LANE), F32)],
        compiler_params=_cp(("arbitrary", "arbitrary")),
        name="post_mixer",
    )(*head_args, w_out, *x_args, mod, g, wr)


def _route_kernel(lg_ref, *refs, n_exp, sets):
    for si, (off, t, cap) in enumerate(sets):
        _route_set(lg_ref[0, off:off + t, :], refs[2 * si], refs[2 * si + 1], refs[-1], n_exp, cap)


def _route_set(lg, idx_ref, gate_ref, pos_ref, n_exp, cap):
    t = lg.shape[0]
    n_tiles = t // LANE
    lt = lg.T[:n_exp, :]
    e = jnp.exp(lt - lt.max(axis=0, keepdims=True))
    aff = e / e.sum(axis=0, keepdims=True)
    def count_ge(v):
        return jnp.where(aff >= v, 1.0, 0.0).sum(axis=1, keepdims=True)

    def bisect(i, cur):
        cand = cur | jnp.left_shift(jnp.int32(1), 30 - i)
        return jnp.where(count_ge(pltpu.bitcast(cand, F32)) >= cap, cand, cur)

    v_bits = lax.fori_loop(0, 31, bisect, jnp.zeros((n_exp, 1), jnp.int32))

    def refine(_, hi):
        pivot = jnp.where(aff < hi, aff, -1.0).max(axis=1, keepdims=True)
        return jnp.where(count_ge(pivot) >= cap, hi, pivot)

    min_normal_bits = 0x00800000
    hi = lax.fori_loop(0, 3, refine, pltpu.bitcast(jnp.maximum(v_bits + 1, min_normal_bits), F32))
    thr = jnp.where(aff < hi, aff, -1.0).max(axis=1, keepdims=True)
    gt = aff > thr
    eq = aff == thr
    need = cap - jnp.where(gt, 1.0, 0.0).sum(axis=1, keepdims=True)

    upper = jnp.where(lax.broadcasted_iota(jnp.int32, (LANE, LANE), 0)
                      <= lax.broadcasted_iota(jnp.int32, (LANE, LANE), 1), 1.0, 0.0).astype(BF16)

    def excl_cumsum(mask_f32):
        carry = jnp.zeros((n_exp, 1), F32)
        parts = []
        for jt in range(n_tiles):
            tile = mask_f32[:, jt * LANE:(jt + 1) * LANE]
            inc = _dot(tile.astype(BF16), upper)
            parts.append(inc - tile + carry)
            carry = carry + inc[:, LANE - 1:LANE]
        return jnp.concatenate(parts, axis=1)

    eq_f = jnp.where(eq, 1.0, 0.0)
    sel = gt | (eq & (excl_cumsum(eq_f) < need))
    sel_f = jnp.where(sel, 1.0, 0.0)
    pos_ref[:, :t] = jnp.where(sel, excl_cumsum(sel_f), -1.0)

    hi_iota = lax.broadcasted_iota(jnp.int32, (RANK_HI, t), 0).astype(F32)
    lo_iota = lax.broadcasted_iota(jnp.int32, (RANK_RADIX, t), 0).astype(F32)
    t_idx = lax.broadcasted_iota(jnp.int32, (1, t), 1)
    t_hi = (t_idx // TOKEN_SPLIT).astype(F32)
    t_lo = (t_idx % TOKEN_SPLIT).astype(F32)
    for ex in range(n_exp):
        pos = pos_ref[ex:ex + 1, :t]
        hi = jnp.floor(pos * (1.0 / RANK_RADIX))
        lo = pos - hi * RANK_RADIX
        hit = hi_iota == hi
        parts = [t_hi, t_lo] + [a.astype(F32) for a in _split3(aff[ex:ex + 1, :])]
        lhs = jnp.concatenate([jnp.where(hit, v, 0.0) for v in parts], axis=0).astype(BF16)
        rhs = jnp.where(lo_iota == lo, 1.0, 0.0).astype(BF16)
        r = _dot_nt(lhs, rhs)
        idx_ref[0, ex] = (r[:RANK_HI] * TOKEN_SPLIT + r[RANK_HI:2 * RANK_HI]).astype(jnp.int32)
        gate_ref[0, ex] = r[2 * RANK_HI:3 * RANK_HI] + r[3 * RANK_HI:4 * RANK_HI] + r[4 * RANK_HI:]


def _route(logits, n_exp, sets):
    batch, tt, _ = logits.shape
    for _, t, cap in sets:
        assert cap % RANK_RADIX == 0 and cap <= RANK_HI * RANK_RADIX and t <= LANE * TOKEN_SPLIT
    kern = functools.partial(_route_kernel, n_exp=n_exp, sets=tuple(sets))
    out_spec = pl.BlockSpec((1, n_exp, RANK_HI, RANK_RADIX), lambda b: (b, 0, 0, 0))
    outs = pl.pallas_call(
        kern,
        grid=(batch,),
        in_specs=[pl.BlockSpec((1, tt, LANE), lambda b: (b, 0, 0))],
        out_specs=[out_spec, out_spec] * len(sets),
        out_shape=[jax.ShapeDtypeStruct((batch, n_exp, RANK_HI, RANK_RADIX), jnp.int32),
                   jax.ShapeDtypeStruct((batch, n_exp, RANK_HI, RANK_RADIX), F32)] * len(sets),
        scratch_shapes=[pltpu.VMEM((n_exp, max(t for _, t, _ in sets)), F32)],
        compiler_params=_cp(("arbitrary",)),
        name="route",
    )(logits)
    flat = lambda a, cap: a.reshape(batch, n_exp, RANK_HI * RANK_RADIX)[:, :, :cap]
    return [(flat(outs[2 * i], cap), flat(outs[2 * i + 1], cap)) for i, (_, _, cap) in enumerate(sets)]


def _gather_kernel(idx_ref, src_ref, xe_ref, tile_ref, *, slots, p, stride):
    unroll = 8

    def body(c, _):
        for u in range(unroll):
            r = c * unroll + u
            t = idx_ref[0, 0, 0, r]
            tile_ref[pl.ds(r, p, stride=stride), :] = src_ref[0, pl.ds(pl.multiple_of(t * p, p), p), :]
        return 0

    lax.fori_loop(0, slots // unroll, body, 0)
    xe_ref[0, 0] = jnp.concatenate([tile_ref[pl.ds(jj * stride, slots), :] for jj in range(p)], axis=1).astype(BF16)


def _gather(idx, h2s, tt, n_exp):
    batch, _, _, slots = idx.shape
    p = h2s.shape[1] // tt
    d = p * LANE
    stride = slots + SUBLANE
    kern = functools.partial(_gather_kernel, slots=slots, p=p, stride=stride)
    return pl.pallas_call(
        kern,
        grid=(batch, n_exp),
        in_specs=[pl.BlockSpec((1, 1, 1, slots), lambda b, e: (b, e, 0, 0), memory_space=pltpu.SMEM),
                  pl.BlockSpec((1, tt * p, LANE), lambda b, e: (b, 0, 0))],
        out_specs=pl.BlockSpec((1, 1, slots, d), lambda b, e: (e, b, 0, 0)),
        out_shape=jax.ShapeDtypeStruct((n_exp, batch, slots, d), BF16),
        scratch_shapes=[pltpu.VMEM((p * stride, LANE), F32)],
        compiler_params=_cp(("arbitrary", "arbitrary")),
        name="moe_gather",
    )(idx, h2s)


def _ffn_kernel(xe_ref, gate_ref, wg_hbm, wu_hbm, wd_hbm, ye_ref, wg_st, wu_st, wd_st, wgb_ref, wub_ref, wdb_ref,
                sem, *, layer, n_exp, f_chunk, cast_rows):
    e, b = pl.program_id(0), pl.program_id(1)
    stage = ((wg_hbm, wg_st, wgb_ref), (wu_hbm, wu_st, wub_ref), (wd_hbm, wd_st, wdb_ref))

    def weight_copies(ex):
        return [pltpu.make_async_copy(hbm.at[layer, ex], st, sem.at[i]) for i, (hbm, st, _) in enumerate(stage)]

    @pl.when((e == 0) & (b == 0))
    def _():
        for cp in weight_copies(0):
            cp.start()

    @pl.when(b == 0)
    def _():
        for cp in weight_copies(e):
            cp.wait()
        for _, st, dst in stage:
            def cast(r, _, st=st, dst=dst):
                rows = pl.ds(pl.multiple_of(r * cast_rows, cast_rows), cast_rows)
                dst[rows, :] = st[rows, :].astype(BF16)
                return 0
            lax.fori_loop(0, st.shape[0] // cast_rows, cast, 0)

        @pl.when(e + 1 < n_exp)
        def _():
            for cp in weight_copies(e + 1):
                cp.start()

    x = xe_ref[0, 0]
    slots, d = x.shape
    f = wgb_ref.shape[1]
    y = jnp.zeros((slots, d), F32)
    for c in range(f // f_chunk):
        cs = slice(c * f_chunk, (c + 1) * f_chunk)
        a = _dot(x, wgb_ref[:, cs])
        u = _dot(x, wub_ref[:, cs])
        y = y + _dot((_silu(a) * u).astype(BF16), wdb_ref[cs, :])
    y = y * gate_ref[0, 0]
    p = d // LANE
    for jj in range(p):
        ye_ref[0, 0, pl.ds(jj, slots, stride=p), :] = y[:, jj * LANE:(jj + 1) * LANE]


def _ffn(xe, gates, wg, wu, wd, layer):
    n_exp, batch, slots, d = xe.shape
    f = wg.shape[3]
    p = d // LANE
    kern = functools.partial(_ffn_kernel, layer=layer, n_exp=n_exp, f_chunk=min(f, 512), cast_rows=min(d, 256))
    hbm = pl.BlockSpec(memory_space=pl.ANY)
    return pl.pallas_call(
        kern,
        grid=(n_exp, batch),
        in_specs=[pl.BlockSpec((1, 1, slots, d), lambda e, b: (e, b, 0, 0)),
                  pl.BlockSpec((1, 1, slots, 1), lambda e, b: (e, b, 0, 0)),
                  hbm, hbm, hbm],
        out_specs=pl.BlockSpec((1, 1, slots * p, LANE), lambda e, b: (e, b, 0, 0)),
        out_shape=jax.ShapeDtypeStruct((n_exp, batch, slots * p, LANE), F32),
        scratch_shapes=[pltpu.VMEM((d, f), F32), pltpu.VMEM((d, f), F32), pltpu.VMEM((f, d), F32),
                        pltpu.VMEM((d, f), BF16), pltpu.VMEM((d, f), BF16), pltpu.VMEM((f, d), BF16),
                        pltpu.SemaphoreType.DMA((3,))],
        compiler_params=_cp(("arbitrary", "arbitrary")),
        name="moe_ffn",
    )(xe, gates, wg, wu, wd)


def _combine_kernel(idx_ref, ye_ref, acc_ref, *, slots, p):
    @pl.when(pl.program_id(1) == 0)
    def _():
        acc_ref[...] = jnp.zeros_like(acc_ref)

    unroll = 4

    def body(c, _):
        rows, vals = [], []
        for u in range(unroll):
            r = c * unroll + u
            row = pl.multiple_of(idx_ref[0, 0, 0, r] * p, p)
            rows.append(row)
            vals.append(acc_ref[0, pl.ds(row, p), :] + ye_ref[0, 0, pl.ds(pl.multiple_of(r * p, p), p), :])
        for row, val in zip(rows, vals):
            acc_ref[0, pl.ds(row, p), :] = val
        return 0

    lax.fori_loop(0, slots // unroll, body, 0)


def _combine(idx, ye, tt):
    n_exp, batch, sp, _ = ye.shape
    slots = idx.shape[3]
    p = sp // slots
    kern = functools.partial(_combine_kernel, slots=slots, p=p)
    return pl.pallas_call(
        kern,
        grid=(batch, n_exp),
        in_specs=[pl.BlockSpec((1, 1, 1, slots), lambda b, e: (b, e, 0, 0), memory_space=pltpu.SMEM),
                  pl.BlockSpec((1, 1, sp, LANE), lambda b, e: (e, b, 0, 0))],
        out_specs=pl.BlockSpec((1, tt * p, LANE), lambda b, e: (b, 0, 0)),
        out_shape=jax.ShapeDtypeStruct((batch, tt * p, LANE), F32),
        compiler_params=_cp(("arbitrary", "arbitrary")),
        name="moe_combine",
    )(idx, ye)


def _final_kernel(moe_ref, x1_ref, mod_ref, g_ref, o_ref):
    p = x1_ref.shape[2] // LANE
    moe = jnp.concatenate([moe_ref[0, pl.ds(jj, TM, stride=p), :] for jj in range(p)], axis=1)
    o_ref[0] = x1_ref[0] + mod_ref[0][5:6, :] * _rms(moe, g_ref[3:4, :])


def _final(moe_s, x1, mod, g, n_ctx_tiles, skip_tiles):
    batch, tt, d = x1.shape
    p = d // LANE
    n_tiles = tt // TM - skip_tiles
    return pl.pallas_call(
        _final_kernel,
        grid=(batch, n_tiles),
        in_specs=[pl.BlockSpec((1, TM * p, LANE), lambda b, i: (b, i + skip_tiles, 0)),
                  pl.BlockSpec((1, TM, d), lambda b, i: (b, i + skip_tiles, 0)),
                  pl.BlockSpec((1, 6, d), lambda b, i: (jnp.where(i + skip_tiles < n_ctx_tiles, batch, b), 0, 0)),
                  pl.BlockSpec((4, d), lambda b, i: (0, 0))],
        out_specs=pl.BlockSpec((1, TM, d), lambda b, i: (b, i, 0)),
        out_shape=jax.ShapeDtypeStruct((batch, n_tiles * TM, d), F32),
        compiler_params=_cp(("arbitrary", "arbitrary")),
        name="post_ffn",
    )(moe_s, x1, mod, g)


def _rope_tables(ctx, seq):
    half = HEAD_DIM // 2
    inv = ROPE_BASE ** (-np.arange(0, half, 2, dtype=np.float32) / half)
    t = np.arange(seq)
    ang_r = (t // GRID_W).astype(np.float32)[:, None] * inv[None, :]
    ang_c = (t % GRID_W).astype(np.float32)[:, None] * inv[None, :]
    ang = jnp.asarray(np.concatenate([ang_r, ang_r, ang_c, ang_c], axis=1))
    sign = np.tile(np.concatenate([-np.ones(16), np.ones(16)]), 2).astype(np.float32)
    cos = jnp.concatenate([jnp.ones((ctx, HEAD_DIM), F32), jnp.cos(ang)], axis=0)
    sin = jnp.concatenate([jnp.zeros((ctx, HEAD_DIM), F32), jnp.sin(ang) * sign[None, :]], axis=0)
    return jnp.tile(cos, (1, LANE // HEAD_DIM)), jnp.tile(sin, (1, LANE // HEAD_DIM)), cos.T, sin.T


def _moe(h2s, logits, x1, mod, g, layer, w_gate, w_up, w_down, ctx, n_ctx_tiles, skip_tiles):
    batch, tt, d = x1.shape
    n_exp = w_gate.shape[1]
    seq = tt - ctx
    (idx_ctx, gate_ctx), (idx_lat, gate_lat) = _route(
        logits, n_exp, [(0, ctx, EC_CAPACITY * ctx // n_exp), (ctx, seq, EC_CAPACITY * seq // n_exp)])
    idx = jnp.concatenate([idx_ctx, idx_lat + ctx], axis=2)[:, :, None, :]
    gates = jnp.transpose(jnp.concatenate([gate_ctx, gate_lat], axis=2), (1, 0, 2))[..., None]
    xe = _gather(idx, h2s, tt, n_exp)
    ye = _ffn(xe, gates, w_gate, w_up, w_down, layer)
    moe_s = _combine(idx, ye, tt)
    return _final(moe_s, x1, mod, g, n_ctx_tiles, skip_tiles)


def kernel(x, c, ctx, c_ctx, ada_w, ada_b, norm_g, na_w_qkv, na_w_out, na_rel_bias, ml_w_in, ml_b_gate,
           ml_norm_w, ml_w_out, moe_w_router, moe_w_gate, moe_w_up, moe_w_down):
    batch, seq, d = x.shape
    n_ctx = ctx.shape[1]
    depth = ada_w.shape[0]
    rows = seq // GRID_W
    n_exp = moe_w_router.shape[-1]
    ml_heads = ml_norm_w.shape[-1] // ML_V_DIM
    assert n_ctx % TM == 0 and seq % TM == 0 and n_ctx % ML_CHUNK == 0 and seq % ML_CHUNK == 0
    assert rows >= K_ROWS and d % (2 * HEAD_DIM) == 0 and n_exp <= LANE and batch < 16
    n_ct = n_ctx // TM

    xa = (ctx, x)
    c_all = jnp.zeros((16, d), F32).at[:batch].set(c).at[batch].set(c_ctx)
    mod = _ada(c_all, ada_w, ada_b).reshape(depth, 16, 6, d)
    rope = _rope_tables(n_ctx, seq)

    for l in range(depth):
        last = l == depth - 1
        jx = l // 2
        g = norm_g[l]
        wr = jnp.zeros((d, LANE), F32).at[:, :n_exp].set(moe_w_router[l])
        wr_hi = wr.astype(BF16)
        wr2 = jnp.concatenate([wr_hi, (wr - wr_hi.astype(F32)).astype(BF16)], axis=1)
        if l % 2 == 0:
            scale = HEAD_DIM ** -0.5
            w_qkv = na_w_qkv[jx]
            wk = w_qkv[:, d:2 * d].astype(BF16)
            wt = jnp.concatenate([w_qkv[:, :d] * (scale * LOG2E), w_qkv[:, 2 * d:]], axis=1).T.astype(BF16)
            k, qt, vt = _proj_na(xa, mod[l], g, wk, wt, n_ct)
            o = _na(k, qt, vt, _na_bias_table(na_rel_bias[jx]), n_ctx, rows)
            x1, h2s, logits = _post((o,), na_w_out[jx].astype(BF16), xa, mod[l], g, wr2, n_ct)
        else:
            qk_w = ml_heads * HEAD_DIM
            v_w = ml_heads * ML_V_DIM
            main_w = 2 * qk_w + 2 * v_w
            w_in = ml_w_in[jx]
            ng = 2 * ml_heads
            pad = jnp.zeros((d, LANE - ng), F32)
            w_gates = [w_in[:, main_w:main_w + ng], pad, w_in[:, main_w + ng:], pad]
            wk = jnp.concatenate([w_in[:, qk_w:2 * qk_w] * (HEAD_DIM ** -0.5)] + w_gates, axis=1).astype(BF16)
            wt = jnp.concatenate([w_in[:, :qk_w], w_in[:, 2 * qk_w:main_w]] + w_gates, axis=1).T.astype(BF16)
            zpad = jnp.zeros((LANE - ng,), F32)
            bg = jnp.concatenate([ml_b_gate[jx][:ng], zpad, ml_b_gate[jx][ng:], zpad])[None, :]
            bgt = jnp.broadcast_to(bg.T, (2 * LANE, TM))
            k, gates, qt, vt, ot, gates_t = _proj_ml(xa, mod[l], g, wk, wt, bg, bgt, rope, n_ct, qk_w, v_w)
            hf = _scan(qt, k, vt, gates, gates_t, ml_heads, n_ctx, reverse=False)
            hb = _scan(qt, k, vt, gates, gates_t, ml_heads, n_ctx, reverse=True)
            nw = jnp.broadcast_to(ml_norm_w[jx][:, None], (v_w, LANE))
            x1, h2s, logits = _post((hf, hb, ot, nw), ml_w_out[jx].astype(BF16), xa, mod[l], g,
                                    wr2, n_ct, ml_heads=ml_heads)
        xa = _moe(h2s, logits, x1, mod[l], g, l, moe_w_gate, moe_w_up, moe_w_down, n_ctx, n_ct,
                  n_ct if last else 0)
    return xa
```
